```python
import math
import jax, jax.numpy as jnp
from jax import lax
import numpy as np

D_MODEL = 2048
BATCH = 8
SEQ = 4096
DEPTH = 4

SSM_WIDTH = D_MODEL // 2
SSM_GROUP = 16
SSM_GROUPS = SSM_WIDTH // SSM_GROUP
SSM_STATE = 64
ATTN_WIDTH = D_MODEL - SSM_WIDTH
HEAD_DIM = 64
ATTN_HEADS = ATTN_WIDTH // (2 * HEAD_DIM)
MIX_WIDTH = SSM_WIDTH + ATTN_WIDTH
IN_WIDTH = 2 * SSM_WIDTH + 4 * ATTN_WIDTH
ROPE_THETA = 10000.0
Q_BLOCK = 128
NORM_EPS = 1e-6
DT_MIN = 1e-3
DT_MAX = 1e-1

kernel_name = "hybrid_s5_diffattn_adaln_trunk"


def rms_norm(x, g):
    xf = x.astype(jnp.float32)
    y = xf * lax.rsqrt(jnp.mean(xf * xf, axis=-1, keepdims=True) + NORM_EPS)
    return (y * g.astype(jnp.float32)).astype(x.dtype)


def rope_tables(positions):
    half = HEAD_DIM // 2
    inv_freq = ROPE_THETA ** (-jnp.arange(half, dtype=jnp.float32) / half)
    ang = positions.astype(jnp.float32)[..., None] * inv_freq
    return jnp.cos(ang)[:, :, None, None, :], jnp.sin(ang)[:, :, None, None, :]


def apply_rope(t, cos, sin):
    half = HEAD_DIM // 2
    tf = t.astype(jnp.float32)
    t1, t2 = tf[..., :half], tf[..., half:]
    return jnp.concatenate([t1 * cos - t2 * sin, t2 * cos + t1 * sin], axis=-1).astype(t.dtype)


def s5_mixer(u, a_re, a_im, b_re, b_im, c_re, c_im, d_skip, log_step, w_glu, b_glu):
    bsz, s = u.shape[0], u.shape[1]
    uf = u.astype(jnp.float32).reshape(bsz, s, SSM_GROUPS, SSM_GROUP)
    lam = lax.complex(a_re.astype(jnp.float32), a_im.astype(jnp.float32))
    step = jnp.exp(log_step.astype(jnp.float32))[:, None]
    a_bar = jnp.exp(lam * step)
    b_mat = lax.complex(b_re.astype(jnp.float32), b_im.astype(jnp.float32))
    b_bar = ((a_bar - 1.0) / lam)[..., None] * b_mat
    bu = jnp.einsum('gnp,bsgp->bsgn', b_bar, uf)
    a_seq = jnp.broadcast_to(a_bar, (1, s, SSM_GROUPS, SSM_STATE))

    def combine(left, right):
        a_l, b_l = left
        a_r, b_r = right
        return a_r * a_l, a_r * b_l + b_r

    _, states = lax.associative_scan(combine, (a_seq, bu), axis=1)
    c_mat = lax.complex(c_re.astype(jnp.float32), c_im.astype(jnp.float32))
    y = jnp.einsum('gpn,bsgn->bsgp', c_mat, states).real + d_skip.astype(jnp.float32) * uf
    y = jax.nn.gelu(y.reshape(bsz, s, SSM_WIDTH))
    ab = y @ w_glu.astype(jnp.float32) + b_glu.astype(jnp.float32)
    out = ab[..., :SSM_WIDTH] * jax.nn.sigmoid(ab[..., SSM_WIDTH:])
    return out.astype(u.dtype)


def diff_attention(q, k, v, lam, sub_g, lambda_init):
    bsz, s = q.shape[0], q.shape[1]
    nblk = s // Q_BLOCK
    scale = HEAD_DIM ** -0.5
    qb = q.reshape(bsz, nblk, Q_BLOCK, ATTN_HEADS, 2, HEAD_DIM).transpose(1, 0, 2, 3, 4, 5)
    vf = v.astype(jnp.float32)
    key_pos = jnp.arange(s)

    def block(args):
        q_blk, i = args
        sc = jnp.einsum('bqhmd,bkhmd->bhmqk', q_blk, k,
                        preferred_element_type=jnp.float32) * scale
        q_pos = i * Q_BLOCK + jnp.arange(Q_BLOCK)
        mask = key_pos[None, :] <= q_pos[:, None]
        sc = jnp.where(mask, sc, -jnp.inf)
        p = jax.nn.softmax(sc, axis=-1)
        w = p[:, :, 0] - lam * p[:, :, 1]
        return jnp.einsum('bhqk,bkhe->bqhe', w, vf)

    out = lax.map(block, (qb, jnp.arange(nblk)))
    out = out.transpose(1, 0, 2, 3, 4).reshape(bsz, s, ATTN_HEADS, 2 * HEAD_DIM)
    out = rms_norm(out, sub_g) * (1.0 - lambda_init)
    return out.astype(v.dtype)


def setup_inputs(seed: int = 0) -> dict:
    key = jax.random.key(seed)
    ks = jax.random.split(key, 24)
    f32 = jnp.float32
    x = jax.random.normal(ks[0], (BATCH, SEQ, D_MODEL), f32)
    c = jax.random.normal(ks[1], (BATCH, D_MODEL), f32)
    offset = jax.random.randint(ks[2], (BATCH, 1), 0, 2048, dtype=jnp.int32)
    positions = (offset + jnp.arange(SEQ, dtype=jnp.int32)[None, :]).astype(jnp.int32)
    norm_g = 1.0 + 0.02 * jax.random.normal(ks[3], (DEPTH, D_MODEL), f32)
    w_ada = 0.5 * D_MODEL ** -0.5 * jax.random.normal(ks[4], (DEPTH, D_MODEL, 3 * D_MODEL), f32)
    b_ada = 0.02 * jax.random.normal(ks[5], (DEPTH, 3 * D_MODEL), f32)
    w_in = D_MODEL ** -0.5 * jax.random.normal(ks[6], (DEPTH, D_MODEL, IN_WIDTH), f32)
    w_out = MIX_WIDTH ** -0.5 * jax.random.normal(ks[7], (DEPTH, MIX_WIDTH, D_MODEL), f32)
    ssm_a_re = -0.5 + 0.01 * jax.random.normal(ks[8], (DEPTH, SSM_GROUPS, SSM_STATE), f32)
    ssm_a_im = (math.pi * jnp.arange(SSM_STATE, dtype=f32))[None, None, :] \
        + 0.01 * jax.random.normal(ks[9], (DEPTH, SSM_GROUPS, SSM_STATE), f32)
    ssm_b_re = (2 * SSM_GROUP) ** -0.5 * jax.random.normal(ks[10], (DEPTH, SSM_GROUPS, SSM_STATE, SSM_GROUP), f32)
    ssm_b_im = (2 * SSM_GROUP) ** -0.5 * jax.random.normal(ks[11], (DEPTH, SSM_GROUPS, SSM_STATE, SSM_GROUP), f32)
    ssm_c_re = (2 * SSM_STATE) ** -0.5 * jax.random.normal(ks[12], (DEPTH, SSM_GROUPS, SSM_GROUP, SSM_STATE), f32)
    ssm_c_im = (2 * SSM_STATE) ** -0.5 * jax.random.normal(ks[13], (DEPTH, SSM_GROUPS, SSM_GROUP, SSM_STATE), f32)
    ssm_d = jax.random.normal(ks[14], (DEPTH, SSM_GROUPS, SSM_GROUP), f32)
    ssm_log_step = jax.random.uniform(ks[15], (DEPTH, SSM_GROUPS), f32,
                                      minval=math.log(DT_MIN), maxval=math.log(DT_MAX))
    w_glu = SSM_WIDTH ** -0.5 * jax.random.normal(ks[16], (DEPTH, SSM_WIDTH, 2 * SSM_WIDTH), f32)
    b_glu = 0.02 * jax.random.normal(ks[17], (DEPTH, 2 * SSM_WIDTH), f32)
    lam_q1 = 0.1 * jax.random.normal(ks[18], (DEPTH, HEAD_DIM), f32)
    lam_k1 = 0.1 * jax.random.normal(ks[19], (DEPTH, HEAD_DIM), f32)
    lam_q2 = 0.1 * jax.random.normal(ks[20], (DEPTH, HEAD_DIM), f32)
    lam_k2 = 0.1 * jax.random.normal(ks[21], (DEPTH, HEAD_DIM), f32)
    sub_g = 1.0 + 0.02 * jax.random.normal(ks[22], (DEPTH, 2 * HEAD_DIM), f32)
    final_g = 1.0 + 0.02 * jax.random.normal(ks[23], (D_MODEL,), f32)
    return {"x": x, "c": c, "positions": positions, "norm_g": norm_g, "w_ada": w_ada, "b_ada": b_ada,
            "w_in": w_in, "w_out": w_out, "ssm_a_re": ssm_a_re, "ssm_a_im": ssm_a_im,
            "ssm_b_re": ssm_b_re, "ssm_b_im": ssm_b_im, "ssm_c_re": ssm_c_re, "ssm_c_im": ssm_c_im,
            "ssm_d": ssm_d, "ssm_log_step": ssm_log_step, "w_glu": w_glu, "b_glu": b_glu,
            "lam_q1": lam_q1, "lam_k1": lam_k1, "lam_q2": lam_q2, "lam_k2": lam_k2,
            "sub_g": sub_g, "final_g": final_g}


def reference(x, c, positions, norm_g, w_ada, b_ada, w_in, w_out, ssm_a_re, ssm_a_im,
              ssm_b_re, ssm_b_im, ssm_c_re, ssm_c_im, ssm_d, ssm_log_step, w_glu, b_glu,
              lam_q1, lam_k1, lam_q2, lam_k2, sub_g, final_g):
    bsz, s = x.shape[0], x.shape[1]
    cos, sin = rope_tables(positions)
    c_act = jax.nn.silu(c)
    splits = [SSM_WIDTH, 2 * SSM_WIDTH, 2 * SSM_WIDTH + ATTN_WIDTH,
              2 * SSM_WIDTH + 2 * ATTN_WIDTH, 2 * SSM_WIDTH + 3 * ATTN_WIDTH]
    for l in range(DEPTH):
        lambda_init = 0.8 - 0.6 * math.exp(-0.3 * l)
        mod = c_act @ w_ada[l] + b_ada[l]
        shift, scale, gate = jnp.split(mod, 3, axis=-1)
        h = rms_norm(x, norm_g[l]) * (1.0 + scale[:, None, :]) + shift[:, None, :]
        proj = h @ w_in[l]
        u, z_s, q, k, v, z_a = jnp.split(proj, splits, axis=-1)
        y_s = s5_mixer(u, ssm_a_re[l], ssm_a_im[l], ssm_b_re[l], ssm_b_im[l], ssm_c_re[l],
                       ssm_c_im[l], ssm_d[l], ssm_log_step[l], w_glu[l], b_glu[l]) * jax.nn.silu(z_s)
        q = apply_rope(q.reshape(bsz, s, ATTN_HEADS, 2, HEAD_DIM), cos, sin)
        k = apply_rope(k.reshape(bsz, s, ATTN_HEADS, 2, HEAD_DIM), cos, sin)
        v = v.reshape(bsz, s, ATTN_HEADS, 2 * HEAD_DIM)
        lam = (jnp.exp(jnp.sum(lam_q1[l].astype(jnp.float32) * lam_k1[l].astype(jnp.float32)))
               - jnp.exp(jnp.sum(lam_q2[l].astype(jnp.float32) * lam_k2[l].astype(jnp.float32)))
               + lambda_init)
        y_a = diff_attention(q, k, v, lam, sub_g[l], lambda_init).reshape(bsz, s, ATTN_WIDTH)
        y_a = y_a * jax.nn.silu(z_a)
        y = jnp.concatenate([y_s, y_a.astype(y_s.dtype)], axis=-1) @ w_out[l]
        x = x + gate[:, None, :] * y
    return rms_norm(x, final_g)
```

```python
import functools
import math

import jax
import jax.numpy as jnp
import numpy as np
from jax import lax
from jax.experimental import pallas as pl
from jax.experimental.pallas import tpu as pltpu

LANES = 128
SUBLANES = 8
V7X_VMEM_REQUEST_BYTES = 56 * 1024 * 1024

SSM_GROUP = 16
SSM_STATE = 64
HEAD_DIM = 64
ROPE_THETA = 10000.0
NORM_EPS = 1e-6
CHUNK = 16
PAIR = 4 * HEAD_DIM

F32 = jnp.float32
BF16 = jnp.bfloat16


def _params(*sem):
    return pltpu.CompilerParams(dimension_semantics=sem, vmem_limit_bytes=V7X_VMEM_REQUEST_BYTES)


def _ada_kernel(c_ref, w_ref, b_ref, o_ref):
    c = c_ref[...]
    act = c * jax.nn.sigmoid(c)
    o_ref[...] = jnp.dot(act, w_ref[...], preferred_element_type=F32,
                         precision=lax.Precision.HIGHEST) + b_ref[...]


def _ada_mod(c, w_ada, b_ada):
    depth, d, n3 = w_ada.shape
    bsz = c.shape[0]
    tn = _tile(n3, 1024)
    return pl.pallas_call(
        _ada_kernel,
        grid=(depth, n3 // tn),
        in_specs=[pl.BlockSpec((bsz, d), lambda l, n: (0, 0)),
                  pl.BlockSpec((None, d, tn), lambda l, n: (l, 0, n)),
                  pl.BlockSpec((None, 1, tn), lambda l, n: (l, 0, n))],
        out_specs=pl.BlockSpec((None, bsz, tn), lambda l, n: (l, 0, n)),
        out_shape=jax.ShapeDtypeStruct((depth, bsz, n3), F32),
        compiler_params=_params("arbitrary", "arbitrary"),
        name="ada_mod",
    )(c, w_ada, b_ada.reshape(depth, 1, n3))


def _in_proj_kernel(x_ref, shift_ref, scale_ref, g_ref, w_ref, cos_ref, sin_ref, o_ref, h_ref, *, npairs):
    n = pl.program_id(2)

    @pl.when(n == 0)
    def _():
        xf = x_ref[...]
        y = xf * lax.rsqrt(jnp.mean(xf * xf, axis=-1, keepdims=True) + NORM_EPS) * g_ref[...]
        h_ref[...] = (y * (1.0 + scale_ref[...]) + shift_ref[...]).astype(BF16)

    acc = jnp.dot(h_ref[...], w_ref[...], preferred_element_type=F32)
    is_rope = jnp.logical_or(n == 2, n == 3)

    @pl.when(is_rope)
    def _():
        qs = jnp.where(n == 2, HEAD_DIM ** -0.5, 1.0).astype(F32)
        cs = cos_ref[...] * qs
        sn = sin_ref[...] * qs
        for j in range(npairs):
            lo = j * PAIR
            t1 = acc[:, lo:lo + LANES]
            t2 = acc[:, lo + LANES:lo + PAIR]
            o_ref[:, lo:lo + LANES] = (t1 * cs - t2 * sn).astype(BF16)
            o_ref[:, lo + LANES:lo + PAIR] = (t2 * cs + t1 * sn).astype(BF16)

    @pl.when(jnp.logical_not(is_rope))
    def _():
        o_ref[...] = acc.astype(BF16)


def _in_proj(x, shift, scale, norm_g, w_in_p, layer, cos_t, sin_t, tm):
    bsz, s, d = x.shape
    width = w_in_p.shape[-1] // 6
    npairs = width // PAIR
    return pl.pallas_call(
        functools.partial(_in_proj_kernel, npairs=npairs),
        grid=(bsz, s // tm, 6),
        in_specs=[pl.BlockSpec((None, tm, d), lambda b, m, n: (b, m, 0)),
                  pl.BlockSpec((None, 1, d), lambda b, m, n: (b, 0, 0)),
                  pl.BlockSpec((None, 1, d), lambda b, m, n: (b, 0, 0)),
                  pl.BlockSpec((None, 1, d), lambda b, m, n: (layer, 0, 0)),
                  pl.BlockSpec((None, d, width), lambda b, m, n: (layer, 0, n)),
                  pl.BlockSpec((None, tm, LANES), lambda b, m, n: (b, m, 0)),
                  pl.BlockSpec((None, tm, LANES), lambda b, m, n: (b, m, 0))],
        out_specs=pl.BlockSpec((None, tm, width), lambda b, m, n: (b, m, n)),
        out_shape=jax.ShapeDtypeStruct((bsz, s, 6 * width), BF16),
        scratch_shapes=[pltpu.VMEM((tm, d), BF16)],
        compiler_params=_params("arbitrary", "arbitrary", "arbitrary"),
        name="in_proj",
    )(x, shift, scale, norm_g, w_in_p, cos_t, sin_t)


def _gelu_tanh(y):
    return 0.5 * y * (1.0 + jnp.tanh(math.sqrt(2.0 / math.pi) * (y + 0.044715 * (y * y * y))))


def _s5_kernel(a_ref, toep_ref, bst_ref, cst_ref, av_ref, o_ref, sb_ref, xp_ref, *, bsz, nchunks):
    a = a_ref[...]
    sb_ref[...] = jnp.dot(a, bst_ref[...], preferred_element_type=F32)
    ar = av_ref[0:1, :]
    ai1 = av_ref[1:2, :]
    ai2 = av_ref[2:3, :]
    ns2 = 2 * SSM_STATE

    def body(c, carry):
        xs, xw = carry
        r = pl.multiple_of(c * bsz, bsz)
        xp_ref[pl.ds(r, bsz), :] = xs
        sb = sb_ref[pl.ds(r, bsz), :]
        return (ar * xs + ai1 * xw + sb[:, :ns2], ar * xw + ai2 * xs + sb[:, ns2:])

    zero = jnp.zeros((bsz, ns2), F32)
    lax.fori_loop(0, nchunks, body, (zero, zero), unroll=8)
    y = jnp.dot(a, toep_ref[...], preferred_element_type=F32)
    y = y + jnp.dot(xp_ref[...].astype(BF16), cst_ref[...], preferred_element_type=F32)
    o_ref[...] = _gelu_tanh(y).astype(BF16)


def _s5_core(a, toep, bst, cst, avec, layer, bsz):
    g, m, kk = a.shape
    ns2 = 2 * SSM_STATE
    return pl.pallas_call(
        functools.partial(_s5_kernel, bsz=bsz, nchunks=m // bsz),
        grid=(g,),
        in_specs=[pl.BlockSpec((None, m, kk), lambda i: (i, 0, 0)),
                  pl.BlockSpec((None, None, kk, kk), lambda i: (layer, i, 0, 0)),
                  pl.BlockSpec((None, None, kk, 2 * ns2), lambda i: (layer, i, 0, 0)),
                  pl.BlockSpec((None, None, ns2, kk), lambda i: (layer, i, 0, 0)),
                  pl.BlockSpec((None, None, 4, ns2), lambda i: (layer, i, 0, 0))],
        out_specs=pl.BlockSpec((None, m, kk), lambda i: (i, 0, 0)),
        out_shape=jax.ShapeDtypeStruct((g, m, kk), BF16),
        scratch_shapes=[pltpu.VMEM((m, 2 * ns2), F32), pltpu.VMEM((m, ns2), F32)],
        compiler_params=_params("arbitrary"),
        name="s5_core",
    )(a, toep, bst, cst, avec)


def _s5_tables(a_re, a_im, b_re, b_im, c_re, c_im, d_skip, log_step):
    hi = lax.Precision.HIGHEST
    step = jnp.exp(log_step)[..., None]
    lr, li = a_re * step, a_im * step
    tau = jnp.arange(CHUNK + 1, dtype=F32)[:, None]
    mag = jnp.exp(tau * lr[..., None, :])
    pw_re = mag * jnp.cos(tau * li[..., None, :])
    pw_im = mag * jnp.sin(tau * li[..., None, :])
    num_re, num_im = pw_re[..., 1, :] - 1.0, pw_im[..., 1, :]
    den = a_re * a_re + a_im * a_im
    cf_re = (num_re * a_re + num_im * a_im) / den
    cf_im = (num_im * a_re - num_re * a_im) / den
    bb_re = cf_re[..., None] * b_re - cf_im[..., None] * b_im
    bb_im = cf_re[..., None] * b_im + cf_im[..., None] * b_re
    e_re = pw_re[..., None] * bb_re[..., None, :, :] - pw_im[..., None] * bb_im[..., None, :, :]
    e_im = pw_re[..., None] * bb_im[..., None, :, :] + pw_im[..., None] * bb_re[..., None, :, :]
    kern = (jnp.einsum('lgpn,lgtnq->lgtpq', c_re, e_re, precision=hi)
            - jnp.einsum('lgpn,lgtnq->lgtpq', c_im, e_im, precision=hi))
    eye = jnp.eye(SSM_GROUP, dtype=F32)
    kern = kern.at[:, :, 0].add(d_skip[..., :, None] * eye)
    idx = jnp.arange(CHUNK)
    lag = idx[None, :] - idx[:, None]
    blk = jnp.where((lag >= 0)[..., None, None], kern[:, :, jnp.clip(lag, 0)], 0.0)
    dp, g = a_re.shape[:2]
    kk = CHUNK * SSM_GROUP
    toep = blk.transpose(0, 1, 2, 5, 3, 4).reshape(dp, g, kk, kk)
    back = CHUNK - 1 - idx
    bs_re = e_re[:, :, back].transpose(0, 1, 2, 4, 3).reshape(dp, g, kk, SSM_STATE)
    bs_im = e_im[:, :, back].transpose(0, 1, 2, 4, 3).reshape(dp, g, kk, SSM_STATE)
    bst = jnp.concatenate([bs_re, bs_im, bs_im, bs_re], axis=-1)
    fw_re, fw_im = pw_re[:, :, 1:], pw_im[:, :, 1:]
    ca_re = c_re[:, :, None] * fw_re[..., None, :] - c_im[:, :, None] * fw_im[..., None, :]
    ca_im = c_re[:, :, None] * fw_im[..., None, :] + c_im[:, :, None] * fw_re[..., None, :]
    cst = jnp.concatenate([ca_re.transpose(0, 1, 4, 2, 3).reshape(dp, g, SSM_STATE, kk),
                           -ca_im.transpose(0, 1, 4, 2, 3).reshape(dp, g, SSM_STATE, kk)], axis=2)
    ar, ai = pw_re[:, :, CHUNK], pw_im[:, :, CHUNK]
    avec = jnp.stack([jnp.concatenate([ar, ar], -1), jnp.concatenate([-ai, ai], -1),
                      jnp.concatenate([ai, -ai], -1), jnp.zeros_like(jnp.concatenate([ar, ar], -1))], axis=2)
    return toep.astype(BF16), bst.astype(BF16), cst.astype(BF16), avec


def _glu_kernel(gy_ref, zs_ref, w_ref, b_ref, o_ref, *, width):
    ab = jnp.dot(gy_ref[...], w_ref[...], preferred_element_type=F32) + b_ref[...]
    z = zs_ref[...].astype(F32)
    o_ref[...] = (ab[:, :width] * jax.nn.sigmoid(ab[:, width:]) * (z * jax.nn.sigmoid(z))).astype(BF16)


def _glu(gy, proj, w_glu_b, b_glu, layer, tm):
    bsz, s, width = gy.shape
    return pl.pallas_call(
        functools.partial(_glu_kernel, width=width),
        grid=(bsz, s // tm),
        in_specs=[pl.BlockSpec((None, tm, width), lambda b, m: (b, m, 0)),
                  pl.BlockSpec((None, tm, width), lambda b, m: (b, m, 1)),
                  pl.BlockSpec((None, width, 2 * width), lambda b, m: (layer, 0, 0)),
                  pl.BlockSpec((None, 1, 2 * width), lambda b, m: (layer, 0, 0))],
        out_specs=pl.BlockSpec((None, tm, width), lambda b, m: (b, m, 0)),
        out_shape=jax.ShapeDtypeStruct((bsz, s, width), BF16),
        compiler_params=_params("arbitrary", "arbitrary"),
        name="s5_glu",
    )(gy, proj, w_glu_b, b_glu)


def _attn_kernel(linit_ref, q_ref, k_ref, v_ref, za_ref, lq1_ref, lk1_ref, lq2_ref, lk2_ref, sg_ref,
                 o_ref, km_ref, m_ref, l_ref, acc_ref, *, tq):
    qi = pl.program_id(2)

    @pl.when(qi == 0)
    def _():
        kk = k_ref[...]
        grp = (lax.broadcasted_iota(jnp.int32, kk.shape, 1) % LANES) // (HEAD_DIM // 2)
        for combo in range(4):
            km_ref[combo] = jnp.where(grp == combo, kk, jnp.zeros_like(kk))

    m_ref[...] = jnp.full(m_ref.shape, -jnp.inf, F32)
    l_ref[...] = jnp.zeros(l_ref.shape, F32)
    acc_ref[...] = jnp.zeros(acc_ref.shape, F32)
    q = q_ref[...]

    def tile(kt, diagonal):
        ks = pl.multiple_of(kt * tq, tq)
        if diagonal:
            row = lax.broadcasted_iota(jnp.int32, (tq, tq), 0)
            col = lax.broadcasted_iota(jnp.int32, (tq, tq), 1)
            keep = col <= row
        for combo in range(4):
            hp = combo // 2
            s = lax.dot_general(q, km_ref[combo, pl.ds(ks, tq), :], (((1,), (1,)), ((), ())),
                                preferred_element_type=F32)
            if diagonal:
                s = jnp.where(keep, s, -jnp.inf)
            m_prev = m_ref[combo]
            m_new = jnp.maximum(m_prev, jnp.max(s, axis=-1, keepdims=True))
            alpha = jnp.exp(m_prev - m_new)
            p = jnp.exp(s - m_new)
            l_ref[combo] = alpha * l_ref[combo] + jnp.sum(p, axis=-1, keepdims=True)
            vv = v_ref[pl.ds(ks, tq), hp * LANES:(hp + 1) * LANES]
            acc_ref[combo] = alpha * acc_ref[combo] + jnp.dot(p.astype(BF16), vv, preferred_element_type=F32)
            m_ref[combo] = m_new

    def full_tile(kt, carry):
        tile(kt, False)
        return carry

    lax.fori_loop(0, qi, full_tile, 0)
    tile(qi, True)

    linit = linit_ref[0]
    lam = (jnp.exp(jnp.sum(lq1_ref[...] * lk1_ref[...], axis=-1, keepdims=True))
           - jnp.exp(jnp.sum(lq2_ref[...] * lk2_ref[...], axis=-1, keepdims=True)) + linit)
    for hp in range(2):
        o = acc_ref[2 * hp] / l_ref[2 * hp] - lam * (acc_ref[2 * hp + 1] / l_ref[2 * hp + 1])
        y = o * lax.rsqrt(jnp.mean(o * o, axis=-1, keepdims=True) + NORM_EPS) * sg_ref[...]
        y = y * (1.0 - linit)
        z = za_ref[:, hp * LANES:(hp + 1) * LANES].astype(F32)
        o_ref[:, hp * LANES:(hp + 1) * LANES] = (y * (z * jax.nn.sigmoid(z))).astype(BF16)


def _attention(proj, lam_q1, lam_k1, lam_q2, lam_k2, sub_g, layer, lambda_init, tq):
    bsz, s, w6 = proj.shape
    width = w6 // 6
    npairs = width // PAIR
    vec = lambda: pl.BlockSpec((None, 1, HEAD_DIM), lambda b, j, i: (layer, 0, 0))
    return pl.pallas_call(
        functools.partial(_attn_kernel, tq=tq),
        grid=(bsz, npairs, s // tq),
        in_specs=[pl.BlockSpec(memory_space=pltpu.SMEM),
                  pl.BlockSpec((None, tq, PAIR), lambda b, j, i: (b, i, 2 * npairs + j)),
                  pl.BlockSpec((None, s, PAIR), lambda b, j, i: (b, 0, 3 * npairs + j)),
                  pl.BlockSpec((None, s, PAIR), lambda b, j, i: (b, 0, 4 * npairs + j)),
                  pl.BlockSpec((None, tq, PAIR), lambda b, j, i: (b, i, 5 * npairs + j)),
                  vec(), vec(), vec(), vec(),
                  pl.BlockSpec((None, 1, 2 * HEAD_DIM), lambda b, j, i: (layer, 0, 0))],
        out_specs=pl.BlockSpec((None, tq, PAIR), lambda b, j, i: (b, i, j)),
        out_shape=jax.ShapeDtypeStruct((bsz, s, width), BF16),
        scratch_shapes=[pltpu.VMEM((4, s, PAIR), BF16),
                        pltpu.VMEM((4, tq, 1), F32), pltpu.VMEM((4, tq, 1), F32),
                        pltpu.VMEM((4, tq, LANES), F32)],
        compiler_params=_params("arbitrary", "arbitrary", "arbitrary"),
        name="diff_attn",
    )(jnp.full((1,), lambda_init, F32), proj, proj, proj, proj, lam_q1, lam_k1, lam_q2, lam_k2, sub_g)


def _out_proj_kernel(ys_ref, ya_ref, ws_ref, wa_ref, x_ref, gate_ref, fg_ref, o_ref, *, final):
    y = jnp.dot(ys_ref[...], ws_ref[...], preferred_element_type=F32)
    y = y + jnp.dot(ya_ref[...], wa_ref[...], preferred_element_type=F32)
    xn = x_ref[...] + gate_ref[...] * y
    if final:
        xn = xn * lax.rsqrt(jnp.mean(xn * xn, axis=-1, keepdims=True) + NORM_EPS) * fg_ref[...]
    o_ref[...] = xn


def _out_proj(ys, ya, w_out_b, x, gate, final_g, layer, final, tm):
    bsz, s, d = x.shape
    width = ys.shape[-1]
    return pl.pallas_call(
        functools.partial(_out_proj_kernel, final=final),
        grid=(bsz, s // tm),
        in_specs=[pl.BlockSpec((None, tm, width), lambda b, m: (b, m, 0)),
                  pl.BlockSpec((None, tm, width), lambda b, m: (b, m, 0)),
                  pl.BlockSpec((None, width, d), lambda b, m: (layer, 0, 0)),
                  pl.BlockSpec((None, width, d), lambda b, m: (layer, 1, 0)),
                  pl.BlockSpec((None, tm, d), lambda b, m: (b, m, 0)),
                  pl.BlockSpec((None, 1, d), lambda b, m: (b, 0, 0)),
                  pl.BlockSpec((1, d), lambda b, m: (0, 0))],
        out_specs=pl.BlockSpec((None, tm, d), lambda b, m: (b, m, 0)),
        out_shape=jax.ShapeDtypeStruct((bsz, s, d), F32),
        compiler_params=_params("arbitrary", "arbitrary"),
        name="out_proj",
    )(ys, ya, w_out_b, w_out_b, x, gate, final_g)


def _qk_relayout(w):
    half = HEAD_DIM // 2
    lead = w.shape[:-1]
    w = w.reshape(*lead, -1, 2, 2, 2, half)
    nd = len(lead)
    w = w.transpose(*range(nd), nd, nd + 3, nd + 1, nd + 2, nd + 4)
    return w.reshape(*lead, -1)


def _tile(n, target):
    t = min(n, target)
    while n % t or t % LANES:
        t -= LANES
    return t


def kernel(x, c, positions, norm_g, w_ada, b_ada, w_in, w_out, ssm_a_re, ssm_a_im, ssm_b_re, ssm_b_im,
           ssm_c_re, ssm_c_im, ssm_d, ssm_log_step, w_glu, b_glu, lam_q1, lam_k1, lam_q2, lam_k2,
           sub_g, final_g):
    bsz, s, d = x.shape
    depth = w_in.shape[0]
    width = d // 2
    groups = width // SSM_GROUP
    nchunks = s // CHUNK
    assert w_in.shape[-1] == 6 * width and width % PAIR == 0 and s % CHUNK == 0 and bsz % SUBLANES == 0
    tm = _tile(s, 512)
    tq = _tile(s, 512)

    w_in_b = w_in.astype(BF16)
    w_in_p = jnp.concatenate(
        [w_in_b[:, :, :2 * width], _qk_relayout(w_in_b[:, :, 2 * width:3 * width]),
         _qk_relayout(w_in_b[:, :, 3 * width:4 * width]), w_in_b[:, :, 4 * width:]], axis=-1)
    w_out_b = w_out.astype(BF16)
    w_glu_b = w_glu.astype(BF16)
    toep, bst, cst, avec = _s5_tables(ssm_a_re, ssm_a_im, ssm_b_re, ssm_b_im, ssm_c_re, ssm_c_im,
                                      ssm_d, ssm_log_step)
    half = HEAD_DIM // 2
    inv_freq = ROPE_THETA ** (-jnp.arange(half, dtype=F32) / half)
    ang = positions.astype(F32)[..., None] * inv_freq
    cos_t = jnp.tile(jnp.cos(ang), (1, 1, LANES // half))
    sin_t = jnp.tile(jnp.sin(ang), (1, 1, LANES // half))

    mod = _ada_mod(c, w_ada, b_ada).reshape(depth, bsz, 1, 3 * d)
    norm_g3 = norm_g.reshape(depth, 1, d)
    b_glu3 = b_glu.reshape(depth, 1, 2 * width)
    lam3 = [v.reshape(depth, 1, HEAD_DIM) for v in (lam_q1, lam_k1, lam_q2, lam_k2)]
    sub_g3 = sub_g.reshape(depth, 1, 2 * HEAD_DIM)
    final_g2 = final_g.reshape(1, d)

    for l in range(depth):
        lambda_init = 0.8 - 0.6 * math.exp(-0.3 * l)
        shift, scale, gate = mod[l, :, :, :d], mod[l, :, :, d:2 * d], mod[l, :, :, 2 * d:]
        proj = _in_proj(x, shift, scale, norm_g3, w_in_p, l, cos_t, sin_t, tm)
        a = proj[:, :, :width].reshape(bsz, nchunks, CHUNK, groups, SSM_GROUP)
        a = a.transpose(3, 1, 0, 2, 4).reshape(groups, nchunks * bsz, CHUNK * SSM_GROUP)
        gy = _s5_core(a, toep, bst, cst, avec, l, bsz)
        gy = gy.reshape(groups, nchunks, bsz, CHUNK, SSM_GROUP).transpose(2, 1, 3, 0, 4).reshape(bsz, s, width)
        ys = _glu(gy, proj, w_glu_b, b_glu3, l, tm)
        ya = _attention(proj, *lam3, sub_g3, l, lambda_init, tq)
        x = _out_proj(ys, ya, w_out_b, x, gate, final_g2, l, l == depth - 1, tm)
    return x
```

```python
import functools
import math

import jax
import jax.numpy as jnp
import numpy as np
from jax import lax
from jax.experimental import pallas as pl
from jax.experimental.pallas import tpu as pltpu

LANES = 128
SUBLANES = 8
V7X_VMEM_REQUEST_BYTES = 56 * 1024 * 1024

SSM_GROUP = 16
SSM_STATE = 64
HEAD_DIM = 64
ROPE_THETA = 10000.0
NORM_EPS = 1e-6
CHUNK = 16
PAIR = 4 * HEAD_DIM
TK = 256
Q_SCALE = HEAD_DIM ** -0.5 * math.log2(math.e)

F32 = jnp.float32
BF16 = jnp.bfloat16


def _params(*sem):
    return pltpu.CompilerParams(dimension_semantics=sem, vmem_limit_bytes=V7X_VMEM_REQUEST_BYTES)


def _ada_kernel(c_ref, w_ref, b_ref, o_ref):
    c = c_ref[...]
    act = c * jax.nn.sigmoid(c)
    o_ref[...] = jnp.dot(act, w_ref[...], preferred_element_type=F32,
                         precision=lax.Precision.HIGHEST) + b_ref[...]


def _ada_mod(c, w_ada, b_ada):
    depth, d, n3 = w_ada.shape
    bsz = c.shape[0]
    tn = _tile(n3, 1024)
    return pl.pallas_call(
        _ada_kernel,
        grid=(depth, n3 // tn),
        in_specs=[pl.BlockSpec((bsz, d), lambda l, n: (0, 0)),
                  pl.BlockSpec((None, d, tn), lambda l, n: (l, 0, n)),
                  pl.BlockSpec((None, 1, tn), lambda l, n: (l, 0, n))],
        out_specs=pl.BlockSpec((None, bsz, tn), lambda l, n: (l, 0, n)),
        out_shape=jax.ShapeDtypeStruct((depth, bsz, n3), F32),
        compiler_params=_params("arbitrary", "arbitrary"),
        name="ada_mod",
    )(c, w_ada, b_ada.reshape(depth, 1, n3))


def _in_proj_kernel(x_ref, shift_ref, scale_ref, g_ref, w_ref, cos_ref, sin_ref, o_ref, h_ref, *, npairs):
    n = pl.program_id(2)

    @pl.when(n == 0)
    def _():
        xf = x_ref[...]
        y = xf * lax.rsqrt(jnp.mean(xf * xf, axis=-1, keepdims=True) + NORM_EPS) * g_ref[...]
        h_ref[...] = (y * (1.0 + scale_ref[...]) + shift_ref[...]).astype(BF16)

    acc = jnp.dot(h_ref[...], w_ref[...], preferred_element_type=F32)
    is_rope = jnp.logical_or(n == 2, n == 3)

    @pl.when(is_rope)
    def _():
        qs = jnp.where(n == 2, Q_SCALE, 1.0).astype(F32)
        cs = cos_ref[...] * qs
        sn = sin_ref[...] * qs
        for j in range(npairs):
            lo = j * PAIR
            t1 = acc[:, lo:lo + LANES]
            t2 = acc[:, lo + LANES:lo + PAIR]
            o_ref[:, lo:lo + LANES] = (t1 * cs - t2 * sn).astype(BF16)
            o_ref[:, lo + LANES:lo + PAIR] = (t2 * cs + t1 * sn).astype(BF16)

    @pl.when(jnp.logical_not(is_rope))
    def _():
        o_ref[...] = acc.astype(BF16)


def _in_proj(x, shift, scale, norm_g, w_in_p, layer, cos_t, sin_t, tm):
    bsz, s, d = x.shape
    width = w_in_p.shape[-1] // 6
    npairs = width // PAIR
    return pl.pallas_call(
        functools.partial(_in_proj_kernel, npairs=npairs),
        grid=(bsz, s // tm, 6),
        in_specs=[pl.BlockSpec((None, tm, d), lambda b, m, n: (b, m, 0)),
                  pl.BlockSpec((None, 1, d), lambda b, m, n: (b, 0, 0)),
                  pl.BlockSpec((None, 1, d), lambda b, m, n: (b, 0, 0)),
                  pl.BlockSpec((None, 1, d), lambda b, m, n: (layer, 0, 0)),
                  pl.BlockSpec((None, d, width), lambda b, m, n: (layer, 0, n)),
                  pl.BlockSpec((None, tm, LANES), lambda b, m, n: (b, m, 0)),
                  pl.BlockSpec((None, tm, LANES), lambda b, m, n: (b, m, 0))],
        out_specs=pl.BlockSpec((None, tm, width), lambda b, m, n: (b, m, n)),
        out_shape=jax.ShapeDtypeStruct((bsz, s, 6 * width), BF16),
        scratch_shapes=[pltpu.VMEM((tm, d), BF16)],
        compiler_params=_params("arbitrary", "arbitrary", "arbitrary"),
        name="in_proj",
    )(x, shift, scale, norm_g, w_in_p, cos_t, sin_t)


def _gelu_tanh(y):
    return 0.5 * y * (1.0 + jnp.tanh(math.sqrt(2.0 / math.pi) * (y + 0.044715 * (y * y * y))))


def _s5_kernel(a_ref, toep_ref, bst_ref, cst_ref, av_ref, o_ref, sb_ref, xp_ref, *, bsz, nchunks):
    a = a_ref[...]
    sb_ref[...] = jnp.dot(a, bst_ref[...], preferred_element_type=F32)
    ar = av_ref[0:1, :]
    ai1 = av_ref[1:2, :]
    ai2 = av_ref[2:3, :]
    ns2 = 2 * SSM_STATE

    def body(c, carry):
        xs, xw = carry
        r = pl.multiple_of(c * bsz, bsz)
        xp_ref[pl.ds(r, bsz), :] = xs
        sb = sb_ref[pl.ds(r, bsz), :]
        return (ar * xs + ai1 * xw + sb[:, :ns2], ar * xw + ai2 * xs + sb[:, ns2:])

    zero = jnp.zeros((bsz, ns2), F32)
    lax.fori_loop(0, nchunks, body, (zero, zero), unroll=8)
    y = jnp.dot(a, toep_ref[...], preferred_element_type=F32)
    y = y + jnp.dot(xp_ref[...].astype(BF16), cst_ref[...], preferred_element_type=F32)
    o_ref[...] = _gelu_tanh(y).astype(BF16)


def _s5_core(a, toep, bst, cst, avec, layer, bsz):
    g, m, kk = a.shape
    ns2 = 2 * SSM_STATE
    return pl.pallas_call(
        functools.partial(_s5_kernel, bsz=bsz, nchunks=m // bsz),
        grid=(g,),
        in_specs=[pl.BlockSpec((None, m, kk), lambda i: (i, 0, 0)),
                  pl.BlockSpec((None, None, kk, kk), lambda i: (layer, i, 0, 0)),
                  pl.BlockSpec((None, None, kk, 2 * ns2), lambda i: (layer, i, 0, 0)),
                  pl.BlockSpec((None, None, ns2, kk), lambda i: (layer, i, 0, 0)),
                  pl.BlockSpec((None, None, 4, ns2), lambda i: (layer, i, 0, 0))],
        out_specs=pl.BlockSpec((None, m, kk), lambda i: (i, 0, 0)),
        out_shape=jax.ShapeDtypeStruct((g, m, kk), BF16),
        scratch_shapes=[pltpu.VMEM((m, 2 * ns2), F32), pltpu.VMEM((m, ns2), F32)],
        compiler_params=_params("arbitrary"),
        name="s5_core",
    )(a, toep, bst, cst, avec)


def _s5_tables(a_re, a_im, b_re, b_im, c_re, c_im, d_skip, log_step):
    hi = lax.Precision.HIGHEST
    step = jnp.exp(log_step)[..., None]
    lr, li = a_re * step, a_im * step
    tau = jnp.arange(CHUNK + 1, dtype=F32)[:, None]
    mag = jnp.exp(tau * lr[..., None, :])
    pw_re = mag * jnp.cos(tau * li[..., None, :])
    pw_im = mag * jnp.sin(tau * li[..., None, :])
    num_re, num_im = pw_re[..., 1, :] - 1.0, pw_im[..., 1, :]
    den = a_re * a_re + a_im * a_im
    cf_re = (num_re * a_re + num_im * a_im) / den
    cf_im = (num_im * a_re - num_re * a_im) / den
    bb_re = cf_re[..., None] * b_re - cf_im[..., None] * b_im
    bb_im = cf_re[..., None] * b_im + cf_im[..., None] * b_re
    e_re = pw_re[..., None] * bb_re[..., None, :, :] - pw_im[..., None] * bb_im[..., None, :, :]
    e_im = pw_re[..., None] * bb_im[..., None, :, :] + pw_im[..., None] * bb_re[..., None, :, :]
    kern = (jnp.einsum('lgpn,lgtnq->lgtpq', c_re, e_re, precision=hi)
            - jnp.einsum('lgpn,lgtnq->lgtpq', c_im, e_im, precision=hi))
    eye = jnp.eye(SSM_GROUP, dtype=F32)
    kern = kern.at[:, :, 0].add(d_skip[..., :, None] * eye)
    idx = jnp.arange(CHUNK)
    lag = idx[None, :] - idx[:, None]
    blk = jnp.where((lag >= 0)[..., None, None], kern[:, :, jnp.clip(lag, 0)], 0.0)
    dp, g = a_re.shape[:2]
    kk = CHUNK * SSM_GROUP
    toep = blk.transpose(0, 1, 2, 5, 3, 4).reshape(dp, g, kk, kk)
    back = CHUNK - 1 - idx
    bs_re = e_re[:, :, back].transpose(0, 1, 2, 4, 3).reshape(dp, g, kk, SSM_STATE)
    bs_im = e_im[:, :, back].transpose(0, 1, 2, 4, 3).reshape(dp, g, kk, SSM_STATE)
    bst = jnp.concatenate([bs_re, bs_im, bs_im, bs_re], axis=-1)
    fw_re, fw_im = pw_re[:, :, 1:], pw_im[:, :, 1:]
    ca_re = c_re[:, :, None] * fw_re[..., None, :] - c_im[:, :, None] * fw_im[..., None, :]
    ca_im = c_re[:, :, None] * fw_im[..., None, :] + c_im[:, :, None] * fw_re[..., None, :]
    cst = jnp.concatenate([ca_re.transpose(0, 1, 4, 2, 3).reshape(dp, g, SSM_STATE, kk),
                           -ca_im.transpose(0, 1, 4, 2, 3).reshape(dp, g, SSM_STATE, kk)], axis=2)
    ar, ai = pw_re[:, :, CHUNK], pw_im[:, :, CHUNK]
    avec = jnp.stack([jnp.concatenate([ar, ar], -1), jnp.concatenate([-ai, ai], -1),
                      jnp.concatenate([ai, -ai], -1), jnp.zeros_like(jnp.concatenate([ar, ar], -1))], axis=2)
    return toep.astype(BF16), bst.astype(BF16), cst.astype(BF16), avec


def _glu_kernel(gy_ref, zs_ref, w_ref, b_ref, o_ref, *, width):
    ab = jnp.dot(gy_ref[...], w_ref[...], preferred_element_type=F32) + b_ref[...]
    z = zs_ref[...].astype(F32)
    o_ref[...] = (ab[:, :width] * jax.nn.sigmoid(ab[:, width:]) * (z * jax.nn.sigmoid(z))).astype(BF16)


def _glu(gy, proj, w_glu_b, b_glu, layer, tm):
    bsz, s, width = gy.shape
    return pl.pallas_call(
        functools.partial(_glu_kernel, width=width),
        grid=(bsz, s // tm),
        in_specs=[pl.BlockSpec((None, tm, width), lambda b, m: (b, m, 0)),
                  pl.BlockSpec((None, tm, width), lambda b, m: (b, m, 1)),
                  pl.BlockSpec((None, width, 2 * width), lambda b, m: (layer, 0, 0)),
                  pl.BlockSpec((None, 1, 2 * width), lambda b, m: (layer, 0, 0))],
        out_specs=pl.BlockSpec((None, tm, width), lambda b, m: (b, m, 0)),
        out_shape=jax.ShapeDtypeStruct((bsz, s, width), BF16),
        compiler_params=_params("arbitrary", "arbitrary"),
        name="s5_glu",
    )(gy, proj, w_glu_b, b_glu)


def _attn_kernel(linit_ref, q_ref, k_ref, v_ref, za_ref, lq1_ref, lk1_ref, lq2_ref, lk2_ref, sg_ref,
                 o_ref, km_ref, vt_ref, qt_ref, m_ref, l_ref, acc_ref, s0_ref, s1_ref, p0_ref, p1_ref,
                 a0_ref, a1_ref, c0_ref, c1_ref, *, tq):
    qi = pl.program_id(2)
    nq = tq // TK

    @pl.when(qi == 0)
    def _():
        kk = k_ref[...]
        grp = (lax.broadcasted_iota(jnp.int32, kk.shape, 1) % LANES) // (HEAD_DIM // 2)
        for combo in range(4):
            km_ref[combo] = jnp.where(grp == combo, kk, jnp.zeros_like(kk))
        vt_ref[...] = v_ref[...].astype(F32).T.astype(BF16)

    qt_ref[...] = q_ref[...].astype(F32).T.astype(BF16)
    m_ref[...] = jnp.full(m_ref.shape, -jnp.inf, F32)
    l_ref[...] = jnp.zeros(l_ref.shape, F32)
    acc_ref[...] = jnp.zeros(acc_ref.shape, F32)

    def scores(kt, s_ref, cm_ref, c0):
        ks = pl.multiple_of(kt * TK, TK)
        qt = qt_ref[:, c0:]
        for combo in range(4):
            s = jnp.dot(km_ref[combo, pl.ds(ks, TK), :], qt, preferred_element_type=F32)
            s_ref[combo, :, c0:] = s
            cm_ref[combo, :, c0:] = jnp.max(s, axis=0, keepdims=True)

    def softmax(s_ref, cm_ref, p_ref, a_ref, c0, shift):
        for combo in range(4):
            for c in range(c0, tq, LANES):
                cs = slice(c, c + LANES)
                if shift is not None:
                    row = lax.broadcasted_iota(jnp.int32, (TK, LANES), 0)
                    col = lax.broadcasted_iota(jnp.int32, (TK, LANES), 1)
                    s = jnp.where(row + (shift - c) <= col, s_ref[combo, :, cs], -jnp.inf)
                    cmax = jnp.max(s, axis=0, keepdims=True)
                else:
                    s = s_ref[combo, :, cs]
                    cmax = cm_ref[combo, :, cs]
                m_prev = m_ref[combo, :, cs]
                m_new = jnp.maximum(m_prev, cmax)
                alpha = jnp.exp2(m_prev - m_new)
                p = jnp.exp2(s - m_new)
                l_ref[combo, :, cs] = alpha * l_ref[combo, :, cs] + jnp.sum(p, axis=0, keepdims=True)
                p_ref[combo, :, cs] = p.astype(BF16)
                a_ref[combo, :, cs] = alpha
                m_ref[combo, :, cs] = m_new

    def values(kt, p_ref, a_ref, c0):
        ks = pl.multiple_of(kt * TK, TK)
        cs = slice(c0, tq)
        for combo in range(4):
            hp = combo // 2
            pv = jnp.dot(vt_ref[hp * LANES:(hp + 1) * LANES, pl.ds(ks, TK)], p_ref[combo, :, cs],
                         preferred_element_type=F32)
            acc_ref[combo, :, cs] = a_ref[combo, :, cs] * acc_ref[combo, :, cs] + pv

    scores(0, s0_ref, c0_ref, 0)

    def pair(i, carry):
        kt = 2 * i
        scores(kt + 1, s1_ref, c1_ref, 0)
        softmax(s0_ref, c0_ref, p0_ref, a0_ref, 0, None)
        values(kt, p0_ref, a0_ref, 0)
        scores(kt + 2, s0_ref, c0_ref, 0)
        softmax(s1_ref, c1_ref, p1_ref, a1_ref, 0, None)
        values(kt + 1, p1_ref, a1_ref, 0)
        return carry

    lax.fori_loop(0, qi * (nq // 2), pair, 0)
    kd = qi * nq
    bufs = ((s0_ref, c0_ref, p0_ref, a0_ref), (s1_ref, c1_ref, p1_ref, a1_ref))
    for d in range(nq):
        s_ref, cm_ref, p_ref, a_ref = bufs[d % 2]
        if d > 0:
            scores(kd + d, s_ref, cm_ref, d * TK)
        softmax(s_ref, cm_ref, p_ref, a_ref, d * TK, d * TK)
        values(kd + d, p_ref, a_ref, d * TK)

    linit = linit_ref[0]
    lam = (jnp.exp(jnp.sum(lq1_ref[...] * lk1_ref[...], axis=-1, keepdims=True))
           - jnp.exp(jnp.sum(lq2_ref[...] * lk2_ref[...], axis=-1, keepdims=True)) + linit)
    for hp in range(2):
        ot = acc_ref[2 * hp] / l_ref[2 * hp] - lam * (acc_ref[2 * hp + 1] / l_ref[2 * hp + 1])
        o = ot.T
        y = o * lax.rsqrt(jnp.mean(o * o, axis=-1, keepdims=True) + NORM_EPS) * sg_ref[...]
        y = y * (1.0 - linit)
        z = za_ref[:, hp * LANES:(hp + 1) * LANES].astype(F32)
        o_ref[:, hp * LANES:(hp + 1) * LANES] = (y * (z * jax.nn.sigmoid(z))).astype(BF16)


def _attention(proj, lam_q1, lam_k1, lam_q2, lam_k2, sub_g, layer, lambda_init, tq):
    bsz, s, w6 = proj.shape
    width = w6 // 6
    npairs = width // PAIR
    assert tq % (2 * TK) == 0
    vec = lambda: pl.BlockSpec((None, 1, HEAD_DIM), lambda b, j, i: (layer, 0, 0))
    sbuf = lambda: pltpu.VMEM((4, TK, tq), F32)
    pbuf = lambda: pltpu.VMEM((4, TK, tq), BF16)
    rowv = lambda: pltpu.VMEM((4, 1, tq), F32)
    return pl.pallas_call(
        functools.partial(_attn_kernel, tq=tq),
        grid=(bsz, npairs, s // tq),
        in_specs=[pl.BlockSpec(memory_space=pltpu.SMEM),
                  pl.BlockSpec((None, tq, PAIR), lambda b, j, i: (b, i, 2 * npairs + j)),
                  pl.BlockSpec((None, s, PAIR), lambda b, j, i: (b, 0, 3 * npairs + j)),
                  pl.BlockSpec((None, s, PAIR), lambda b, j, i: (b, 0, 4 * npairs + j)),
                  pl.BlockSpec((None, tq, PAIR), lambda b, j, i: (b, i, 5 * npairs + j)),
                  vec(), vec(), vec(), vec(),
                  pl.BlockSpec((None, 1, 2 * HEAD_DIM), lambda b, j, i: (layer, 0, 0))],
        out_specs=pl.BlockSpec((None, tq, PAIR), lambda b, j, i: (b, i, j)),
        out_shape=jax.ShapeDtypeStruct((bsz, s, width), BF16),
        scratch_shapes=[pltpu.VMEM((4, s, PAIR), BF16), pltpu.VMEM((PAIR, s), BF16), pltpu.VMEM((PAIR, tq), BF16),
                        rowv(), rowv(), pltpu.VMEM((4, LANES, tq), F32),
                        sbuf(), sbuf(), pbuf(), pbuf(), rowv(), rowv(), rowv(), rowv()],
        compiler_params=_params("arbitrary", "arbitrary", "arbitrary"),
        name="diff_attn",
    )(jnp.full((1,), lambda_init, F32), proj, proj, proj, proj, lam_q1, lam_k1, lam_q2, lam_k2, sub_g)


def _out_proj_kernel(ys_ref, ya_ref, ws_ref, wa_ref, x_ref, gate_ref, fg_ref, o_ref, *, final):
    y = jnp.dot(ys_ref[...], ws_ref[...], preferred_element_type=F32)
    y = y + jnp.dot(ya_ref[...], wa_ref[...], preferred_element_type=F32)
    xn = x_ref[...] + gate_ref[...] * y
    if final:
        xn = xn * lax.rsqrt(jnp.mean(xn * xn, axis=-1, keepdims=True) + NORM_EPS) * fg_ref[...]
    o_ref[...] = xn


def _out_proj(ys, ya, w_out_b, x, gate, final_g, layer, final, tm):
    bsz, s, d = x.shape
    width = ys.shape[-1]
    return pl.pallas_call(
        functools.partial(_out_proj_kernel, final=final),
        grid=(bsz, s // tm),
        in_specs=[pl.BlockSpec((None, tm, width), lambda b, m: (b, m, 0)),
                  pl.BlockSpec((None, tm, width), lambda b, m: (b, m, 0)),
                  pl.BlockSpec((None, width, d), lambda b, m: (layer, 0, 0)),
                  pl.BlockSpec((None, width, d), lambda b, m: (layer, 1, 0)),
                  pl.BlockSpec((None, tm, d), lambda b, m: (b, m, 0)),
                  pl.BlockSpec((None, 1, d), lambda b, m: (b, 0, 0)),
                  pl.BlockSpec((1, d), lambda b, m: (0, 0))],
        out_specs=pl.BlockSpec((None, tm, d), lambda b, m: (b, m, 0)),
        out_shape=jax.ShapeDtypeStruct((bsz, s, d), F32),
        compiler_params=_params("arbitrary", "arbitrary"),
        name="out_proj",
    )(ys, ya, w_out_b, w_out_b, x, gate, final_g)


def _qk_relayout(w):
    half = HEAD_DIM // 2
    lead = w.shape[:-1]
    w = w.reshape(*lead, -1, 2, 2, 2, half)
    nd = len(lead)
    w = w.transpose(*range(nd), nd, nd + 3, nd + 1, nd + 2, nd + 4)
    return w.reshape(*lead, -1)


def _tile(n, target):
    t = min(n, target)
    while n % t or t % LANES:
        t -= LANES
    return t


def kernel(x, c, positions, norm_g, w_ada, b_ada, w_in, w_out, ssm_a_re, ssm_a_im, ssm_b_re, ssm_b_im,
           ssm_c_re, ssm_c_im, ssm_d, ssm_log_step, w_glu, b_glu, lam_q1, lam_k1, lam_q2, lam_k2,
           sub_g, final_g):
    bsz, s, d = x.shape
    depth = w_in.shape[0]
    width = d // 2
    groups = width // SSM_GROUP
    nchunks = s // CHUNK
    assert w_in.shape[-1] == 6 * width and width % PAIR == 0 and s % CHUNK == 0 and bsz % SUBLANES == 0
    tm = _tile(s, 512)
    tq = _tile(s, 512)

    w_in_b = w_in.astype(BF16)
    w_in_p = jnp.concatenate(
        [w_in_b[:, :, :2 * width], _qk_relayout(w_in_b[:, :, 2 * width:3 * width]),
         _qk_relayout(w_in_b[:, :, 3 * width:4 * width]), w_in_b[:, :, 4 * width:]], axis=-1)
    w_out_b = w_out.astype(BF16)
    w_glu_b = w_glu.astype(BF16)
    toep, bst, cst, avec = _s5_tables(ssm_a_re, ssm_a_im, ssm_b_re, ssm_b_im, ssm_c_re, ssm_c_im,
                                      ssm_d, ssm_log_step)
    half = HEAD_DIM // 2
    inv_freq = ROPE_THETA ** (-jnp.arange(half, dtype=F32) / half)
    ang = positions.astype(F32)[..., None] * inv_freq
    cos_t = jnp.tile(jnp.cos(ang), (1, 1, LANES // half))
    sin_t = jnp.tile(jnp.sin(ang), (1, 1, LANES // half))

    mod = _ada_mod(c, w_ada, b_ada).reshape(depth, bsz, 1, 3 * d)
    norm_g3 = norm_g.reshape(depth, 1, d)
    b_glu3 = b_glu.reshape(depth, 1, 2 * width)
    lam3 = [v.reshape(depth, 1, HEAD_DIM) for v in (lam_q1, lam_k1, lam_q2, lam_k2)]
    sub_g3 = sub_g.reshape(depth, 1, 2 * HEAD_DIM)
    final_g2 = final_g.reshape(1, d)

    for l in range(depth):
        lambda_init = 0.8 - 0.6 * math.exp(-0.3 * l)
        shift, scale, gate = mod[l, :, :, :d], mod[l, :, :, d:2 * d], mod[l, :, :, 2 * d:]
        proj = _in_proj(x, shift, scale, norm_g3, w_in_p, l, cos_t, sin_t, tm)
        a = proj[:, :, :width].reshape(bsz, nchunks, CHUNK, groups, SSM_GROUP)
        a = a.transpose(3, 1, 0, 2, 4).reshape(groups, nchunks * bsz, CHUNK * SSM_GROUP)
        gy = _s5_core(a, toep, bst, cst, avec, l, bsz)
        gy = gy.reshape(groups, nchunks, bsz, CHUNK, SSM_GROUP).transpose(2, 1, 3, 0, 4).reshape(bsz, s, width)
        ys = _glu(gy, proj, w_glu_b, b_glu3, l, tm)
        ya = _attention(proj, *lam3, sub_g3, l, lambda_init, tq)
        x = _out_proj(ys, ya, w_out_b, x, gate, final_g2, l, l == depth - 1, tm)
    return x
```

```python
import functools
import math

import jax
import jax.numpy as jnp
import numpy as np
from jax import lax
from jax.experimental import pallas as pl
from jax.experimental.pallas import tpu as pltpu

LANES = 128
SUBLANES = 8
V7X_VMEM_REQUEST_BYTES = 56 * 1024 * 1024

SSM_GROUP = 16
SSM_STATE = 64
HEAD_DIM = 64
ROPE_THETA = 10000.0
NORM_EPS = 1e-6
CHUNK = 16
PAIR = 4 * HEAD_DIM
TK = 256
Q_SCALE = HEAD_DIM ** -0.5 * math.log2(math.e)

F32 = jnp.float32
BF16 = jnp.bfloat16


def _params(*sem):
    return pltpu.CompilerParams(dimension_semantics=sem, vmem_limit_bytes=V7X_VMEM_REQUEST_BYTES)


def _ada_kernel(c_ref, w_ref, b_ref, o_ref):
    c = c_ref[...]
    act = c * jax.nn.sigmoid(c)
    o_ref[...] = jnp.dot(act, w_ref[...], preferred_element_type=F32,
                         precision=lax.Precision.HIGHEST) + b_ref[...]


def _ada_mod(c, w_ada, b_ada):
    depth, d, n3 = w_ada.shape
    bsz = c.shape[0]
    tn = _tile(n3, 1024)
    return pl.pallas_call(
        _ada_kernel,
        grid=(depth, n3 // tn),
        in_specs=[pl.BlockSpec((bsz, d), lambda l, n: (0, 0)),
                  pl.BlockSpec((None, d, tn), lambda l, n: (l, 0, n)),
                  pl.BlockSpec((None, 1, tn), lambda l, n: (l, 0, n))],
        out_specs=pl.BlockSpec((None, bsz, tn), lambda l, n: (l, 0, n)),
        out_shape=jax.ShapeDtypeStruct((depth, bsz, n3), F32),
        compiler_params=_params("arbitrary", "arbitrary"),
        name="ada_mod",
    )(c, w_ada, b_ada.reshape(depth, 1, n3))


def _in_proj_kernel(x_ref, shift_ref, scale_ref, g_ref, w_ref, wut_ref, cos_ref, sin_ref, at_ref, o_ref, h_ref,
                    *, npairs, nc):
    n = pl.program_id(2)
    d = x_ref.shape[-1] // 2
    width = w_ref.shape[-1]

    @pl.when(n == 0)
    def _():
        for t in range(2):
            xf = x_ref[:, t * d:(t + 1) * d]
            y = xf * lax.rsqrt(jnp.mean(xf * xf, axis=-1, keepdims=True) + NORM_EPS) * g_ref[...]
            h = (y * (1.0 + scale_ref[...]) + shift_ref[...]).astype(BF16)
            h_ref[t * nc:(t + 1) * nc, :] = h
            ut = lax.dot_general(wut_ref[...], h, (((1,), (1,)), ((), ())), preferred_element_type=F32)
            at_ref[:, t * SSM_GROUP:(t + 1) * SSM_GROUP, :] = ut.reshape(-1, SSM_GROUP, nc).astype(BF16)

    @pl.when(n > 0)
    def _():
        acc = jnp.dot(h_ref[...], w_ref[...], preferred_element_type=F32)
        is_rope = jnp.logical_or(n == 2, n == 3)

        @pl.when(is_rope)
        def _():
            qs = jnp.where(n == 2, Q_SCALE, 1.0).astype(F32)
            for t in range(2):
                cs = cos_ref[:, t * LANES:(t + 1) * LANES] * qs
                sn = sin_ref[:, t * LANES:(t + 1) * LANES] * qs
                for j in range(npairs):
                    lo = j * PAIR
                    t1 = acc[t * nc:(t + 1) * nc, lo:lo + LANES]
                    t2 = acc[t * nc:(t + 1) * nc, lo + LANES:lo + PAIR]
                    o_ref[:, t * width + lo:t * width + lo + LANES] = (t1 * cs - t2 * sn).astype(BF16)
                    o_ref[:, t * width + lo + LANES:t * width + lo + PAIR] = (t2 * cs + t1 * sn).astype(BF16)

        @pl.when(jnp.logical_not(is_rope))
        def _():
            for t in range(2):
                o_ref[:, t * width:(t + 1) * width] = acc[t * nc:(t + 1) * nc, :].astype(BF16)


def _in_proj(x, shift, scale, norm_g, w_in_p, w_ut, layer, cos_t, sin_t):
    bsz, s, d = x.shape
    width = w_in_p.shape[-1] // 6
    npairs = width // PAIR
    nc = s // CHUNK
    groups = width // SSM_GROUP
    seg = lambda n: jnp.maximum(n, 1)
    at, proj = pl.pallas_call(
        functools.partial(_in_proj_kernel, npairs=npairs, nc=nc),
        grid=(bsz, CHUNK // 2, 6),
        in_specs=[pl.BlockSpec((None, nc, 2 * d), lambda b, ip, n: (b, 0, ip)),
                  pl.BlockSpec((None, 1, d), lambda b, ip, n: (b, 0, 0)),
                  pl.BlockSpec((None, 1, d), lambda b, ip, n: (b, 0, 0)),
                  pl.BlockSpec((None, 1, d), lambda b, ip, n: (layer, 0, 0)),
                  pl.BlockSpec((None, d, width), lambda b, ip, n: (layer, 0, seg(n))),
                  pl.BlockSpec((None, width, d), lambda b, ip, n: (layer, 0, 0)),
                  pl.BlockSpec((None, nc, 2 * LANES), lambda b, ip, n: (b, 0, ip)),
                  pl.BlockSpec((None, nc, 2 * LANES), lambda b, ip, n: (b, 0, ip))],
        out_specs=[pl.BlockSpec((groups, None, 2 * SSM_GROUP, nc), lambda b, ip, n: (0, b, ip, 0)),
                   pl.BlockSpec((None, None, nc, 2 * width), lambda b, ip, n: (seg(n) - 1, b, 0, ip))],
        out_shape=[jax.ShapeDtypeStruct((groups, bsz, CHUNK * SSM_GROUP, nc), BF16),
                   jax.ShapeDtypeStruct((5, bsz, nc, CHUNK * width), BF16)],
        scratch_shapes=[pltpu.VMEM((2 * nc, d), BF16)],
        compiler_params=_params("arbitrary", "arbitrary", "arbitrary"),
        name="in_proj",
    )(x.reshape(bsz, nc, CHUNK * d), shift, scale, norm_g, w_in_p, w_ut,
      cos_t.reshape(bsz, nc, CHUNK * LANES), sin_t.reshape(bsz, nc, CHUNK * LANES))
    return at, proj.reshape(5, bsz, s, width)


def _gelu_tanh(y):
    return 0.5 * y * (1.0 + jnp.tanh(math.sqrt(2.0 / math.pi) * (y + 0.044715 * (y * y * y))))


def _s5_kernel(at_ref, toep_ref, bst_ref, cst_ref, av_ref, gt_ref, a_ref, sb_ref, xp_ref, *, bsz, nchunks):
    ns2 = 2 * SSM_STATE
    kk = CHUNK * SSM_GROUP
    for b in range(bsz):
        a_ref[b * nchunks:(b + 1) * nchunks, :] = at_ref[b].T
    a = a_ref[...]
    sb = jnp.dot(a, bst_ref[...], preferred_element_type=F32)
    sb_ref[...] = jnp.swapaxes(sb.reshape(bsz, nchunks, 2 * ns2), 0, 1).reshape(nchunks * bsz, 2 * ns2)
    ar = av_ref[0:1, :]
    ai1 = av_ref[1:2, :]
    ai2 = av_ref[2:3, :]

    def body(c, carry):
        xs, xw = carry
        r = pl.multiple_of(c * bsz, bsz)
        xp_ref[pl.ds(r, bsz), :] = xs
        sb = sb_ref[pl.ds(r, bsz), :]
        return (ar * xs + ai1 * xw + sb[:, :ns2], ar * xw + ai2 * xs + sb[:, ns2:])

    zero = jnp.zeros((bsz, ns2), F32)
    lax.fori_loop(0, nchunks, body, (zero, zero), unroll=8)
    xp = jnp.swapaxes(xp_ref[...].reshape(nchunks, bsz, ns2), 0, 1).reshape(bsz * nchunks, ns2)
    y = jnp.dot(a, toep_ref[...], preferred_element_type=F32)
    y = y + jnp.dot(xp.astype(BF16), cst_ref[...], preferred_element_type=F32)
    gy = _gelu_tanh(y).astype(BF16)
    for b in range(bsz):
        gt_ref[b] = gy[b * nchunks:(b + 1) * nchunks, :].T


def _s5_core(at, toep, bst, cst, avec, layer):
    g, bsz, kk, nchunks = at.shape
    ns2 = 2 * SSM_STATE
    m = bsz * nchunks
    return pl.pallas_call(
        functools.partial(_s5_kernel, bsz=bsz, nchunks=nchunks),
        grid=(g,),
        in_specs=[pl.BlockSpec((None, bsz, kk, nchunks), lambda i: (i, 0, 0, 0)),
                  pl.BlockSpec((None, None, kk, kk), lambda i: (layer, i, 0, 0)),
                  pl.BlockSpec((None, None, kk, 2 * ns2), lambda i: (layer, i, 0, 0)),
                  pl.BlockSpec((None, None, ns2, kk), lambda i: (layer, i, 0, 0)),
                  pl.BlockSpec((None, None, 4, ns2), lambda i: (layer, i, 0, 0))],
        out_specs=pl.BlockSpec((None, bsz, kk, nchunks), lambda i: (i, 0, 0, 0)),
        out_shape=jax.ShapeDtypeStruct((g, bsz, kk, nchunks), BF16),
        scratch_shapes=[pltpu.VMEM((m, kk), BF16), pltpu.VMEM((m, 2 * ns2), F32), pltpu.VMEM((m, ns2), F32)],
        compiler_params=_params("arbitrary"),
        name="s5_core",
    )(at, toep, bst, cst, avec)


def _s5_tables(a_re, a_im, b_re, b_im, c_re, c_im, d_skip, log_step):
    hi = lax.Precision.HIGHEST
    step = jnp.exp(log_step)[..., None]
    lr, li = a_re * step, a_im * step
    tau = jnp.arange(CHUNK + 1, dtype=F32)[:, None]
    mag = jnp.exp(tau * lr[..., None, :])
    pw_re = mag * jnp.cos(tau * li[..., None, :])
    pw_im = mag * jnp.sin(tau * li[..., None, :])
    num_re, num_im = pw_re[..., 1, :] - 1.0, pw_im[..., 1, :]
    den = a_re * a_re + a_im * a_im
    cf_re = (num_re * a_re + num_im * a_im) / den
    cf_im = (num_im * a_re - num_re * a_im) / den
    bb_re = cf_re[..., None] * b_re - cf_im[..., None] * b_im
    bb_im = cf_re[..., None] * b_im + cf_im[..., None] * b_re
    e_re = pw_re[..., None] * bb_re[..., None, :, :] - pw_im[..., None] * bb_im[..., None, :, :]
    e_im = pw_re[..., None] * bb_im[..., None, :, :] + pw_im[..., None] * bb_re[..., None, :, :]
    kern = (jnp.einsum('lgpn,lgtnq->lgtpq', c_re, e_re, precision=hi)
            - jnp.einsum('lgpn,lgtnq->lgtpq', c_im, e_im, precision=hi))
    eye = jnp.eye(SSM_GROUP, dtype=F32)
    kern = kern.at[:, :, 0].add(d_skip[..., :, None] * eye)
    idx = jnp.arange(CHUNK)
    lag = idx[None, :] - idx[:, None]
    blk = jnp.where((lag >= 0)[..., None, None], kern[:, :, jnp.clip(lag, 0)], 0.0)
    dp, g = a_re.shape[:2]
    kk = CHUNK * SSM_GROUP
    toep = blk.transpose(0, 1, 2, 5, 3, 4).reshape(dp, g, kk, kk)
    back = CHUNK - 1 - idx
    bs_re = e_re[:, :, back].transpose(0, 1, 2, 4, 3).reshape(dp, g, kk, SSM_STATE)
    bs_im = e_im[:, :, back].transpose(0, 1, 2, 4, 3).reshape(dp, g, kk, SSM_STATE)
    bst = jnp.concatenate([bs_re, bs_im, bs_im, bs_re], axis=-1)
    fw_re, fw_im = pw_re[:, :, 1:], pw_im[:, :, 1:]
    ca_re = c_re[:, :, None] * fw_re[..., None, :] - c_im[:, :, None] * fw_im[..., None, :]
    ca_im = c_re[:, :, None] * fw_im[..., None, :] + c_im[:, :, None] * fw_re[..., None, :]
    cst = jnp.concatenate([ca_re.transpose(0, 1, 4, 2, 3).reshape(dp, g, SSM_STATE, kk),
                           -ca_im.transpose(0, 1, 4, 2, 3).reshape(dp, g, SSM_STATE, kk)], axis=2)
    ar, ai = pw_re[:, :, CHUNK], pw_im[:, :, CHUNK]
    avec = jnp.stack([jnp.concatenate([ar, ar], -1), jnp.concatenate([-ai, ai], -1),
                      jnp.concatenate([ai, -ai], -1), jnp.zeros_like(jnp.concatenate([ar, ar], -1))], axis=2)
    return toep.astype(BF16), bst.astype(BF16), cst.astype(BF16), avec


def _glu_kernel(gt_ref, zs_ref, w_ref, b_ref, o_ref, *, width, nc):
    gt = gt_ref[...]
    gy = jnp.concatenate([gt[:, t * SSM_GROUP:(t + 1) * SSM_GROUP, :].reshape(width, nc).T for t in range(2)],
                         axis=0)
    ab = jnp.dot(gy, w_ref[...], preferred_element_type=F32) + b_ref[...]
    out = ab[:, :width] * jax.nn.sigmoid(ab[:, width:])
    for t in range(2):
        z = zs_ref[:, t * width:(t + 1) * width].astype(F32)
        o_ref[:, t * width:(t + 1) * width] = (out[t * nc:(t + 1) * nc, :] * (z * jax.nn.sigmoid(z))).astype(BF16)


def _glu(gt, proj, w_glu_b, b_glu, layer):
    groups, bsz, kk, nc = gt.shape
    width = groups * SSM_GROUP
    s = nc * CHUNK
    ys = pl.pallas_call(
        functools.partial(_glu_kernel, width=width, nc=nc),
        grid=(bsz, CHUNK // 2),
        in_specs=[pl.BlockSpec((groups, None, 2 * SSM_GROUP, nc), lambda b, ip: (0, b, ip, 0)),
                  pl.BlockSpec((None, None, nc, 2 * width), lambda b, ip: (0, b, 0, ip)),
                  pl.BlockSpec((None, width, 2 * width), lambda b, ip: (layer, 0, 0)),
                  pl.BlockSpec((None, 1, 2 * width), lambda b, ip: (layer, 0, 0))],
        out_specs=pl.BlockSpec((None, nc, 2 * width), lambda b, ip: (b, 0, ip)),
        out_shape=jax.ShapeDtypeStruct((bsz, nc, CHUNK * width), BF16),
        compiler_params=_params("arbitrary", "arbitrary"),
        name="s5_glu",
    )(gt, proj.reshape(5, bsz, nc, CHUNK * width), w_glu_b, b_glu)
    return ys.reshape(bsz, s, width)


def _attn_kernel(linit_ref, q_ref, k_ref, v_ref, za_ref, lq1_ref, lk1_ref, lq2_ref, lk2_ref, sg_ref,
                 o_ref, km_ref, vt_ref, qt_ref, m_ref, l_ref, acc_ref, s0_ref, s1_ref, p0_ref, p1_ref,
                 a0_ref, a1_ref, c0_ref, c1_ref, *, tq):
    qi = pl.program_id(2)
    nq = tq // TK

    @pl.when(qi == 0)
    def _():
        kk = k_ref[...]
        grp = (lax.broadcasted_iota(jnp.int32, kk.shape, 1) % LANES) // (HEAD_DIM // 2)
        for combo in range(4):
            km_ref[combo] = jnp.where(grp == combo, kk, jnp.zeros_like(kk))
        vt_ref[...] = v_ref[...].astype(F32).T.astype(BF16)

    qt_ref[...] = q_ref[...].astype(F32).T.astype(BF16)
    m_ref[...] = jnp.full(m_ref.shape, -jnp.inf, F32)
    l_ref[...] = jnp.zeros(l_ref.shape, F32)
    acc_ref[...] = jnp.zeros(acc_ref.shape, F32)

    def scores(kt, s_ref, cm_ref, c0):
        ks = pl.multiple_of(kt * TK, TK)
        qt = qt_ref[:, c0:]
        for combo in range(4):
            s = jnp.dot(km_ref[combo, pl.ds(ks, TK), :], qt, preferred_element_type=F32)
            s_ref[combo, :, c0:] = s
            cm_ref[combo, :, c0:] = jnp.max(s, axis=0, keepdims=True)

    def softmax(s_ref, cm_ref, p_ref, a_ref, c0, shift):
        for combo in range(4):
            for c in range(c0, tq, LANES):
                cs = slice(c, c + LANES)
                if shift is not None:
                    row = lax.broadcasted_iota(jnp.int32, (TK, LANES), 0)
                    col = lax.broadcasted_iota(jnp.int32, (TK, LANES), 1)
                    s = jnp.where(row + (shift - c) <= col, s_ref[combo, :, cs], -jnp.inf)
                    cmax = jnp.max(s, axis=0, keepdims=True)
                else:
                    s = s_ref[combo, :, cs]
                    cmax = cm_ref[combo, :, cs]
                m_prev = m_ref[combo, :, cs]
                m_new = jnp.maximum(m_prev, cmax)
                alpha = jnp.exp2(m_prev - m_new)
                p = jnp.exp2(s - m_new)
                l_ref[combo, :, cs] = alpha * l_ref[combo, :, cs] + jnp.sum(p, axis=0, keepdims=True)
                p_ref[combo, :, cs] = p.astype(BF16)
                a_ref[combo, :, cs] = alpha
                m_ref[combo, :, cs] = m_new

    def values(kt, p_ref, a_ref, c0):
        ks = pl.multiple_of(kt * TK, TK)
        cs = slice(c0, tq)
        for combo in range(4):
            hp = combo // 2
            pv = jnp.dot(vt_ref[hp * LANES:(hp + 1) * LANES, pl.ds(ks, TK)], p_ref[combo, :, cs],
                         preferred_element_type=F32)
            acc_ref[combo, :, cs] = a_ref[combo, :, cs] * acc_ref[combo, :, cs] + pv

    scores(0, s0_ref, c0_ref, 0)

    def pair(i, carry):
        kt = 2 * i
        scores(kt + 1, s1_ref, c1_ref, 0)
        softmax(s0_ref, c0_ref, p0_ref, a0_ref, 0, None)
        values(kt, p0_ref, a0_ref, 0)
        scores(kt + 2, s0_ref, c0_ref, 0)
        softmax(s1_ref, c1_ref, p1_ref, a1_ref, 0, None)
        values(kt + 1, p1_ref, a1_ref, 0)
        return carry

    lax.fori_loop(0, qi * (nq // 2), pair, 0)
    kd = qi * nq
    bufs = ((s0_ref, c0_ref, p0_ref, a0_ref), (s1_ref, c1_ref, p1_ref, a1_ref))
    for d in range(nq):
        s_ref, cm_ref, p_ref, a_ref = bufs[d % 2]
        if d > 0:
            scores(kd + d, s_ref, cm_ref, d * TK)
        softmax(s_ref, cm_ref, p_ref, a_ref, d * TK, d * TK)
        values(kd + d, p_ref, a_ref, d * TK)

    linit = linit_ref[0]
    lam = (jnp.exp(jnp.sum(lq1_ref[...] * lk1_ref[...], axis=-1, keepdims=True))
           - jnp.exp(jnp.sum(lq2_ref[...] * lk2_ref[...], axis=-1, keepdims=True)) + linit)
    for hp in range(2):
        ot = acc_ref[2 * hp] / l_ref[2 * hp] - lam * (acc_ref[2 * hp + 1] / l_ref[2 * hp + 1])
        o = ot.T
        y = o * lax.rsqrt(jnp.mean(o * o, axis=-1, keepdims=True) + NORM_EPS) * sg_ref[...]
        y = y * (1.0 - linit)
        z = za_ref[:, hp * LANES:(hp + 1) * LANES].astype(F32)
        o_ref[:, hp * LANES:(hp + 1) * LANES] = (y * (z * jax.nn.sigmoid(z))).astype(BF16)


def _attention(proj, lam_q1, lam_k1, lam_q2, lam_k2, sub_g, layer, lambda_init, tq):
    _, bsz, s, width = proj.shape
    npairs = width // PAIR
    assert tq % (2 * TK) == 0
    vec = lambda: pl.BlockSpec((None, 1, HEAD_DIM), lambda b, j, i: (layer, 0, 0))
    sbuf = lambda: pltpu.VMEM((4, TK, tq), F32)
    pbuf = lambda: pltpu.VMEM((4, TK, tq), BF16)
    rowv = lambda: pltpu.VMEM((4, 1, tq), F32)
    return pl.pallas_call(
        functools.partial(_attn_kernel, tq=tq),
        grid=(bsz, npairs, s // tq),
        in_specs=[pl.BlockSpec(memory_space=pltpu.SMEM),
                  pl.BlockSpec((None, None, tq, PAIR), lambda b, j, i: (1, b, i, j)),
                  pl.BlockSpec((None, None, s, PAIR), lambda b, j, i: (2, b, 0, j)),
                  pl.BlockSpec((None, None, s, PAIR), lambda b, j, i: (3, b, 0, j)),
                  pl.BlockSpec((None, None, tq, PAIR), lambda b, j, i: (4, b, i, j)),
                  vec(), vec(), vec(), vec(),
                  pl.BlockSpec((None, 1, 2 * HEAD_DIM), lambda b, j, i: (layer, 0, 0))],
        out_specs=pl.BlockSpec((None, tq, PAIR), lambda b, j, i: (b, i, j)),
        out_shape=jax.ShapeDtypeStruct((bsz, s, width), BF16),
        scratch_shapes=[pltpu.VMEM((4, s, PAIR), BF16), pltpu.VMEM((PAIR, s), BF16), pltpu.VMEM((PAIR, tq), BF16),
                        rowv(), rowv(), pltpu.VMEM((4, LANES, tq), F32),
                        sbuf(), sbuf(), pbuf(), pbuf(), rowv(), rowv(), rowv(), rowv()],
        compiler_params=_params("arbitrary", "arbitrary", "arbitrary"),
        name="diff_attn",
    )(jnp.full((1,), lambda_init, F32), proj, proj, proj, proj, lam_q1, lam_k1, lam_q2, lam_k2, sub_g)


def _out_proj_kernel(ys_ref, ya_ref, ws_ref, wa_ref, x_ref, gate_ref, fg_ref, o_ref, *, final):
    y = jnp.dot(ys_ref[...], ws_ref[...], preferred_element_type=F32)
    y = y + jnp.dot(ya_ref[...], wa_ref[...], preferred_element_type=F32)
    xn = x_ref[...] + gate_ref[...] * y
    if final:
        xn = xn * lax.rsqrt(jnp.mean(xn * xn, axis=-1, keepdims=True) + NORM_EPS) * fg_ref[...]
    o_ref[...] = xn


def _out_proj(ys, ya, w_out_b, x, gate, final_g, layer, final, tm):
    bsz, s, d = x.shape
    width = ys.shape[-1]
    return pl.pallas_call(
        functools.partial(_out_proj_kernel, final=final),
        grid=(bsz, s // tm),
        in_specs=[pl.BlockSpec((None, tm, width), lambda b, m: (b, m, 0)),
                  pl.BlockSpec((None, tm, width), lambda b, m: (b, m, 0)),
                  pl.BlockSpec((None, width, d), lambda b, m: (layer, 0, 0)),
                  pl.BlockSpec((None, width, d), lambda b, m: (layer, 1, 0)),
                  pl.BlockSpec((None, tm, d), lambda b, m: (b, m, 0)),
                  pl.BlockSpec((None, 1, d), lambda b, m: (b, 0, 0)),
                  pl.BlockSpec((1, d), lambda b, m: (0, 0))],
        out_specs=pl.BlockSpec((None, tm, d), lambda b, m: (b, m, 0)),
        out_shape=jax.ShapeDtypeStruct((bsz, s, d), F32),
        compiler_params=_params("arbitrary", "arbitrary"),
        name="out_proj",
    )(ys, ya, w_out_b, w_out_b, x, gate, final_g)


def _qk_relayout(w):
    half = HEAD_DIM // 2
    lead = w.shape[:-1]
    w = w.reshape(*lead, -1, 2, 2, 2, half)
    nd = len(lead)
    w = w.transpose(*range(nd), nd, nd + 3, nd + 1, nd + 2, nd + 4)
    return w.reshape(*lead, -1)


def _tile(n, target):
    t = min(n, target)
    while n % t or t % LANES:
        t -= LANES
    return t


def kernel(x, c, positions, norm_g, w_ada, b_ada, w_in, w_out, ssm_a_re, ssm_a_im, ssm_b_re, ssm_b_im,
           ssm_c_re, ssm_c_im, ssm_d, ssm_log_step, w_glu, b_glu, lam_q1, lam_k1, lam_q2, lam_k2,
           sub_g, final_g):
    bsz, s, d = x.shape
    depth = w_in.shape[0]
    width = d // 2
    groups = width // SSM_GROUP
    nchunks = s // CHUNK
    assert w_in.shape[-1] == 6 * width and width % PAIR == 0 and s % CHUNK == 0 and bsz % SUBLANES == 0
    tm = _tile(s, 512)
    tq = _tile(s, 512)

    w_in_b = w_in.astype(BF16)
    w_in_p = jnp.concatenate(
        [w_in_b[:, :, :2 * width], _qk_relayout(w_in_b[:, :, 2 * width:3 * width]),
         _qk_relayout(w_in_b[:, :, 3 * width:4 * width]), w_in_b[:, :, 4 * width:]], axis=-1)
    w_ut = jnp.swapaxes(w_in_b[:, :, :width], 1, 2)
    w_out_b = w_out.astype(BF16)
    w_glu_b = w_glu.astype(BF16)
    toep, bst, cst, avec = _s5_tables(ssm_a_re, ssm_a_im, ssm_b_re, ssm_b_im, ssm_c_re, ssm_c_im,
                                      ssm_d, ssm_log_step)
    half = HEAD_DIM // 2
    inv_freq = ROPE_THETA ** (-jnp.arange(half, dtype=F32) / half)
    ang = positions.astype(F32)[..., None] * inv_freq
    cos_t = jnp.tile(jnp.cos(ang), (1, 1, LANES // half))
    sin_t = jnp.tile(jnp.sin(ang), (1, 1, LANES // half))

    mod = _ada_mod(c, w_ada, b_ada).reshape(depth, bsz, 1, 3 * d)
    norm_g3 = norm_g.reshape(depth, 1, d)
    b_glu3 = b_glu.reshape(depth, 1, 2 * width)
    lam3 = [v.reshape(depth, 1, HEAD_DIM) for v in (lam_q1, lam_k1, lam_q2, lam_k2)]
    sub_g3 = sub_g.reshape(depth, 1, 2 * HEAD_DIM)
    final_g2 = final_g.reshape(1, d)

    for l in range(depth):
        lambda_init = 0.8 - 0.6 * math.exp(-0.3 * l)
        shift, scale, gate = mod[l, :, :, :d], mod[l, :, :, d:2 * d], mod[l, :, :, 2 * d:]
        at, proj = _in_proj(x, shift, scale, norm_g3, w_in_p, w_ut, l, cos_t, sin_t)
        gt = _s5_core(at, toep, bst, cst, avec, l)
        ys = _glu(gt, proj, w_glu_b, b_glu3, l)
        ya = _attention(proj, *lam3, sub_g3, l, lambda_init, tq)
        x = _out_proj(ys, ya, w_out_b, x, gate, final_g2, l, l == depth - 1, tm)
    return x
```

```python
import functools
import math

import jax
import jax.numpy as jnp
import numpy as np
from jax import lax
from jax.experimental import pallas as pl
from jax.experimental.pallas import tpu as pltpu

LANES = 128
SUBLANES = 8
V7X_VMEM_REQUEST_BYTES = 56 * 1024 * 1024

SSM_GROUP = 16
SSM_STATE = 64
HEAD_DIM = 64
ROPE_THETA = 10000.0
NORM_EPS = 1e-6
CHUNK = 16
PAIR = 4 * HEAD_DIM
TK = 256
TOKEN_TILE = 512
Q_SCALE = HEAD_DIM ** -0.5 * math.log2(math.e)

F32 = jnp.float32
BF16 = jnp.bfloat16


def _params(*sem):
    return pltpu.CompilerParams(dimension_semantics=sem, vmem_limit_bytes=V7X_VMEM_REQUEST_BYTES)


def _ada_kernel(c_ref, w_ref, b_ref, o_ref):
    c = c_ref[...]
    act = c * jax.nn.sigmoid(c)
    o_ref[...] = jnp.dot(act, w_ref[...], preferred_element_type=F32,
                         precision=lax.Precision.HIGHEST) + b_ref[...]


def _ada_mod(c, w_ada, b_ada):
    depth, d, n3 = w_ada.shape
    bsz = c.shape[0]
    tn = _tile(n3, 1024)
    return pl.pallas_call(
        _ada_kernel,
        grid=(depth, n3 // tn),
        in_specs=[pl.BlockSpec((bsz, d), lambda l, n: (0, 0)),
                  pl.BlockSpec((None, d, tn), lambda l, n: (l, 0, n)),
                  pl.BlockSpec((None, 1, tn), lambda l, n: (l, 0, n))],
        out_specs=pl.BlockSpec((None, bsz, tn), lambda l, n: (l, 0, n)),
        out_shape=jax.ShapeDtypeStruct((depth, bsz, n3), F32),
        compiler_params=_params("arbitrary", "arbitrary"),
        name="ada_mod",
    )(c, w_ada, b_ada.reshape(depth, 1, n3))


def _in_proj_kernel(x_ref, shift_ref, scale_ref, g_ref, w_ref, wut_ref, cos_ref, sin_ref, at_ref, o_ref, h_ref,
                    *, npairs, nc):
    n = pl.program_id(2)
    width = w_ref.shape[-1]

    @pl.when(n == 0)
    def _():
        for t in range(2):
            xf = x_ref[t]
            y = xf * lax.rsqrt(jnp.mean(xf * xf, axis=-1, keepdims=True) + NORM_EPS) * g_ref[...]
            h = (y * (1.0 + scale_ref[...]) + shift_ref[...]).astype(BF16)
            h_ref[t * nc:(t + 1) * nc, :] = h
            ut = lax.dot_general(wut_ref[...], h, (((1,), (1,)), ((), ())), preferred_element_type=F32)
            at_ref[:, t * SSM_GROUP:(t + 1) * SSM_GROUP, :] = ut.reshape(-1, SSM_GROUP, nc).astype(BF16)

    @pl.when(n > 0)
    def _():
        acc = jnp.dot(h_ref[...], w_ref[...], preferred_element_type=F32)
        is_rope = jnp.logical_or(n == 2, n == 3)

        @pl.when(is_rope)
        def _():
            qs = jnp.where(n == 2, Q_SCALE, 1.0).astype(F32)
            for t in range(2):
                cs = cos_ref[t] * qs
                sn = sin_ref[t] * qs
                for j in range(npairs):
                    lo = j * PAIR
                    t1 = acc[t * nc:(t + 1) * nc, lo:lo + LANES]
                    t2 = acc[t * nc:(t + 1) * nc, lo + LANES:lo + PAIR]
                    o_ref[t, :, lo:lo + LANES] = (t1 * cs - t2 * sn).astype(BF16)
                    o_ref[t, :, lo + LANES:lo + PAIR] = (t2 * cs + t1 * sn).astype(BF16)

        @pl.when(jnp.logical_not(is_rope))
        def _():
            for t in range(2):
                o_ref[t] = acc[t * nc:(t + 1) * nc, :].astype(BF16)


def _in_proj(x, shift, scale, norm_g, w_in_p, w_ut, layer, cos_t, sin_t):
    bsz, _, nc, d = x.shape
    width = w_in_p.shape[-1] // 6
    npairs = width // PAIR
    groups = width // SSM_GROUP
    seg = lambda n: jnp.maximum(n, 1)
    return pl.pallas_call(
        functools.partial(_in_proj_kernel, npairs=npairs, nc=nc),
        grid=(bsz, CHUNK // 2, 6),
        in_specs=[pl.BlockSpec((None, 2, nc, d), lambda b, ip, n: (b, ip, 0, 0)),
                  pl.BlockSpec((None, 1, d), lambda b, ip, n: (b, 0, 0)),
                  pl.BlockSpec((None, 1, d), lambda b, ip, n: (b, 0, 0)),
                  pl.BlockSpec((None, 1, d), lambda b, ip, n: (layer, 0, 0)),
                  pl.BlockSpec((None, d, width), lambda b, ip, n: (layer, 0, seg(n))),
                  pl.BlockSpec((None, width, d), lambda b, ip, n: (layer, 0, 0)),
                  pl.BlockSpec((None, 2, nc, LANES), lambda b, ip, n: (b, ip, 0, 0)),
                  pl.BlockSpec((None, 2, nc, LANES), lambda b, ip, n: (b, ip, 0, 0))],
        out_specs=[pl.BlockSpec((groups, None, 2 * SSM_GROUP, nc), lambda b, ip, n: (0, b, ip, 0)),
                   pl.BlockSpec((None, None, 2, nc, width), lambda b, ip, n: (seg(n) - 1, b, ip, 0, 0))],
        out_shape=[jax.ShapeDtypeStruct((groups, bsz, CHUNK * SSM_GROUP, nc), BF16),
                   jax.ShapeDtypeStruct((5, bsz, CHUNK, nc, width), BF16)],
        scratch_shapes=[pltpu.VMEM((2 * nc, d), BF16)],
        compiler_params=_params("arbitrary", "arbitrary", "arbitrary"),
        name="in_proj",
    )(x, shift, scale, norm_g, w_in_p, w_ut, cos_t, sin_t)


def _gelu_tanh(y):
    return 0.5 * y * (1.0 + jnp.tanh(math.sqrt(2.0 / math.pi) * (y + 0.044715 * (y * y * y))))


def _s5_kernel(at_ref, toep_ref, bst_ref, cst_ref, av_ref, gt_ref, a_ref, sb_ref, xp_ref, *, bsz, nchunks):
    ns2 = 2 * SSM_STATE
    kk = CHUNK * SSM_GROUP
    for b in range(bsz):
        a_ref[b * nchunks:(b + 1) * nchunks, :] = at_ref[b].T
    a = a_ref[...]
    sb = jnp.dot(a, bst_ref[...], preferred_element_type=F32)
    sb_ref[...] = jnp.swapaxes(sb.reshape(bsz, nchunks, 2 * ns2), 0, 1).reshape(nchunks * bsz, 2 * ns2)
    ar = av_ref[0:1, :]
    ai1 = av_ref[1:2, :]
    ai2 = av_ref[2:3, :]

    def body(c, carry):
        xs, xw = carry
        r = pl.multiple_of(c * bsz, bsz)
        xp_ref[pl.ds(r, bsz), :] = xs
        sb = sb_ref[pl.ds(r, bsz), :]
        return (ar * xs + ai1 * xw + sb[:, :ns2], ar * xw + ai2 * xs + sb[:, ns2:])

    zero = jnp.zeros((bsz, ns2), F32)
    lax.fori_loop(0, nchunks, body, (zero, zero), unroll=8)
    xp = jnp.swapaxes(xp_ref[...].reshape(nchunks, bsz, ns2), 0, 1).reshape(bsz * nchunks, ns2)
    y = jnp.dot(a, toep_ref[...], preferred_element_type=F32)
    y = y + jnp.dot(xp.astype(BF16), cst_ref[...], preferred_element_type=F32)
    gy = _gelu_tanh(y).astype(BF16)
    for b in range(bsz):
        gt_ref[b] = gy[b * nchunks:(b + 1) * nchunks, :].T


def _s5_core(at, toep, bst, cst, avec, layer):
    g, bsz, kk, nchunks = at.shape
    ns2 = 2 * SSM_STATE
    m = bsz * nchunks
    return pl.pallas_call(
        functools.partial(_s5_kernel, bsz=bsz, nchunks=nchunks),
        grid=(g,),
        in_specs=[pl.BlockSpec((None, bsz, kk, nchunks), lambda i: (i, 0, 0, 0)),
                  pl.BlockSpec((None, None, kk, kk), lambda i: (layer, i, 0, 0)),
                  pl.BlockSpec((None, None, kk, 2 * ns2), lambda i: (layer, i, 0, 0)),
                  pl.BlockSpec((None, None, ns2, kk), lambda i: (layer, i, 0, 0)),
                  pl.BlockSpec((None, None, 4, ns2), lambda i: (layer, i, 0, 0))],
        out_specs=pl.BlockSpec((None, bsz, kk, nchunks), lambda i: (i, 0, 0, 0)),
        out_shape=jax.ShapeDtypeStruct((g, bsz, kk, nchunks), BF16),
        scratch_shapes=[pltpu.VMEM((m, kk), BF16), pltpu.VMEM((m, 2 * ns2), F32), pltpu.VMEM((m, ns2), F32)],
        compiler_params=_params("arbitrary"),
        name="s5_core",
    )(at, toep, bst, cst, avec)


def _s5_tables(a_re, a_im, b_re, b_im, c_re, c_im, d_skip, log_step):
    hi = lax.Precision.HIGHEST
    step = jnp.exp(log_step)[..., None]
    lr, li = a_re * step, a_im * step
    tau = jnp.arange(CHUNK + 1, dtype=F32)[:, None]
    mag = jnp.exp(tau * lr[..., None, :])
    pw_re = mag * jnp.cos(tau * li[..., None, :])
    pw_im = mag * jnp.sin(tau * li[..., None, :])
    num_re, num_im = pw_re[..., 1, :] - 1.0, pw_im[..., 1, :]
    den = a_re * a_re + a_im * a_im
    cf_re = (num_re * a_re + num_im * a_im) / den
    cf_im = (num_im * a_re - num_re * a_im) / den
    bb_re = cf_re[..., None] * b_re - cf_im[..., None] * b_im
    bb_im = cf_re[..., None] * b_im + cf_im[..., None] * b_re
    e_re = pw_re[..., None] * bb_re[..., None, :, :] - pw_im[..., None] * bb_im[..., None, :, :]
    e_im = pw_re[..., None] * bb_im[..., None, :, :] + pw_im[..., None] * bb_re[..., None, :, :]
    kern = (jnp.einsum('lgpn,lgtnq->lgtpq', c_re, e_re, precision=hi)
            - jnp.einsum('lgpn,lgtnq->lgtpq', c_im, e_im, precision=hi))
    eye = jnp.eye(SSM_GROUP, dtype=F32)
    kern = kern.at[:, :, 0].add(d_skip[..., :, None] * eye)
    idx = jnp.arange(CHUNK)
    lag = idx[None, :] - idx[:, None]
    blk = jnp.where((lag >= 0)[..., None, None], kern[:, :, jnp.clip(lag, 0)], 0.0)
    dp, g = a_re.shape[:2]
    kk = CHUNK * SSM_GROUP
    toep = blk.transpose(0, 1, 2, 5, 3, 4).reshape(dp, g, kk, kk)
    back = CHUNK - 1 - idx
    bs_re = e_re[:, :, back].transpose(0, 1, 2, 4, 3).reshape(dp, g, kk, SSM_STATE)
    bs_im = e_im[:, :, back].transpose(0, 1, 2, 4, 3).reshape(dp, g, kk, SSM_STATE)
    bst = jnp.concatenate([bs_re, bs_im, bs_im, bs_re], axis=-1)
    fw_re, fw_im = pw_re[:, :, 1:], pw_im[:, :, 1:]
    ca_re = c_re[:, :, None] * fw_re[..., None, :] - c_im[:, :, None] * fw_im[..., None, :]
    ca_im = c_re[:, :, None] * fw_im[..., None, :] + c_im[:, :, None] * fw_re[..., None, :]
    cst = jnp.concatenate([ca_re.transpose(0, 1, 4, 2, 3).reshape(dp, g, SSM_STATE, kk),
                           -ca_im.transpose(0, 1, 4, 2, 3).reshape(dp, g, SSM_STATE, kk)], axis=2)
    ar, ai = pw_re[:, :, CHUNK], pw_im[:, :, CHUNK]
    avec = jnp.stack([jnp.concatenate([ar, ar], -1), jnp.concatenate([-ai, ai], -1),
                      jnp.concatenate([ai, -ai], -1), jnp.zeros_like(jnp.concatenate([ar, ar], -1))], axis=2)
    return toep.astype(BF16), bst.astype(BF16), cst.astype(BF16), avec


def _glu_kernel(gt_ref, zs_ref, w_ref, b_ref, o_ref, *, width, nc):
    gt = gt_ref[...]
    gy = jnp.concatenate([gt[:, t * SSM_GROUP:(t + 1) * SSM_GROUP, :].reshape(width, nc).T for t in range(2)],
                         axis=0)
    ab = jnp.dot(gy, w_ref[...], preferred_element_type=F32) + b_ref[...]
    out = ab[:, :width] * jax.nn.sigmoid(ab[:, width:])
    for t in range(2):
        z = zs_ref[t].astype(F32)
        o_ref[t] = (out[t * nc:(t + 1) * nc, :] * (z * jax.nn.sigmoid(z))).astype(BF16)


def _glu(gt, proj, w_glu_b, b_glu, layer):
    groups, bsz, kk, nc = gt.shape
    width = groups * SSM_GROUP
    return pl.pallas_call(
        functools.partial(_glu_kernel, width=width, nc=nc),
        grid=(bsz, CHUNK // 2),
        in_specs=[pl.BlockSpec((groups, None, 2 * SSM_GROUP, nc), lambda b, ip: (0, b, ip, 0)),
                  pl.BlockSpec((None, None, 2, nc, width), lambda b, ip: (0, b, ip, 0, 0)),
                  pl.BlockSpec((None, width, 2 * width), lambda b, ip: (layer, 0, 0)),
                  pl.BlockSpec((None, 1, 2 * width), lambda b, ip: (layer, 0, 0))],
        out_specs=pl.BlockSpec((None, 2, nc, width), lambda b, ip: (b, ip, 0, 0)),
        out_shape=jax.ShapeDtypeStruct((bsz, CHUNK, nc, width), BF16),
        compiler_params=_params("arbitrary", "arbitrary"),
        name="s5_glu",
    )(gt, proj, w_glu_b, b_glu)


def _attn_kernel(linit_ref, q_ref, k_ref, v_ref, za_ref, lq1_ref, lk1_ref, lq2_ref, lk2_ref, sg_ref,
                 o_ref, km_ref, vt_ref, qt_ref, m_ref, l_ref, acc_ref, s0_ref, s1_ref, p0_ref, p1_ref,
                 a0_ref, a1_ref, c0_ref, c1_ref, *, tq):
    qi = pl.program_id(2)
    nq = tq // TK

    @pl.when(qi == 0)
    def _():
        kk = _natural(k_ref[...])
        grp = (lax.broadcasted_iota(jnp.int32, kk.shape, 1) % LANES) // (HEAD_DIM // 2)
        for combo in range(4):
            km_ref[combo] = jnp.where(grp == combo, kk, jnp.zeros_like(kk))
        vt_ref[...] = _natural(v_ref[...]).astype(F32).T.astype(BF16)

    qt_ref[...] = _natural(q_ref[...]).astype(F32).T.astype(BF16)
    m_ref[...] = jnp.full(m_ref.shape, -jnp.inf, F32)
    l_ref[...] = jnp.zeros(l_ref.shape, F32)
    acc_ref[...] = jnp.zeros(acc_ref.shape, F32)

    def scores(kt, s_ref, cm_ref, c0):
        ks = pl.multiple_of(kt * TK, TK)
        qt = qt_ref[:, c0:]
        for combo in range(4):
            s = jnp.dot(km_ref[combo, pl.ds(ks, TK), :], qt, preferred_element_type=F32)
            s_ref[combo, :, c0:] = s
            cm_ref[combo, :, c0:] = jnp.max(s, axis=0, keepdims=True)

    def softmax(s_ref, cm_ref, p_ref, a_ref, c0, shift):
        for combo in range(4):
            for c in range(c0, tq, LANES):
                cs = slice(c, c + LANES)
                if shift is not None:
                    row = lax.broadcasted_iota(jnp.int32, (TK, LANES), 0)
                    col = lax.broadcasted_iota(jnp.int32, (TK, LANES), 1)
                    s = jnp.where(row + (shift - c) <= col, s_ref[combo, :, cs], -jnp.inf)
                    cmax = jnp.max(s, axis=0, keepdims=True)
                else:
                    s = s_ref[combo, :, cs]
                    cmax = cm_ref[combo, :, cs]
                m_prev = m_ref[combo, :, cs]
                m_new = jnp.maximum(m_prev, cmax)
                alpha = jnp.exp2(m_prev - m_new)
                p = jnp.exp2(s - m_new)
                l_ref[combo, :, cs] = alpha * l_ref[combo, :, cs] + jnp.sum(p, axis=0, keepdims=True)
                p_ref[combo, :, cs] = p.astype(BF16)
                a_ref[combo, :, cs] = alpha
                m_ref[combo, :, cs] = m_new

    def values(kt, p_ref, a_ref, c0):
        ks = pl.multiple_of(kt * TK, TK)
        cs = slice(c0, tq)
        for combo in range(4):
            hp = combo // 2
            pv = jnp.dot(vt_ref[hp * LANES:(hp + 1) * LANES, pl.ds(ks, TK)], p_ref[combo, :, cs],
                         preferred_element_type=F32)
            acc_ref[combo, :, cs] = a_ref[combo, :, cs] * acc_ref[combo, :, cs] + pv

    scores(0, s0_ref, c0_ref, 0)

    def pair(i, carry):
        kt = 2 * i
        scores(kt + 1, s1_ref, c1_ref, 0)
        softmax(s0_ref, c0_ref, p0_ref, a0_ref, 0, None)
        values(kt, p0_ref, a0_ref, 0)
        scores(kt + 2, s0_ref, c0_ref, 0)
        softmax(s1_ref, c1_ref, p1_ref, a1_ref, 0, None)
        values(kt + 1, p1_ref, a1_ref, 0)
        return carry

    lax.fori_loop(0, qi * (nq // 2), pair, 0)
    kd = qi * nq
    bufs = ((s0_ref, c0_ref, p0_ref, a0_ref), (s1_ref, c1_ref, p1_ref, a1_ref))
    for d in range(nq):
        s_ref, cm_ref, p_ref, a_ref = bufs[d % 2]
        if d > 0:
            scores(kd + d, s_ref, cm_ref, d * TK)
        softmax(s_ref, cm_ref, p_ref, a_ref, d * TK, d * TK)
        values(kd + d, p_ref, a_ref, d * TK)

    linit = linit_ref[0]
    lam = (jnp.exp(jnp.sum(lq1_ref[...] * lk1_ref[...], axis=-1, keepdims=True))
           - jnp.exp(jnp.sum(lq2_ref[...] * lk2_ref[...], axis=-1, keepdims=True)) + linit)
    za = _natural(za_ref[...]).astype(F32)
    for hp in range(2):
        ot = acc_ref[2 * hp] / l_ref[2 * hp] - lam * (acc_ref[2 * hp + 1] / l_ref[2 * hp + 1])
        o = ot.T
        y = o * lax.rsqrt(jnp.mean(o * o, axis=-1, keepdims=True) + NORM_EPS) * sg_ref[...]
        y = y * (1.0 - linit)
        z = za[:, hp * LANES:(hp + 1) * LANES]
        y = y * (z * jax.nn.sigmoid(z))
        o_ref[:, :, hp * LANES:(hp + 1) * LANES] = jnp.swapaxes(
            y.reshape(tq // CHUNK, CHUNK, LANES), 0, 1).astype(BF16)


def _natural(v):
    return jnp.swapaxes(v, 0, 1).reshape(v.shape[0] * v.shape[1], v.shape[2])


def _attention(proj, lam_q1, lam_k1, lam_q2, lam_k2, sub_g, layer, lambda_init, tq):
    _, bsz, _, nc, width = proj.shape
    s = nc * CHUNK
    tc = tq // CHUNK
    npairs = width // PAIR
    assert tq % (2 * TK) == 0
    vec = lambda: pl.BlockSpec((None, 1, HEAD_DIM), lambda b, j, i: (layer, 0, 0))
    sbuf = lambda: pltpu.VMEM((4, TK, tq), F32)
    pbuf = lambda: pltpu.VMEM((4, TK, tq), BF16)
    rowv = lambda: pltpu.VMEM((4, 1, tq), F32)
    return pl.pallas_call(
        functools.partial(_attn_kernel, tq=tq),
        grid=(bsz, npairs, s // tq),
        in_specs=[pl.BlockSpec(memory_space=pltpu.SMEM),
                  pl.BlockSpec((None, None, CHUNK, tc, PAIR), lambda b, j, i: (1, b, 0, i, j)),
                  pl.BlockSpec((None, None, CHUNK, nc, PAIR), lambda b, j, i: (2, b, 0, 0, j)),
                  pl.BlockSpec((None, None, CHUNK, nc, PAIR), lambda b, j, i: (3, b, 0, 0, j)),
                  pl.BlockSpec((None, None, CHUNK, tc, PAIR), lambda b, j, i: (4, b, 0, i, j)),
                  vec(), vec(), vec(), vec(),
                  pl.BlockSpec((None, 1, 2 * HEAD_DIM), lambda b, j, i: (layer, 0, 0))],
        out_specs=pl.BlockSpec((None, CHUNK, tc, PAIR), lambda b, j, i: (b, 0, i, j)),
        out_shape=jax.ShapeDtypeStruct((bsz, CHUNK, nc, width), BF16),
        scratch_shapes=[pltpu.VMEM((4, s, PAIR), BF16), pltpu.VMEM((PAIR, s), BF16), pltpu.VMEM((PAIR, tq), BF16),
                        rowv(), rowv(), pltpu.VMEM((4, LANES, tq), F32),
                        sbuf(), sbuf(), pbuf(), pbuf(), rowv(), rowv(), rowv(), rowv()],
        compiler_params=_params("arbitrary", "arbitrary", "arbitrary"),
        name="diff_attn",
    )(jnp.full((1,), lambda_init, F32), proj, proj, proj, proj, lam_q1, lam_k1, lam_q2, lam_k2, sub_g)


def _out_proj_kernel(ys_ref, ya_ref, ws_ref, wa_ref, x_ref, gate_ref, fg_ref, o_ref, *, final):
    pb, cb, width = ys_ref.shape
    rows = pb * cb
    y = jnp.dot(ys_ref[...].reshape(rows, width), ws_ref[...], preferred_element_type=F32)
    y = y + jnp.dot(ya_ref[...].reshape(rows, width), wa_ref[...], preferred_element_type=F32)
    xn = x_ref[...].reshape(rows, -1) + gate_ref[...] * y
    if final:
        xn = xn * lax.rsqrt(jnp.mean(xn * xn, axis=-1, keepdims=True) + NORM_EPS) * fg_ref[...]
        o_ref[...] = _natural(xn.reshape(pb, cb, -1))
    else:
        o_ref[...] = xn.reshape(pb, cb, -1)


def _out_proj(ys, ya, w_out_b, x, gate, final_g, layer, final):
    bsz, _, nc, d = x.shape
    width = ys.shape[-1]
    if final:
        pb, cb = CHUNK, TOKEN_TILE // CHUNK
        out_spec = pl.BlockSpec((None, TOKEN_TILE, d), lambda b, p, m: (b, m, 0))
        out_shape = jax.ShapeDtypeStruct((bsz, nc * CHUNK, d), F32)
    else:
        pb, cb = TOKEN_TILE // nc, nc
        out_spec = pl.BlockSpec((None, pb, cb, d), lambda b, p, m: (b, p, m, 0))
        out_shape = jax.ShapeDtypeStruct(x.shape, F32)
    tile = lambda last: pl.BlockSpec((None, pb, cb, last), lambda b, p, m: (b, p, m, 0))
    return pl.pallas_call(
        functools.partial(_out_proj_kernel, final=final),
        grid=(bsz, CHUNK // pb, nc // cb),
        in_specs=[tile(width), tile(width),
                  pl.BlockSpec((None, width, d), lambda b, p, m: (layer, 0, 0)),
                  pl.BlockSpec((None, width, d), lambda b, p, m: (layer, 1, 0)),
                  tile(d),
                  pl.BlockSpec((None, 1, d), lambda b, p, m: (b, 0, 0)),
                  pl.BlockSpec((1, d), lambda b, p, m: (0, 0))],
        out_specs=out_spec,
        out_shape=out_shape,
        compiler_params=_params("arbitrary", "arbitrary", "arbitrary"),
        name="out_proj",
    )(ys, ya, w_out_b, w_out_b, x, gate, final_g)


def _to_phase_major_kernel(x_ref, o_ref):
    rows, d = x_ref.shape
    o_ref[...] = jnp.swapaxes(x_ref[...].reshape(rows // CHUNK, CHUNK, d), 0, 1)


def _to_phase_major(x):
    bsz, s, d = x.shape
    tc = TOKEN_TILE // CHUNK
    return pl.pallas_call(
        _to_phase_major_kernel,
        grid=(bsz, s // TOKEN_TILE),
        in_specs=[pl.BlockSpec((None, TOKEN_TILE, d), lambda b, m: (b, m, 0))],
        out_specs=pl.BlockSpec((None, CHUNK, tc, d), lambda b, m: (b, 0, m, 0)),
        out_shape=jax.ShapeDtypeStruct((bsz, CHUNK, s // CHUNK, d), F32),
        compiler_params=_params("arbitrary", "arbitrary"),
        name="to_phase_major",
    )(x)


def _qk_relayout(w):
    half = HEAD_DIM // 2
    lead = w.shape[:-1]
    w = w.reshape(*lead, -1, 2, 2, 2, half)
    nd = len(lead)
    w = w.transpose(*range(nd), nd, nd + 3, nd + 1, nd + 2, nd + 4)
    return w.reshape(*lead, -1)


def _tile(n, target):
    t = min(n, target)
    while n % t or t % LANES:
        t -= LANES
    return t


def kernel(x, c, positions, norm_g, w_ada, b_ada, w_in, w_out, ssm_a_re, ssm_a_im, ssm_b_re, ssm_b_im,
           ssm_c_re, ssm_c_im, ssm_d, ssm_log_step, w_glu, b_glu, lam_q1, lam_k1, lam_q2, lam_k2,
           sub_g, final_g):
    bsz, s, d = x.shape
    depth = w_in.shape[0]
    width = d // 2
    nchunks = s // CHUNK
    assert w_in.shape[-1] == 6 * width and width % PAIR == 0 and bsz % SUBLANES == 0
    assert s % TOKEN_TILE == 0 and TOKEN_TILE % nchunks == 0

    w_in_b = w_in.astype(BF16)
    w_in_p = jnp.concatenate(
        [w_in_b[:, :, :2 * width], _qk_relayout(w_in_b[:, :, 2 * width:3 * width]),
         _qk_relayout(w_in_b[:, :, 3 * width:4 * width]), w_in_b[:, :, 4 * width:]], axis=-1)
    w_ut = jnp.swapaxes(w_in_b[:, :, :width], 1, 2)
    w_out_b = w_out.astype(BF16)
    w_glu_b = w_glu.astype(BF16)
    toep, bst, cst, avec = _s5_tables(ssm_a_re, ssm_a_im, ssm_b_re, ssm_b_im, ssm_c_re, ssm_c_im,
                                      ssm_d, ssm_log_step)
    half = HEAD_DIM // 2
    inv_freq = ROPE_THETA ** (-jnp.arange(half, dtype=F32) / half)
    pos_pm = jnp.swapaxes(positions.reshape(bsz, nchunks, CHUNK), 1, 2)
    ang = pos_pm.astype(F32)[..., None] * inv_freq
    cos_t = jnp.tile(jnp.cos(ang), (1, 1, 1, LANES // half))
    sin_t = jnp.tile(jnp.sin(ang), (1, 1, 1, LANES // half))

    mod = _ada_mod(c, w_ada, b_ada).reshape(depth, bsz, 1, 3 * d)
    norm_g3 = norm_g.reshape(depth, 1, d)
    b_glu3 = b_glu.reshape(depth, 1, 2 * width)
    lam3 = [v.reshape(depth, 1, HEAD_DIM) for v in (lam_q1, lam_k1, lam_q2, lam_k2)]
    sub_g3 = sub_g.reshape(depth, 1, 2 * HEAD_DIM)
    final_g2 = final_g.reshape(1, d)

    x = _to_phase_major(x)
    for l in range(depth):
        lambda_init = 0.8 - 0.6 * math.exp(-0.3 * l)
        shift, scale, gate = mod[l, :, :, :d], mod[l, :, :, d:2 * d], mod[l, :, :, 2 * d:]
        at, proj = _in_proj(x, shift, scale, norm_g3, w_in_p, w_ut, l, cos_t, sin_t)
        gt = _s5_core(at, toep, bst, cst, avec, l)
        ys = _glu(gt, proj, w_glu_b, b_glu3, l)
        ya = _attention(proj, *lam3, sub_g3, l, lambda_init, TOKEN_TILE)
        x = _out_proj(ys, ya, w_out_b, x, gate, final_g2, l, l == depth - 1)
    return x
```

```python
import functools
import math

import jax
import jax.numpy as jnp
import numpy as np
from jax import lax
from jax.experimental import pallas as pl
from jax.experimental.pallas import tpu as pltpu

LANES = 128
SUBLANES = 8
V7X_VMEM_REQUEST_BYTES = 56 * 1024 * 1024

SSM_GROUP = 16
SSM_STATE = 64
HEAD_DIM = 64
ROPE_THETA = 10000.0
NORM_EPS = 1e-6
CHUNK = 16
PAIR = 4 * HEAD_DIM
TK = 256
TOKEN_TILE = 512
Q_SCALE = HEAD_DIM ** -0.5 * math.log2(math.e)

F32 = jnp.float32
BF16 = jnp.bfloat16


def _params(*sem):
    return pltpu.CompilerParams(dimension_semantics=sem, vmem_limit_bytes=V7X_VMEM_REQUEST_BYTES)


def _ada_kernel(c_ref, w_ref, b_ref, o_ref):
    c = c_ref[...]
    act = c * jax.nn.sigmoid(c)
    o_ref[...] = jnp.dot(act, w_ref[...], preferred_element_type=F32,
                         precision=lax.Precision.HIGHEST) + b_ref[...]


def _ada_mod(c, w_ada, b_ada):
    depth, d, n3 = w_ada.shape
    bsz = c.shape[0]
    tn = _tile(n3, 1024)
    return pl.pallas_call(
        _ada_kernel,
        grid=(depth, n3 // tn),
        in_specs=[pl.BlockSpec((bsz, d), lambda l, n: (0, 0)),
                  pl.BlockSpec((None, d, tn), lambda l, n: (l, 0, n)),
                  pl.BlockSpec((None, 1, tn), lambda l, n: (l, 0, n))],
        out_specs=pl.BlockSpec((None, bsz, tn), lambda l, n: (l, 0, n)),
        out_shape=jax.ShapeDtypeStruct((depth, bsz, n3), F32),
        compiler_params=_params("arbitrary", "arbitrary"),
        name="ada_mod",
    )(c, w_ada, b_ada.reshape(depth, 1, n3))


def _in_proj_kernel(x_ref, shift_ref, scale_ref, g_ref, w_ref, wut_ref, cos_ref, sin_ref, at_ref, o_ref, h_ref,
                    *, npairs, nc):
    n = pl.program_id(2)
    width = w_ref.shape[-1]

    @pl.when(n == 0)
    def _():
        for t in range(2):
            xf = x_ref[t]
            y = xf * lax.rsqrt(jnp.mean(xf * xf, axis=-1, keepdims=True) + NORM_EPS) * g_ref[...]
            h = (y * (1.0 + scale_ref[...]) + shift_ref[...]).astype(BF16)
            h_ref[t * nc:(t + 1) * nc, :] = h
            ut = lax.dot_general(wut_ref[...], h, (((1,), (1,)), ((), ())), preferred_element_type=F32)
            at_ref[:, t * SSM_GROUP:(t + 1) * SSM_GROUP, :] = ut.reshape(-1, SSM_GROUP, nc).astype(BF16)

    @pl.when(n > 0)
    def _():
        acc = jnp.dot(h_ref[...], w_ref[...], preferred_element_type=F32)
        is_rope = jnp.logical_or(n == 2, n == 3)

        @pl.when(is_rope)
        def _():
            qs = jnp.where(n == 2, Q_SCALE, 1.0).astype(F32)
            for t in range(2):
                cs = cos_ref[t] * qs
                sn = sin_ref[t] * qs
                for j in range(npairs):
                    lo = j * PAIR
                    t1 = acc[t * nc:(t + 1) * nc, lo:lo + LANES]
                    t2 = acc[t * nc:(t + 1) * nc, lo + LANES:lo + PAIR]
                    o_ref[t, :, lo:lo + LANES] = (t1 * cs - t2 * sn).astype(BF16)
                    o_ref[t, :, lo + LANES:lo + PAIR] = (t2 * cs + t1 * sn).astype(BF16)

        @pl.when(jnp.logical_not(is_rope))
        def _():
            for t in range(2):
                o_ref[t] = acc[t * nc:(t + 1) * nc, :].astype(BF16)


def _in_proj(x, shift, scale, norm_g, w_in_p, w_ut, layer, cos_t, sin_t):
    bsz, _, nc, d = x.shape
    width = w_in_p.shape[-1] // 6
    npairs = width // PAIR
    groups = width // SSM_GROUP
    seg = lambda n: jnp.maximum(n, 1)
    return pl.pallas_call(
        functools.partial(_in_proj_kernel, npairs=npairs, nc=nc),
        grid=(bsz, CHUNK // 2, 6),
        in_specs=[pl.BlockSpec((None, 2, nc, d), lambda b, ip, n: (b, ip, 0, 0)),
                  pl.BlockSpec((None, 1, d), lambda b, ip, n: (b, 0, 0)),
                  pl.BlockSpec((None, 1, d), lambda b, ip, n: (b, 0, 0)),
                  pl.BlockSpec((None, 1, d), lambda b, ip, n: (layer, 0, 0)),
                  pl.BlockSpec((None, d, width), lambda b, ip, n: (layer, 0, seg(n))),
                  pl.BlockSpec((None, width, d), lambda b, ip, n: (layer, 0, 0)),
                  pl.BlockSpec((None, 2, nc, LANES), lambda b, ip, n: (b, ip, 0, 0)),
                  pl.BlockSpec((None, 2, nc, LANES), lambda b, ip, n: (b, ip, 0, 0))],
        out_specs=[pl.BlockSpec((groups, None, 2 * SSM_GROUP, nc), lambda b, ip, n: (0, b, ip, 0)),
                   pl.BlockSpec((None, None, 2, nc, width), lambda b, ip, n: (seg(n) - 1, b, ip, 0, 0))],
        out_shape=[jax.ShapeDtypeStruct((groups, bsz, CHUNK * SSM_GROUP, nc), BF16),
                   jax.ShapeDtypeStruct((5, bsz, CHUNK, nc, width), BF16)],
        scratch_shapes=[pltpu.VMEM((2 * nc, d), BF16)],
        compiler_params=_params("arbitrary", "arbitrary", "arbitrary"),
        name="in_proj",
    )(x, shift, scale, norm_g, w_in_p, w_ut, cos_t, sin_t)


def _gelu_tanh(y):
    return 0.5 * y * (1.0 + jnp.tanh(math.sqrt(2.0 / math.pi) * (y + 0.044715 * (y * y * y))))


def _s5_kernel(at_ref, toep_ref, bst_ref, cst_ref, av_ref, gt_ref, a_ref, sb_ref, xp_ref, *, bsz, nchunks):
    ns2 = 2 * SSM_STATE
    kk = CHUNK * SSM_GROUP
    for b in range(bsz):
        a_ref[b * nchunks:(b + 1) * nchunks, :] = at_ref[b].T
    a = a_ref[...]
    sb = jnp.dot(a, bst_ref[...], preferred_element_type=F32)
    sb_ref[...] = jnp.swapaxes(sb.reshape(bsz, nchunks, 2 * ns2), 0, 1).reshape(nchunks * bsz, 2 * ns2)
    ar = av_ref[0:1, :]
    ai1 = av_ref[1:2, :]
    ai2 = av_ref[2:3, :]

    def body(c, carry):
        xs, xw = carry
        r = pl.multiple_of(c * bsz, bsz)
        xp_ref[pl.ds(r, bsz), :] = xs
        sb = sb_ref[pl.ds(r, bsz), :]
        return (ar * xs + ai1 * xw + sb[:, :ns2], ar * xw + ai2 * xs + sb[:, ns2:])

    zero = jnp.zeros((bsz, ns2), F32)
    lax.fori_loop(0, nchunks, body, (zero, zero), unroll=8)
    xp = jnp.swapaxes(xp_ref[...].reshape(nchunks, bsz, ns2), 0, 1).reshape(bsz * nchunks, ns2)
    y = jnp.dot(a, toep_ref[...], preferred_element_type=F32)
    y = y + jnp.dot(xp.astype(BF16), cst_ref[...], preferred_element_type=F32)
    gy = _gelu_tanh(y).astype(BF16)
    for b in range(bsz):
        gt_ref[b] = gy[b * nchunks:(b + 1) * nchunks, :].T


def _s5_core(at, toep, bst, cst, avec, layer):
    g, bsz, kk, nchunks = at.shape
    ns2 = 2 * SSM_STATE
    m = bsz * nchunks
    return pl.pallas_call(
        functools.partial(_s5_kernel, bsz=bsz, nchunks=nchunks),
        grid=(g,),
        in_specs=[pl.BlockSpec((None, bsz, kk, nchunks), lambda i: (i, 0, 0, 0)),
                  pl.BlockSpec((None, None, kk, kk), lambda i: (layer, i, 0, 0)),
                  pl.BlockSpec((None, None, kk, 2 * ns2), lambda i: (layer, i, 0, 0)),
                  pl.BlockSpec((None, None, ns2, kk), lambda i: (layer, i, 0, 0)),
                  pl.BlockSpec((None, None, 4, ns2), lambda i: (layer, i, 0, 0))],
        out_specs=pl.BlockSpec((None, bsz, kk, nchunks), lambda i: (i, 0, 0, 0)),
        out_shape=jax.ShapeDtypeStruct((g, bsz, kk, nchunks), BF16),
        scratch_shapes=[pltpu.VMEM((m, kk), BF16), pltpu.VMEM((m, 2 * ns2), F32), pltpu.VMEM((m, ns2), F32)],
        compiler_params=_params("arbitrary"),
        name="s5_core",
    )(at, toep, bst, cst, avec)


def _s5_tables(a_re, a_im, b_re, b_im, c_re, c_im, d_skip, log_step):
    hi = lax.Precision.HIGHEST
    step = jnp.exp(log_step)[..., None]
    lr, li = a_re * step, a_im * step
    tau = jnp.arange(CHUNK + 1, dtype=F32)[:, None]
    mag = jnp.exp(tau * lr[..., None, :])
    pw_re = mag * jnp.cos(tau * li[..., None, :])
    pw_im = mag * jnp.sin(tau * li[..., None, :])
    num_re, num_im = pw_re[..., 1, :] - 1.0, pw_im[..., 1, :]
    den = a_re * a_re + a_im * a_im
    cf_re = (num_re * a_re + num_im * a_im) / den
    cf_im = (num_im * a_re - num_re * a_im) / den
    bb_re = cf_re[..., None] * b_re - cf_im[..., None] * b_im
    bb_im = cf_re[..., None] * b_im + cf_im[..., None] * b_re
    e_re = pw_re[..., None] * bb_re[..., None, :, :] - pw_im[..., None] * bb_im[..., None, :, :]
    e_im = pw_re[..., None] * bb_im[..., None, :, :] + pw_im[..., None] * bb_re[..., None, :, :]
    kern = (jnp.einsum('lgpn,lgtnq->lgtpq', c_re, e_re, precision=hi)
            - jnp.einsum('lgpn,lgtnq->lgtpq', c_im, e_im, precision=hi))
    eye = jnp.eye(SSM_GROUP, dtype=F32)
    kern = kern.at[:, :, 0].add(d_skip[..., :, None] * eye)
    idx = jnp.arange(CHUNK)
    lag = idx[None, :] - idx[:, None]
    blk = jnp.where((lag >= 0)[..., None, None], kern[:, :, jnp.clip(lag, 0)], 0.0)
    dp, g = a_re.shape[:2]
    kk = CHUNK * SSM_GROUP
    toep = blk.transpose(0, 1, 2, 5, 3, 4).reshape(dp, g, kk, kk)
    back = CHUNK - 1 - idx
    bs_re = e_re[:, :, back].transpose(0, 1, 2, 4, 3).reshape(dp, g, kk, SSM_STATE)
    bs_im = e_im[:, :, back].transpose(0, 1, 2, 4, 3).reshape(dp, g, kk, SSM_STATE)
    bst = jnp.concatenate([bs_re, bs_im, bs_im, bs_re], axis=-1)
    fw_re, fw_im = pw_re[:, :, 1:], pw_im[:, :, 1:]
    ca_re = c_re[:, :, None] * fw_re[..., None, :] - c_im[:, :, None] * fw_im[..., None, :]
    ca_im = c_re[:, :, None] * fw_im[..., None, :] + c_im[:, :, None] * fw_re[..., None, :]
    cst = jnp.concatenate([ca_re.transpose(0, 1, 4, 2, 3).reshape(dp, g, SSM_STATE, kk),
                           -ca_im.transpose(0, 1, 4, 2, 3).reshape(dp, g, SSM_STATE, kk)], axis=2)
    ar, ai = pw_re[:, :, CHUNK], pw_im[:, :, CHUNK]
    avec = jnp.stack([jnp.concatenate([ar, ar], -1), jnp.concatenate([-ai, ai], -1),
                      jnp.concatenate([ai, -ai], -1), jnp.zeros_like(jnp.concatenate([ar, ar], -1))], axis=2)
    return toep.astype(BF16), bst.astype(BF16), cst.astype(BF16), avec


def _glu_kernel(gt_ref, zs_ref, w_ref, b_ref, o_ref, *, width, nc):
    gt = gt_ref[...]
    gy = jnp.concatenate([gt[:, t * SSM_GROUP:(t + 1) * SSM_GROUP, :].reshape(width, nc).T for t in range(2)],
                         axis=0)
    ab = jnp.dot(gy, w_ref[...], preferred_element_type=F32) + b_ref[...]
    out = ab[:, :width] * jax.nn.sigmoid(ab[:, width:])
    for t in range(2):
        z = zs_ref[t].astype(F32)
        o_ref[t] = (out[t * nc:(t + 1) * nc, :] * (z * jax.nn.sigmoid(z))).astype(BF16)


def _glu(gt, proj, w_glu_b, b_glu, layer):
    groups, bsz, kk, nc = gt.shape
    width = groups * SSM_GROUP
    return pl.pallas_call(
        functools.partial(_glu_kernel, width=width, nc=nc),
        grid=(bsz, CHUNK // 2),
        in_specs=[pl.BlockSpec((groups, None, 2 * SSM_GROUP, nc), lambda b, ip: (0, b, ip, 0)),
                  pl.BlockSpec((None, None, 2, nc, width), lambda b, ip: (0, b, ip, 0, 0)),
                  pl.BlockSpec((None, width, 2 * width), lambda b, ip: (layer, 0, 0)),
                  pl.BlockSpec((None, 1, 2 * width), lambda b, ip: (layer, 0, 0))],
        out_specs=pl.BlockSpec((None, 2, nc, width), lambda b, ip: (b, ip, 0, 0)),
        out_shape=jax.ShapeDtypeStruct((bsz, CHUNK, nc, width), BF16),
        compiler_params=_params("arbitrary", "arbitrary"),
        name="s5_glu",
    )(gt, proj, w_glu_b, b_glu)


def _attn_kernel(linit_ref, q_ref, k_ref, v_ref, za_ref, lq1_ref, lk1_ref, lq2_ref, lk2_ref, sg_ref,
                 o_ref, km_ref, vt_ref, qt_ref, m_ref, l_ref, acc_ref, s0_ref, s1_ref, p0_ref, p1_ref,
                 a0_ref, a1_ref, c0_ref, c1_ref, *, tq):
    qi = pl.program_id(2)
    nq = tq // TK

    @pl.when(qi == 0)
    def _():
        kk = _natural(k_ref[...])
        grp = (lax.broadcasted_iota(jnp.int32, kk.shape, 1) % LANES) // (HEAD_DIM // 2)
        for combo in range(4):
            km_ref[combo] = jnp.where(grp == combo, kk, jnp.zeros_like(kk))
        vt_ref[...] = _natural(v_ref[...]).astype(F32).T.astype(BF16)

    qt_ref[...] = _natural(q_ref[...]).astype(F32).T.astype(BF16)
    m_ref[...] = jnp.full(m_ref.shape, -jnp.inf, F32)
    l_ref[...] = jnp.zeros(l_ref.shape, F32)
    acc_ref[...] = jnp.zeros(acc_ref.shape, F32)

    def scores(kt, s_ref, cm_ref, c0):
        ks = pl.multiple_of(kt * TK, TK)
        qt = qt_ref[:, c0:]
        for combo in range(4):
            s = jnp.dot(km_ref[combo, pl.ds(ks, TK), :], qt, preferred_element_type=F32)
            s_ref[combo, :, c0:] = s
            cm_ref[combo, :, c0:] = jnp.max(s, axis=0, keepdims=True)

    def softmax(s_ref, cm_ref, p_ref, a_ref, c0, shift):
        for combo in range(4):
            for c in range(c0, tq, LANES):
                cs = slice(c, c + LANES)
                if shift is not None:
                    row = lax.broadcasted_iota(jnp.int32, (TK, LANES), 0)
                    col = lax.broadcasted_iota(jnp.int32, (TK, LANES), 1)
                    s = jnp.where(row + (shift - c) <= col, s_ref[combo, :, cs], -jnp.inf)
                    cmax = jnp.max(s, axis=0, keepdims=True)
                else:
                    s = s_ref[combo, :, cs]
                    cmax = cm_ref[combo, :, cs]
                m_prev = m_ref[combo, :, cs]
                m_new = jnp.maximum(m_prev, cmax)
                alpha = jnp.exp2(m_prev - m_new)
                p = jnp.exp2(s - m_new)
                l_ref[combo, :, cs] = alpha * l_ref[combo, :, cs] + jnp.sum(p, axis=0, keepdims=True)
                p_ref[combo, :, cs] = p.astype(BF16)
                a_ref[combo, :, cs] = alpha
                m_ref[combo, :, cs] = m_new

    def values(kt, p_ref, a_ref, c0):
        ks = pl.multiple_of(kt * TK, TK)
        cs = slice(c0, tq)
        for combo in range(4):
            hp = combo // 2
            pv = jnp.dot(vt_ref[hp * LANES:(hp + 1) * LANES, pl.ds(ks, TK)], p_ref[combo, :, cs],
                         preferred_element_type=F32)
            acc_ref[combo, :, cs] = a_ref[combo, :, cs] * acc_ref[combo, :, cs] + pv

    kd = qi * nq

    def band():
        scores(kd + 1, s1_ref, c1_ref, TK)
        softmax(s0_ref, c0_ref, p0_ref, a0_ref, 0, 0)
        values(kd, p0_ref, a0_ref, 0)
        softmax(s1_ref, c1_ref, p1_ref, a1_ref, TK, TK)
        values(kd + 1, p1_ref, a1_ref, TK)

    @pl.when(qi == 0)
    def _():
        scores(0, s0_ref, c0_ref, 0)
        band()

    @pl.when(qi > 0)
    def _():
        scores(0, s0_ref, c0_ref, 0)
        scores(1, s1_ref, c1_ref, 0)
        softmax(s0_ref, c0_ref, p0_ref, a0_ref, 0, None)

        def pair(i, carry):
            kt = 2 * i
            scores(kt + 2, s0_ref, c0_ref, 0)
            softmax(s1_ref, c1_ref, p1_ref, a1_ref, 0, None)
            values(kt, p0_ref, a0_ref, 0)
            scores(kt + 3, s1_ref, c1_ref, 0)
            softmax(s0_ref, c0_ref, p0_ref, a0_ref, 0, None)
            values(kt + 1, p1_ref, a1_ref, 0)
            return carry

        lax.fori_loop(0, qi - 1, pair, 0)
        softmax(s1_ref, c1_ref, p1_ref, a1_ref, 0, None)
        values(kd - 2, p0_ref, a0_ref, 0)
        values(kd - 1, p1_ref, a1_ref, 0)
        scores(kd, s0_ref, c0_ref, 0)
        band()

    linit = linit_ref[0]
    lam = (jnp.exp(jnp.sum(lq1_ref[...] * lk1_ref[...], axis=-1, keepdims=True))
           - jnp.exp(jnp.sum(lq2_ref[...] * lk2_ref[...], axis=-1, keepdims=True)) + linit)
    za = _natural(za_ref[...]).astype(F32)
    for hp in range(2):
        ot = acc_ref[2 * hp] / l_ref[2 * hp] - lam * (acc_ref[2 * hp + 1] / l_ref[2 * hp + 1])
        o = ot.T
        y = o * lax.rsqrt(jnp.mean(o * o, axis=-1, keepdims=True) + NORM_EPS) * sg_ref[...]
        y = y * (1.0 - linit)
        z = za[:, hp * LANES:(hp + 1) * LANES]
        y = y * (z * jax.nn.sigmoid(z))
        o_ref[:, :, hp * LANES:(hp + 1) * LANES] = jnp.swapaxes(
            y.reshape(tq // CHUNK, CHUNK, LANES), 0, 1).astype(BF16)


def _natural(v):
    return jnp.swapaxes(v, 0, 1).reshape(v.shape[0] * v.shape[1], v.shape[2])


def _attention(proj, lam_q1, lam_k1, lam_q2, lam_k2, sub_g, layer, lambda_init, tq):
    _, bsz, _, nc, width = proj.shape
    s = nc * CHUNK
    tc = tq // CHUNK
    npairs = width // PAIR
    assert tq == 2 * TK
    vec = lambda: pl.BlockSpec((None, 1, HEAD_DIM), lambda b, j, i: (layer, 0, 0))
    sbuf = lambda: pltpu.VMEM((4, TK, tq), F32)
    pbuf = lambda: pltpu.VMEM((4, TK, tq), BF16)
    rowv = lambda: pltpu.VMEM((4, 1, tq), F32)
    return pl.pallas_call(
        functools.partial(_attn_kernel, tq=tq),
        grid=(bsz, npairs, s // tq),
        in_specs=[pl.BlockSpec(memory_space=pltpu.SMEM),
                  pl.BlockSpec((None, None, CHUNK, tc, PAIR), lambda b, j, i: (1, b, 0, i, j)),
                  pl.BlockSpec((None, None, CHUNK, nc, PAIR), lambda b, j, i: (2, b, 0, 0, j)),
                  pl.BlockSpec((None, None, CHUNK, nc, PAIR), lambda b, j, i: (3, b, 0, 0, j)),
                  pl.BlockSpec((None, None, CHUNK, tc, PAIR), lambda b, j, i: (4, b, 0, i, j)),
                  vec(), vec(), vec(), vec(),
                  pl.BlockSpec((None, 1, 2 * HEAD_DIM), lambda b, j, i: (layer, 0, 0))],
        out_specs=pl.BlockSpec((None, CHUNK, tc, PAIR), lambda b, j, i: (b, 0, i, j)),
        out_shape=jax.ShapeDtypeStruct((bsz, CHUNK, nc, width), BF16),
        scratch_shapes=[pltpu.VMEM((4, s, PAIR), BF16), pltpu.VMEM((PAIR, s), BF16), pltpu.VMEM((PAIR, tq), BF16),
                        rowv(), rowv(), pltpu.VMEM((4, LANES, tq), F32),
                        sbuf(), sbuf(), pbuf(), pbuf(), rowv(), rowv(), rowv(), rowv()],
        compiler_params=_params("arbitrary", "arbitrary", "arbitrary"),
        name="diff_attn",
    )(jnp.full((1,), lambda_init, F32), proj, proj, proj, proj, lam_q1, lam_k1, lam_q2, lam_k2, sub_g)


def _out_proj_kernel(ys_ref, ya_ref, ws_ref, wa_ref, x_ref, gate_ref, fg_ref, o_ref, *, final):
    pb, cb, width = ys_ref.shape
    rows = pb * cb
    y = jnp.dot(ys_ref[...].reshape(rows, width), ws_ref[...], preferred_element_type=F32)
    y = y + jnp.dot(ya_ref[...].reshape(rows, width), wa_ref[...], preferred_element_type=F32)
    xn = x_ref[...].reshape(rows, -1) + gate_ref[...] * y
    if final:
        xn = xn * lax.rsqrt(jnp.mean(xn * xn, axis=-1, keepdims=True) + NORM_EPS) * fg_ref[...]
        o_ref[...] = _natural(xn.reshape(pb, cb, -1))
    else:
        o_ref[...] = xn.reshape(pb, cb, -1)


def _out_proj(ys, ya, w_out_b, x, gate, final_g, layer, final):
    bsz, _, nc, d = x.shape
    width = ys.shape[-1]
    if final:
        pb, cb = CHUNK, TOKEN_TILE // CHUNK
        out_spec = pl.BlockSpec((None, TOKEN_TILE, d), lambda b, p, m: (b, m, 0))
        out_shape = jax.ShapeDtypeStruct((bsz, nc * CHUNK, d), F32)
    else:
        pb, cb = TOKEN_TILE // nc, nc
        out_spec = pl.BlockSpec((None, pb, cb, d), lambda b, p, m: (b, p, m, 0))
        out_shape = jax.ShapeDtypeStruct(x.shape, F32)
    tile = lambda last: pl.BlockSpec((None, pb, cb, last), lambda b, p, m: (b, p, m, 0))
    return pl.pallas_call(
        functools.partial(_out_proj_kernel, final=final),
        grid=(bsz, CHUNK // pb, nc // cb),
        in_specs=[tile(width), tile(width),
                  pl.BlockSpec((None, width, d), lambda b, p, m: (layer, 0, 0)),
                  pl.BlockSpec((None, width, d), lambda b, p, m: (layer, 1, 0)),
                  tile(d),
                  pl.BlockSpec((None, 1, d), lambda b, p, m: (b, 0, 0)),
                  pl.BlockSpec((1, d), lambda b, p, m: (0, 0))],
        out_specs=out_spec,
        out_shape=out_shape,
        compiler_params=_params("arbitrary", "arbitrary", "arbitrary"),
        name="out_proj",
    )(ys, ya, w_out_b, w_out_b, x, gate, final_g)


def _to_phase_major_kernel(x_ref, o_ref):
    rows, d = x_ref.shape
    o_ref[...] = jnp.swapaxes(x_ref[...].reshape(rows // CHUNK, CHUNK, d), 0, 1)


def _to_phase_major(x):
    bsz, s, d = x.shape
    tc = TOKEN_TILE // CHUNK
    return pl.pallas_call(
        _to_phase_major_kernel,
        grid=(bsz, s // TOKEN_TILE),
        in_specs=[pl.BlockSpec((None, TOKEN_TILE, d), lambda b, m: (b, m, 0))],
        out_specs=pl.BlockSpec((None, CHUNK, tc, d), lambda b, m: (b, 0, m, 0)),
        out_shape=jax.ShapeDtypeStruct((bsz, CHUNK, s // CHUNK, d), F32),
        compiler_params=_params("arbitrary", "arbitrary"),
        name="to_phase_major",
    )(x)


def _qk_relayout(w):
    half = HEAD_DIM // 2
    lead = w.shape[:-1]
    w = w.reshape(*lead, -1, 2, 2, 2, half)
    nd = len(lead)
    w = w.transpose(*range(nd), nd, nd + 3, nd + 1, nd + 2, nd + 4)
    return w.reshape(*lead, -1)


def _tile(n, target):
    t = min(n, target)
    while n % t or t % LANES:
        t -= LANES
    return t


def kernel(x, c, positions, norm_g, w_ada, b_ada, w_in, w_out, ssm_a_re, ssm_a_im, ssm_b_re, ssm_b_im,
           ssm_c_re, ssm_c_im, ssm_d, ssm_log_step, w_glu, b_glu, lam_q1, lam_k1, lam_q2, lam_k2,
           sub_g, final_g):
    bsz, s, d = x.shape
    depth = w_in.shape[0]
    width = d // 2
    nchunks = s // CHUNK
    assert w_in.shape[-1] == 6 * width and width % PAIR == 0 and bsz % SUBLANES == 0
    assert s % TOKEN_TILE == 0 and TOKEN_TILE % nchunks == 0

    w_in_b = w_in.astype(BF16)
    w_in_p = jnp.concatenate(
        [w_in_b[:, :, :2 * width], _qk_relayout(w_in_b[:, :, 2 * width:3 * width]),
         _qk_relayout(w_in_b[:, :, 3 * width:4 * width]), w_in_b[:, :, 4 * width:]], axis=-1)
    w_ut = jnp.swapaxes(w_in_b[:, :, :width], 1, 2)
    w_out_b = w_out.astype(BF16)
    w_glu_b = w_glu.astype(BF16)
    toep, bst, cst, avec = _s5_tables(ssm_a_re, ssm_a_im, ssm_b_re, ssm_b_im, ssm_c_re, ssm_c_im,
                                      ssm_d, ssm_log_step)
    half = HEAD_DIM // 2
    inv_freq = ROPE_THETA ** (-jnp.arange(half, dtype=F32) / half)
    pos_pm = jnp.swapaxes(positions.reshape(bsz, nchunks, CHUNK), 1, 2)
    ang = pos_pm.astype(F32)[..., None] * inv_freq
    cos_t = jnp.tile(jnp.cos(ang), (1, 1, 1, LANES // half))
    sin_t = jnp.tile(jnp.sin(ang), (1, 1, 1, LANES // half))

    mod = _ada_mod(c, w_ada, b_ada).reshape(depth, bsz, 1, 3 * d)
    norm_g3 = norm_g.reshape(depth, 1, d)
    b_glu3 = b_glu.reshape(depth, 1, 2 * width)
    lam3 = [v.reshape(depth, 1, HEAD_DIM) for v in (lam_q1, lam_k1, lam_q2, lam_k2)]
    sub_g3 = sub_g.reshape(depth, 1, 2 * HEAD_DIM)
    final_g2 = final_g.reshape(1, d)

    x = _to_phase_major(x)
    for l in range(depth):
        lambda_init = 0.8 - 0.6 * math.exp(-0.3 * l)
        shift, scale, gate = mod[l, :, :, :d], mod[l, :, :, d:2 * d], mod[l, :, :, 2 * d:]
        at, proj = _in_proj(x, shift, scale, norm_g3, w_in_p, w_ut, l, cos_t, sin_t)
        gt = _s5_core(at, toep, bst, cst, avec, l)
        ys = _glu(gt, proj, w_glu_b, b_glu3, l)
        ya = _attention(proj, *lam3, sub_g3, l, lambda_init, TOKEN_TILE)
        x = _out_proj(ys, ya, w_out_b, x, gate, final_g2, l, l == depth - 1)
    return x
```

```python
import functools
import math

import jax
import jax.numpy as jnp
import numpy as np
from jax import lax
from jax.experimental import pallas as pl
from jax.experimental.pallas import tpu as pltpu

LANES = 128
SUBLANES = 8
V7X_VMEM_REQUEST_BYTES = 56 * 1024 * 1024

SSM_GROUP = 16
SSM_STATE = 64
HEAD_DIM = 64
ROPE_THETA = 10000.0
NORM_EPS = 1e-6
CHUNK = 16
PAIR = 4 * HEAD_DIM
TK = 256
TOKEN_TILE = 512
ONES_ROWS = 16
Q_SCALE = HEAD_DIM ** -0.5 * math.log2(math.e)

F32 = jnp.float32
BF16 = jnp.bfloat16


def _params(*sem):
    return pltpu.CompilerParams(dimension_semantics=sem, vmem_limit_bytes=V7X_VMEM_REQUEST_BYTES)


def _ada_kernel(c_ref, w_ref, b_ref, o_ref):
    c = c_ref[...]
    act = c * jax.nn.sigmoid(c)
    o_ref[...] = jnp.dot(act, w_ref[...], preferred_element_type=F32,
                         precision=lax.Precision.HIGHEST) + b_ref[...]


def _ada_mod(c, w_ada, b_ada):
    depth, d, n3 = w_ada.shape
    bsz = c.shape[0]
    tn = _tile(n3, 1024)
    return pl.pallas_call(
        _ada_kernel,
        grid=(depth, n3 // tn),
        in_specs=[pl.BlockSpec((bsz, d), lambda l, n: (0, 0)),
                  pl.BlockSpec((None, d, tn), lambda l, n: (l, 0, n)),
                  pl.BlockSpec((None, 1, tn), lambda l, n: (l, 0, n))],
        out_specs=pl.BlockSpec((None, bsz, tn), lambda l, n: (l, 0, n)),
        out_shape=jax.ShapeDtypeStruct((depth, bsz, n3), F32),
        compiler_params=_params("arbitrary", "arbitrary"),
        name="ada_mod",
    )(c, w_ada, b_ada.reshape(depth, 1, n3))


def _in_proj_kernel(x_ref, shift_ref, scale_ref, g_ref, w_ref, wut_ref, cos_ref, sin_ref, at_ref, o_ref, h_ref,
                    *, npairs, nc):
    n = pl.program_id(1)
    ip = pl.program_id(2)
    width = w_ref.shape[-1]

    @pl.when(n == 0)
    def _():
        for t in range(2):
            xf = x_ref[t]
            y = xf * lax.rsqrt(jnp.mean(xf * xf, axis=-1, keepdims=True) + NORM_EPS) * g_ref[...]
            h = (y * (1.0 + scale_ref[...]) + shift_ref[...]).astype(BF16)
            h_ref[ip, t * nc:(t + 1) * nc, :] = h
            ut = lax.dot_general(wut_ref[...], h, (((1,), (1,)), ((), ())), preferred_element_type=F32)
            at_ref[:, t * SSM_GROUP:(t + 1) * SSM_GROUP, :] = ut.reshape(-1, SSM_GROUP, nc).astype(BF16)

    @pl.when(n > 0)
    def _():
        acc = jnp.dot(h_ref[ip], w_ref[...], preferred_element_type=F32)
        is_rope = jnp.logical_or(n == 2, n == 3)

        @pl.when(is_rope)
        def _():
            qs = jnp.where(n == 2, Q_SCALE, 1.0).astype(F32)
            for t in range(2):
                cs = cos_ref[t] * qs
                sn = sin_ref[t] * qs
                for j in range(npairs):
                    lo = j * PAIR
                    t1 = acc[t * nc:(t + 1) * nc, lo:lo + LANES]
                    t2 = acc[t * nc:(t + 1) * nc, lo + LANES:lo + PAIR]
                    o_ref[t, :, lo:lo + LANES] = (t1 * cs - t2 * sn).astype(BF16)
                    o_ref[t, :, lo + LANES:lo + PAIR] = (t2 * cs + t1 * sn).astype(BF16)

        @pl.when(jnp.logical_not(is_rope))
        def _():
            for t in range(2):
                o_ref[t] = acc[t * nc:(t + 1) * nc, :].astype(BF16)


def _in_proj(x, shift, scale, norm_g, w_in_p, w_ut, layer, cos_t, sin_t):
    bsz, _, nc, d = x.shape
    width = w_in_p.shape[-1] // 6
    npairs = width // PAIR
    groups = width // SSM_GROUP
    nip = CHUNK // 2
    seg = lambda n: jnp.maximum(n, 1)
    u_step = lambda n, ip: jnp.where(n == 0, ip, nip - 1)
    o_step = lambda n, ip: jnp.where(n == 0, 0, ip)
    return pl.pallas_call(
        functools.partial(_in_proj_kernel, npairs=npairs, nc=nc),
        grid=(bsz, 6, nip),
        in_specs=[pl.BlockSpec((None, 2, nc, d), lambda b, n, ip: (b, u_step(n, ip), 0, 0)),
                  pl.BlockSpec((None, 1, d), lambda b, n, ip: (b, 0, 0)),
                  pl.BlockSpec((None, 1, d), lambda b, n, ip: (b, 0, 0)),
                  pl.BlockSpec((None, 1, d), lambda b, n, ip: (layer, 0, 0)),
                  pl.BlockSpec((None, d, width), lambda b, n, ip: (layer, 0, seg(n))),
                  pl.BlockSpec((None, width, d), lambda b, n, ip: (layer, 0, 0)),
                  pl.BlockSpec((None, 2, nc, LANES), lambda b, n, ip: (b, ip, 0, 0)),
                  pl.BlockSpec((None, 2, nc, LANES), lambda b, n, ip: (b, ip, 0, 0))],
        out_specs=[pl.BlockSpec((groups, None, 2 * SSM_GROUP, nc), lambda b, n, ip: (0, b, u_step(n, ip), 0)),
                   pl.BlockSpec((None, None, 2, nc, width), lambda b, n, ip: (seg(n) - 1, b, o_step(n, ip), 0, 0))],
        out_shape=[jax.ShapeDtypeStruct((groups, bsz, CHUNK * SSM_GROUP, nc), BF16),
                   jax.ShapeDtypeStruct((5, bsz, CHUNK, nc, width), BF16)],
        scratch_shapes=[pltpu.VMEM((nip, 2 * nc, d), BF16)],
        compiler_params=_params("arbitrary", "arbitrary", "arbitrary"),
        name="in_proj",
    )(x, shift, scale, norm_g, w_in_p, w_ut, cos_t, sin_t)


def _gelu_tanh(y):
    return 0.5 * y * (1.0 + jnp.tanh(math.sqrt(2.0 / math.pi) * (y + 0.044715 * (y * y * y))))


def _s5_kernel(at_ref, toep_ref, bst_ref, cst_ref, av_ref, gt_ref, a_ref, sb_ref, xp_ref, *, bsz, nchunks):
    ns2 = 2 * SSM_STATE
    kk = CHUNK * SSM_GROUP
    for b in range(bsz):
        a_ref[b * nchunks:(b + 1) * nchunks, :] = at_ref[b].T
    a = a_ref[...]
    sb = jnp.dot(a, bst_ref[...], preferred_element_type=F32)
    sb_ref[...] = jnp.swapaxes(sb.reshape(bsz, nchunks, 2 * ns2), 0, 1).reshape(nchunks * bsz, 2 * ns2)
    ar = av_ref[0:1, :]
    ai1 = av_ref[1:2, :]
    ai2 = av_ref[2:3, :]

    def body(c, carry):
        xs, xw = carry
        r = pl.multiple_of(c * bsz, bsz)
        xp_ref[pl.ds(r, bsz), :] = xs
        sb = sb_ref[pl.ds(r, bsz), :]
        return (ar * xs + ai1 * xw + sb[:, :ns2], ar * xw + ai2 * xs + sb[:, ns2:])

    zero = jnp.zeros((bsz, ns2), F32)
    lax.fori_loop(0, nchunks, body, (zero, zero), unroll=8)
    xp = jnp.swapaxes(xp_ref[...].reshape(nchunks, bsz, ns2), 0, 1).reshape(bsz * nchunks, ns2)
    y = jnp.dot(a, toep_ref[...], preferred_element_type=F32)
    y = y + jnp.dot(xp.astype(BF16), cst_ref[...], preferred_element_type=F32)
    gy = _gelu_tanh(y).astype(BF16)
    for b in range(bsz):
        gt_ref[b] = gy[b * nchunks:(b + 1) * nchunks, :].T


def _s5_core(at, toep, bst, cst, avec, layer):
    g, bsz, kk, nchunks = at.shape
    ns2 = 2 * SSM_STATE
    m = bsz * nchunks
    return pl.pallas_call(
        functools.partial(_s5_kernel, bsz=bsz, nchunks=nchunks),
        grid=(g,),
        in_specs=[pl.BlockSpec((None, bsz, kk, nchunks), lambda i: (i, 0, 0, 0)),
                  pl.BlockSpec((None, None, kk, kk), lambda i: (layer, i, 0, 0)),
                  pl.BlockSpec((None, None, kk, 2 * ns2), lambda i: (layer, i, 0, 0)),
                  pl.BlockSpec((None, None, ns2, kk), lambda i: (layer, i, 0, 0)),
                  pl.BlockSpec((None, None, 4, ns2), lambda i: (layer, i, 0, 0))],
        out_specs=pl.BlockSpec((None, bsz, kk, nchunks), lambda i: (i, 0, 0, 0)),
        out_shape=jax.ShapeDtypeStruct((g, bsz, kk, nchunks), BF16),
        scratch_shapes=[pltpu.VMEM((m, kk), BF16), pltpu.VMEM((m, 2 * ns2), F32), pltpu.VMEM((m, ns2), F32)],
        compiler_params=_params("arbitrary"),
        name="s5_core",
    )(at, toep, bst, cst, avec)


def _s5_tables(a_re, a_im, b_re, b_im, c_re, c_im, d_skip, log_step):
    hi = lax.Precision.HIGHEST
    step = jnp.exp(log_step)[..., None]
    lr, li = a_re * step, a_im * step
    tau = jnp.arange(CHUNK + 1, dtype=F32)[:, None]
    mag = jnp.exp(tau * lr[..., None, :])
    pw_re = mag * jnp.cos(tau * li[..., None, :])
    pw_im = mag * jnp.sin(tau * li[..., None, :])
    num_re, num_im = pw_re[..., 1, :] - 1.0, pw_im[..., 1, :]
    den = a_re * a_re + a_im * a_im
    cf_re = (num_re * a_re + num_im * a_im) / den
    cf_im = (num_im * a_re - num_re * a_im) / den
    bb_re = cf_re[..., None] * b_re - cf_im[..., None] * b_im
    bb_im = cf_re[..., None] * b_im + cf_im[..., None] * b_re
    e_re = pw_re[..., None] * bb_re[..., None, :, :] - pw_im[..., None] * bb_im[..., None, :, :]
    e_im = pw_re[..., None] * bb_im[..., None, :, :] + pw_im[..., None] * bb_re[..., None, :, :]
    kern = (jnp.einsum('lgpn,lgtnq->lgtpq', c_re, e_re, precision=hi)
            - jnp.einsum('lgpn,lgtnq->lgtpq', c_im, e_im, precision=hi))
    eye = jnp.eye(SSM_GROUP, dtype=F32)
    kern = kern.at[:, :, 0].add(d_skip[..., :, None] * eye)
    idx = jnp.arange(CHUNK)
    lag = idx[None, :] - idx[:, None]
    blk = jnp.where((lag >= 0)[..., None, None], kern[:, :, jnp.clip(lag, 0)], 0.0)
    dp, g = a_re.shape[:2]
    kk = CHUNK * SSM_GROUP
    toep = blk.transpose(0, 1, 2, 5, 3, 4).reshape(dp, g, kk, kk)
    back = CHUNK - 1 - idx
    bs_re = e_re[:, :, back].transpose(0, 1, 2, 4, 3).reshape(dp, g, kk, SSM_STATE)
    bs_im = e_im[:, :, back].transpose(0, 1, 2, 4, 3).reshape(dp, g, kk, SSM_STATE)
    bst = jnp.concatenate([bs_re, bs_im, bs_im, bs_re], axis=-1)
    fw_re, fw_im = pw_re[:, :, 1:], pw_im[:, :, 1:]
    ca_re = c_re[:, :, None] * fw_re[..., None, :] - c_im[:, :, None] * fw_im[..., None, :]
    ca_im = c_re[:, :, None] * fw_im[..., None, :] + c_im[:, :, None] * fw_re[..., None, :]
    cst = jnp.concatenate([ca_re.transpose(0, 1, 4, 2, 3).reshape(dp, g, SSM_STATE, kk),
                           -ca_im.transpose(0, 1, 4, 2, 3).reshape(dp, g, SSM_STATE, kk)], axis=2)
    ar, ai = pw_re[:, :, CHUNK], pw_im[:, :, CHUNK]
    avec = jnp.stack([jnp.concatenate([ar, ar], -1), jnp.concatenate([-ai, ai], -1),
                      jnp.concatenate([ai, -ai], -1), jnp.zeros_like(jnp.concatenate([ar, ar], -1))], axis=2)
    return toep.astype(BF16), bst.astype(BF16), cst.astype(BF16), avec


def _glu_kernel(gt_ref, zs_ref, w_ref, b_ref, o_ref, *, width, nc):
    gt = gt_ref[...]
    gy = jnp.concatenate([gt[:, t * SSM_GROUP:(t + 1) * SSM_GROUP, :].reshape(width, nc).T for t in range(2)],
                         axis=0)
    ab = jnp.dot(gy, w_ref[...], preferred_element_type=F32) + b_ref[...]
    out = ab[:, :width] * jax.nn.sigmoid(ab[:, width:])
    for t in range(2):
        z = zs_ref[t].astype(F32)
        o_ref[t] = (out[t * nc:(t + 1) * nc, :] * (z * jax.nn.sigmoid(z))).astype(BF16)


def _glu(gt, proj, w_glu_b, b_glu, layer):
    groups, bsz, kk, nc = gt.shape
    width = groups * SSM_GROUP
    return pl.pallas_call(
        functools.partial(_glu_kernel, width=width, nc=nc),
        grid=(bsz, CHUNK // 2),
        in_specs=[pl.BlockSpec((groups, None, 2 * SSM_GROUP, nc), lambda b, ip: (0, b, ip, 0)),
                  pl.BlockSpec((None, None, 2, nc, width), lambda b, ip: (0, b, ip, 0, 0)),
                  pl.BlockSpec((None, width, 2 * width), lambda b, ip: (layer, 0, 0)),
                  pl.BlockSpec((None, 1, 2 * width), lambda b, ip: (layer, 0, 0))],
        out_specs=pl.BlockSpec((None, 2, nc, width), lambda b, ip: (b, ip, 0, 0)),
        out_shape=jax.ShapeDtypeStruct((bsz, CHUNK, nc, width), BF16),
        compiler_params=_params("arbitrary", "arbitrary"),
        name="s5_glu",
    )(gt, proj, w_glu_b, b_glu)


def _attn_kernel(linit_ref, q_ref, k_ref, v_ref, za_ref, lq1_ref, lk1_ref, lq2_ref, lk2_ref, sg_ref,
                 o_ref, km_ref, vt_ref, qt_ref, m_ref, l_ref, acc_ref, s0_ref, s1_ref, p0_ref, p1_ref,
                 a0_ref, a1_ref, c0_ref, c1_ref, *, tq):
    qi = pl.program_id(2)
    nq = tq // TK

    @pl.when(qi == 0)
    def _():
        kk = _natural(k_ref[...])
        grp = (lax.broadcasted_iota(jnp.int32, kk.shape, 1) % LANES) // (HEAD_DIM // 2)
        for combo in range(4):
            km_ref[combo] = jnp.where(grp == combo, kk, jnp.zeros_like(kk))
        vt = _natural(v_ref[...]).astype(F32).T.astype(BF16)
        for hp in range(2):
            vt_ref[hp, :LANES, :] = vt[hp * LANES:(hp + 1) * LANES]
            vt_ref[hp, LANES:, :] = jnp.ones((ONES_ROWS, vt.shape[1]), BF16)

    qt_ref[...] = _natural(q_ref[...]).astype(F32).T.astype(BF16)
    m_ref[...] = jnp.full(m_ref.shape, -jnp.inf, F32)
    l_ref[...] = jnp.zeros(l_ref.shape, F32)
    acc_ref[...] = jnp.zeros(acc_ref.shape, F32)

    def scores(kt, s_ref, cm_ref, c0):
        ks = pl.multiple_of(kt * TK, TK)
        qt = qt_ref[:, c0:]
        for combo in range(4):
            s = jnp.dot(km_ref[combo, pl.ds(ks, TK), :], qt, preferred_element_type=F32)
            s_ref[combo, :, c0:] = s
            cm_ref[combo, :, c0:] = jnp.max(s, axis=0, keepdims=True)

    def softmax(s_ref, cm_ref, p_ref, a_ref, c0, shift):
        for combo in range(4):
            for c in range(c0, tq, LANES):
                cs = slice(c, c + LANES)
                if shift is not None:
                    row = lax.broadcasted_iota(jnp.int32, (TK, LANES), 0)
                    col = lax.broadcasted_iota(jnp.int32, (TK, LANES), 1)
                    s = jnp.where(row + (shift - c) <= col, s_ref[combo, :, cs], -jnp.inf)
                    cmax = jnp.max(s, axis=0, keepdims=True)
                else:
                    s = s_ref[combo, :, cs]
                    cmax = cm_ref[combo, :, cs]
                m_prev = m_ref[combo, :, cs]
                m_new = jnp.maximum(m_prev, cmax)
                alpha = jnp.exp2(m_prev - m_new)
                p = jnp.exp2(s - m_new)
                p_ref[combo, :, cs] = p.astype(BF16)
                a_ref[combo, :, cs] = alpha
                m_ref[combo, :, cs] = m_new

    def values(kt, p_ref, a_ref, c0):
        ks = pl.multiple_of(kt * TK, TK)
        cs = slice(c0, tq)
        for combo in range(4):
            hp = combo // 2
            pv = jnp.dot(vt_ref[hp, :, pl.ds(ks, TK)], p_ref[combo, :, cs], preferred_element_type=F32)
            alpha = a_ref[combo, :, cs]
            acc_ref[combo, :, cs] = alpha * acc_ref[combo, :, cs] + pv[:LANES]
            l_ref[combo, :, cs] = alpha * l_ref[combo, :, cs] + pv[LANES:LANES + 1]

    kd = qi * nq

    def band():
        scores(kd + 1, s1_ref, c1_ref, TK)
        softmax(s0_ref, c0_ref, p0_ref, a0_ref, 0, 0)
        values(kd, p0_ref, a0_ref, 0)
        softmax(s1_ref, c1_ref, p1_ref, a1_ref, TK, TK)
        values(kd + 1, p1_ref, a1_ref, TK)

    @pl.when(qi == 0)
    def _():
        scores(0, s0_ref, c0_ref, 0)
        band()

    @pl.when(qi > 0)
    def _():
        scores(0, s0_ref, c0_ref, 0)
        scores(1, s1_ref, c1_ref, 0)
        softmax(s0_ref, c0_ref, p0_ref, a0_ref, 0, None)

        def pair(i, carry):
            kt = 2 * i
            scores(kt + 2, s0_ref, c0_ref, 0)
            softmax(s1_ref, c1_ref, p1_ref, a1_ref, 0, None)
            values(kt, p0_ref, a0_ref, 0)
            scores(kt + 3, s1_ref, c1_ref, 0)
            softmax(s0_ref, c0_ref, p0_ref, a0_ref, 0, None)
            values(kt + 1, p1_ref, a1_ref, 0)
            return carry

        lax.fori_loop(0, qi - 1, pair, 0)
        softmax(s1_ref, c1_ref, p1_ref, a1_ref, 0, None)
        values(kd - 2, p0_ref, a0_ref, 0)
        values(kd - 1, p1_ref, a1_ref, 0)
        scores(kd, s0_ref, c0_ref, 0)
        band()

    linit = linit_ref[0]
    lam = (jnp.exp(jnp.sum(lq1_ref[...] * lk1_ref[...], axis=-1, keepdims=True))
           - jnp.exp(jnp.sum(lq2_ref[...] * lk2_ref[...], axis=-1, keepdims=True)) + linit)
    za = _natural(za_ref[...]).astype(F32)
    for hp in range(2):
        ot = acc_ref[2 * hp] / l_ref[2 * hp] - lam * (acc_ref[2 * hp + 1] / l_ref[2 * hp + 1])
        o = ot.T
        y = o * lax.rsqrt(jnp.mean(o * o, axis=-1, keepdims=True) + NORM_EPS) * sg_ref[...]
        y = y * (1.0 - linit)
        z = za[:, hp * LANES:(hp + 1) * LANES]
        y = y * (z * jax.nn.sigmoid(z))
        o_ref[:, :, hp * LANES:(hp + 1) * LANES] = jnp.swapaxes(
            y.reshape(tq // CHUNK, CHUNK, LANES), 0, 1).astype(BF16)


def _natural(v):
    return jnp.swapaxes(v, 0, 1).reshape(v.shape[0] * v.shape[1], v.shape[2])


def _attention(proj, lam_q1, lam_k1, lam_q2, lam_k2, sub_g, layer, lambda_init, tq):
    _, bsz, _, nc, width = proj.shape
    s = nc * CHUNK
    tc = tq // CHUNK
    npairs = width // PAIR
    assert tq == 2 * TK
    vec = lambda: pl.BlockSpec((None, 1, HEAD_DIM), lambda b, j, i: (layer, 0, 0))
    sbuf = lambda: pltpu.VMEM((4, TK, tq), F32)
    pbuf = lambda: pltpu.VMEM((4, TK, tq), BF16)
    rowv = lambda: pltpu.VMEM((4, 1, tq), F32)
    return pl.pallas_call(
        functools.partial(_attn_kernel, tq=tq),
        grid=(bsz, npairs, s // tq),
        in_specs=[pl.BlockSpec(memory_space=pltpu.SMEM),
                  pl.BlockSpec((None, None, CHUNK, tc, PAIR), lambda b, j, i: (1, b, 0, i, j)),
                  pl.BlockSpec((None, None, CHUNK, nc, PAIR), lambda b, j, i: (2, b, 0, 0, j)),
                  pl.BlockSpec((None, None, CHUNK, nc, PAIR), lambda b, j, i: (3, b, 0, 0, j)),
                  pl.BlockSpec((None, None, CHUNK, tc, PAIR), lambda b, j, i: (4, b, 0, i, j)),
                  vec(), vec(), vec(), vec(),
                  pl.BlockSpec((None, 1, 2 * HEAD_DIM), lambda b, j, i: (layer, 0, 0))],
        out_specs=pl.BlockSpec((None, CHUNK, tc, PAIR), lambda b, j, i: (b, 0, i, j)),
        out_shape=jax.ShapeDtypeStruct((bsz, CHUNK, nc, width), BF16),
        scratch_shapes=[pltpu.VMEM((4, s, PAIR), BF16), pltpu.VMEM((2, LANES + ONES_ROWS, s), BF16),
                        pltpu.VMEM((PAIR, tq), BF16),
                        rowv(), rowv(), pltpu.VMEM((4, LANES, tq), F32),
                        sbuf(), sbuf(), pbuf(), pbuf(), rowv(), rowv(), rowv(), rowv()],
        compiler_params=_params("arbitrary", "arbitrary", "arbitrary"),
        name="diff_attn",
    )(jnp.full((1,), lambda_init, F32), proj, proj, proj, proj, lam_q1, lam_k1, lam_q2, lam_k2, sub_g)


def _out_proj_kernel(ys_ref, ya_ref, ws_ref, wa_ref, x_ref, gate_ref, fg_ref, o_ref, *, final):
    pb, cb, width = ys_ref.shape
    rows = pb * cb
    y = jnp.dot(ys_ref[...].reshape(rows, width), ws_ref[...], preferred_element_type=F32)
    y = y + jnp.dot(ya_ref[...].reshape(rows, width), wa_ref[...], preferred_element_type=F32)
    xn = x_ref[...].reshape(rows, -1) + gate_ref[...] * y
    if final:
        xn = xn * lax.rsqrt(jnp.mean(xn * xn, axis=-1, keepdims=True) + NORM_EPS) * fg_ref[...]
        o_ref[...] = _natural(xn.reshape(pb, cb, -1))
    else:
        o_ref[...] = xn.reshape(pb, cb, -1)


def _out_proj(ys, ya, w_out_b, x, gate, final_g, layer, final):
    bsz, _, nc, d = x.shape
    width = ys.shape[-1]
    if final:
        pb, cb = CHUNK, TOKEN_TILE // CHUNK
        out_spec = pl.BlockSpec((None, TOKEN_TILE, d), lambda b, p, m: (b, m, 0))
        out_shape = jax.ShapeDtypeStruct((bsz, nc * CHUNK, d), F32)
    else:
        pb, cb = TOKEN_TILE // nc, nc
        out_spec = pl.BlockSpec((None, pb, cb, d), lambda b, p, m: (b, p, m, 0))
        out_shape = jax.ShapeDtypeStruct(x.shape, F32)
    tile = lambda last: pl.BlockSpec((None, pb, cb, last), lambda b, p, m: (b, p, m, 0))
    return pl.pallas_call(
        functools.partial(_out_proj_kernel, final=final),
        grid=(bsz, CHUNK // pb, nc // cb),
        in_specs=[tile(width), tile(width),
                  pl.BlockSpec((None, width, d), lambda b, p, m: (layer, 0, 0)),
                  pl.BlockSpec((None, width, d), lambda b, p, m: (layer, 1, 0)),
                  tile(d),
                  pl.BlockSpec((None, 1, d), lambda b, p, m: (b, 0, 0)),
                  pl.BlockSpec((1, d), lambda b, p, m: (0, 0))],
        out_specs=out_spec,
        out_shape=out_shape,
        compiler_params=_params("arbitrary", "arbitrary", "arbitrary"),
        name="out_proj",
    )(ys, ya, w_out_b, w_out_b, x, gate, final_g)


def _to_phase_major_kernel(x_ref, o_ref):
    rows, d = x_ref.shape
    o_ref[...] = jnp.swapaxes(x_ref[...].reshape(rows // CHUNK, CHUNK, d), 0, 1)


def _to_phase_major(x):
    bsz, s, d = x.shape
    tc = TOKEN_TILE // CHUNK
    return pl.pallas_call(
        _to_phase_major_kernel,
        grid=(bsz, s // TOKEN_TILE),
        in_specs=[pl.BlockSpec((None, TOKEN_TILE, d), lambda b, m: (b, m, 0))],
        out_specs=pl.BlockSpec((None, CHUNK, tc, d), lambda b, m: (b, 0, m, 0)),
        out_shape=jax.ShapeDtypeStruct((bsz, CHUNK, s // CHUNK, d), F32),
        compiler_params=_params("arbitrary", "arbitrary"),
        name="to_phase_major",
    )(x)


def _qk_relayout(w):
    half = HEAD_DIM // 2
    lead = w.shape[:-1]
    w = w.reshape(*lead, -1, 2, 2, 2, half)
    nd = len(lead)
    w = w.transpose(*range(nd), nd, nd + 3, nd + 1, nd + 2, nd + 4)
    return w.reshape(*lead, -1)


def _tile(n, target):
    t = min(n, target)
    while n % t or t % LANES:
        t -= LANES
    return t


def kernel(x, c, positions, norm_g, w_ada, b_ada, w_in, w_out, ssm_a_re, ssm_a_im, ssm_b_re, ssm_b_im,
           ssm_c_re, ssm_c_im, ssm_d, ssm_log_step, w_glu, b_glu, lam_q1, lam_k1, lam_q2, lam_k2,
           sub_g, final_g):
    bsz, s, d = x.shape
    depth = w_in.shape[0]
    width = d // 2
    nchunks = s // CHUNK
    assert w_in.shape[-1] == 6 * width and width % PAIR == 0 and bsz % SUBLANES == 0
    assert s % TOKEN_TILE == 0 and TOKEN_TILE % nchunks == 0

    w_in_b = w_in.astype(BF16)
    w_in_p = jnp.concatenate(
        [w_in_b[:, :, :2 * width], _qk_relayout(w_in_b[:, :, 2 * width:3 * width]),
         _qk_relayout(w_in_b[:, :, 3 * width:4 * width]), w_in_b[:, :, 4 * width:]], axis=-1)
    w_ut = jnp.swapaxes(w_in_b[:, :, :width], 1, 2)
    w_out_b = w_out.astype(BF16)
    w_glu_b = w_glu.astype(BF16)
    toep, bst, cst, avec = _s5_tables(ssm_a_re, ssm_a_im, ssm_b_re, ssm_b_im, ssm_c_re, ssm_c_im,
                                      ssm_d, ssm_log_step)
    half = HEAD_DIM // 2
    inv_freq = ROPE_THETA ** (-jnp.arange(half, dtype=F32) / half)
    pos_pm = jnp.swapaxes(positions.reshape(bsz, nchunks, CHUNK), 1, 2)
    ang = pos_pm.astype(F32)[..., None] * inv_freq
    cos_t = jnp.tile(jnp.cos(ang), (1, 1, 1, LANES // half))
    sin_t = jnp.tile(jnp.sin(ang), (1, 1, 1, LANES // half))

    mod = _ada_mod(c, w_ada, b_ada).reshape(depth, bsz, 1, 3 * d)
    norm_g3 = norm_g.reshape(depth, 1, d)
    b_glu3 = b_glu.reshape(depth, 1, 2 * width)
    lam3 = [v.reshape(depth, 1, HEAD_DIM) for v in (lam_q1, lam_k1, lam_q2, lam_k2)]
    sub_g3 = sub_g.reshape(depth, 1, 2 * HEAD_DIM)
    final_g2 = final_g.reshape(1, d)

    x = _to_phase_major(x)
    for l in range(depth):
        lambda_init = 0.8 - 0.6 * math.exp(-0.3 * l)
        shift, scale, gate = mod[l, :, :, :d], mod[l, :, :, d:2 * d], mod[l, :, :, 2 * d:]
        at, proj = _in_proj(x, shift, scale, norm_g3, w_in_p, w_ut, l, cos_t, sin_t)
        gt = _s5_core(at, toep, bst, cst, avec, l)
        ys = _glu(gt, proj, w_glu_b, b_glu3, l)
        ya = _attention(proj, *lam3, sub_g3, l, lambda_init, TOKEN_TILE)
        x = _out_proj(ys, ya, w_out_b, x, gate, final_g2, l, l == depth - 1)
    return x
```

```python
import functools
import math

import jax
import jax.numpy as jnp
import numpy as np
from jax import lax
from jax.experimental import pallas as pl
from jax.experimental.pallas import tpu as pltpu

LANES = 128
SUBLANES = 8
V7X_VMEM_REQUEST_BYTES = 56 * 1024 * 1024

SSM_GROUP = 16
SSM_STATE = 64
HEAD_DIM = 64
ROPE_THETA = 10000.0
NORM_EPS = 1e-6
CHUNK = 16
PAIR = 4 * HEAD_DIM
TK = 256
TOKEN_TILE = 512
ONES_ROWS = 16
Q_SCALE = HEAD_DIM ** -0.5 * math.log2(math.e)

F32 = jnp.float32
BF16 = jnp.bfloat16


def _params(*sem):
    return pltpu.CompilerParams(dimension_semantics=sem, vmem_limit_bytes=V7X_VMEM_REQUEST_BYTES)


def _ada_kernel(c_ref, w_ref, b_ref, o_ref):
    c = c_ref[...]
    act = c * jax.nn.sigmoid(c)
    o_ref[...] = jnp.dot(act, w_ref[...], preferred_element_type=F32,
                         precision=lax.Precision.HIGHEST) + b_ref[...]


def _ada_mod(c, w_ada, b_ada):
    depth, d, n3 = w_ada.shape
    bsz = c.shape[0]
    tn = _tile(n3, 1024)
    return pl.pallas_call(
        _ada_kernel,
        grid=(depth, n3 // tn),
        in_specs=[pl.BlockSpec((bsz, d), lambda l, n: (0, 0)),
                  pl.BlockSpec((None, d, tn), lambda l, n: (l, 0, n)),
                  pl.BlockSpec((None, 1, tn), lambda l, n: (l, 0, n))],
        out_specs=pl.BlockSpec((None, bsz, tn), lambda l, n: (l, 0, n)),
        out_shape=jax.ShapeDtypeStruct((depth, bsz, n3), F32),
        compiler_params=_params("arbitrary", "arbitrary"),
        name="ada_mod",
    )(c, w_ada, b_ada.reshape(depth, 1, n3))


def _in_proj_kernel(x_ref, shift_ref, scale_ref, g_ref, w_ref, wut_ref, cos_ref, sin_ref, at_ref, o_ref, h_ref,
                    *, npairs, nc):
    n = pl.program_id(1)
    ip = pl.program_id(2)
    width = w_ref.shape[-1]

    @pl.when(n == 0)
    def _():
        for t in range(2):
            xf = x_ref[t]
            y = xf * lax.rsqrt(jnp.mean(xf * xf, axis=-1, keepdims=True) + NORM_EPS) * g_ref[...]
            h = (y * (1.0 + scale_ref[...]) + shift_ref[...]).astype(BF16)
            h_ref[ip, t * nc:(t + 1) * nc, :] = h
            ut = lax.dot_general(wut_ref[...], h, (((1,), (1,)), ((), ())), preferred_element_type=F32)
            at_ref[:, t * SSM_GROUP:(t + 1) * SSM_GROUP, :] = ut.reshape(-1, SSM_GROUP, nc).astype(BF16)

    @pl.when(n > 0)
    def _():
        is_rope = jnp.logical_or(n == 2, n == 3)
        qs = jnp.where(n == 2, Q_SCALE, 1.0).astype(F32)
        cs = [jnp.where(is_rope, cos_ref[t] * qs, 1.0) for t in range(2)]
        sn = [jnp.where(is_rope, sin_ref[t] * qs, 0.0) for t in range(2)]
        h = h_ref[ip]
        for j in range(npairs):
            lo = j * PAIR
            acc = jnp.dot(h, w_ref[:, lo:lo + PAIR], preferred_element_type=F32)
            for t in range(2):
                t1 = acc[t * nc:(t + 1) * nc, :LANES]
                t2 = acc[t * nc:(t + 1) * nc, LANES:]
                o_ref[t, :, lo:lo + LANES] = (t1 * cs[t] - t2 * sn[t]).astype(BF16)
                o_ref[t, :, lo + LANES:lo + PAIR] = (t2 * cs[t] + t1 * sn[t]).astype(BF16)


def _in_proj(x, shift, scale, norm_g, w_in_p, w_ut, layer, cos_t, sin_t):
    bsz, _, nc, d = x.shape
    width = w_in_p.shape[-1] // 6
    npairs = width // PAIR
    groups = width // SSM_GROUP
    nip = CHUNK // 2
    seg = lambda n: jnp.maximum(n, 1)
    u_step = lambda n, ip: jnp.where(n == 0, ip, nip - 1)
    o_step = lambda n, ip: jnp.where(n == 0, 0, ip)
    return pl.pallas_call(
        functools.partial(_in_proj_kernel, npairs=npairs, nc=nc),
        grid=(bsz, 6, nip),
        in_specs=[pl.BlockSpec((None, 2, nc, d), lambda b, n, ip: (b, u_step(n, ip), 0, 0)),
                  pl.BlockSpec((None, 1, d), lambda b, n, ip: (b, 0, 0)),
                  pl.BlockSpec((None, 1, d), lambda b, n, ip: (b, 0, 0)),
                  pl.BlockSpec((None, 1, d), lambda b, n, ip: (layer, 0, 0)),
                  pl.BlockSpec((None, d, width), lambda b, n, ip: (layer, 0, seg(n))),
                  pl.BlockSpec((None, width, d), lambda b, n, ip: (layer, 0, 0)),
                  pl.BlockSpec((None, 2, nc, LANES), lambda b, n, ip: (b, ip, 0, 0)),
                  pl.BlockSpec((None, 2, nc, LANES), lambda b, n, ip: (b, ip, 0, 0))],
        out_specs=[pl.BlockSpec((groups, None, 2 * SSM_GROUP, nc), lambda b, n, ip: (0, b, u_step(n, ip), 0)),
                   pl.BlockSpec((None, None, 2, nc, width), lambda b, n, ip: (seg(n) - 1, b, o_step(n, ip), 0, 0))],
        out_shape=[jax.ShapeDtypeStruct((groups, bsz, CHUNK * SSM_GROUP, nc), BF16),
                   jax.ShapeDtypeStruct((5, bsz, CHUNK, nc, width), BF16)],
        scratch_shapes=[pltpu.VMEM((nip, 2 * nc, d), BF16)],
        compiler_params=_params("arbitrary", "arbitrary", "arbitrary"),
        name="in_proj",
    )(x, shift, scale, norm_g, w_in_p, w_ut, cos_t, sin_t)


def _gelu_tanh(y):
    return 0.5 * y * (1.0 + jnp.tanh(math.sqrt(2.0 / math.pi) * (y + 0.044715 * (y * y * y))))


def _s5_kernel(at_ref, toep_ref, bst_ref, cst_ref, av_ref, gt_ref, a_ref, sb_ref, xp_ref, *, bsz, nchunks):
    ns2 = 2 * SSM_STATE
    kk = CHUNK * SSM_GROUP
    for b in range(bsz):
        a_ref[b * nchunks:(b + 1) * nchunks, :] = at_ref[b].T
    a = a_ref[...]
    sb = jnp.dot(a, bst_ref[...], preferred_element_type=F32)
    sb_ref[...] = jnp.swapaxes(sb.reshape(bsz, nchunks, 2 * ns2), 0, 1).reshape(nchunks * bsz, 2 * ns2)
    ar = av_ref[0:1, :]
    ai1 = av_ref[1:2, :]
    ai2 = av_ref[2:3, :]

    def body(c, carry):
        xs, xw = carry
        r = pl.multiple_of(c * bsz, bsz)
        xp_ref[pl.ds(r, bsz), :] = xs
        sb = sb_ref[pl.ds(r, bsz), :]
        return (ar * xs + ai1 * xw + sb[:, :ns2], ar * xw + ai2 * xs + sb[:, ns2:])

    zero = jnp.zeros((bsz, ns2), F32)
    lax.fori_loop(0, nchunks, body, (zero, zero), unroll=8)
    xp = jnp.swapaxes(xp_ref[...].reshape(nchunks, bsz, ns2), 0, 1).reshape(bsz * nchunks, ns2)
    y = jnp.dot(a, toep_ref[...], preferred_element_type=F32)
    y = y + jnp.dot(xp.astype(BF16), cst_ref[...], preferred_element_type=F32)
    gy = _gelu_tanh(y).astype(BF16)
    for b in range(bsz):
        gt_ref[b] = gy[b * nchunks:(b + 1) * nchunks, :].T


def _s5_core(at, toep, bst, cst, avec, layer):
    g, bsz, kk, nchunks = at.shape
    ns2 = 2 * SSM_STATE
    m = bsz * nchunks
    return pl.pallas_call(
        functools.partial(_s5_kernel, bsz=bsz, nchunks=nchunks),
        grid=(g,),
        in_specs=[pl.BlockSpec((None, bsz, kk, nchunks), lambda i: (i, 0, 0, 0)),
                  pl.BlockSpec((None, None, kk, kk), lambda i: (layer, i, 0, 0)),
                  pl.BlockSpec((None, None, kk, 2 * ns2), lambda i: (layer, i, 0, 0)),
                  pl.BlockSpec((None, None, ns2, kk), lambda i: (layer, i, 0, 0)),
                  pl.BlockSpec((None, None, 4, ns2), lambda i: (layer, i, 0, 0))],
        out_specs=pl.BlockSpec((None, bsz, kk, nchunks), lambda i: (i, 0, 0, 0)),
        out_shape=jax.ShapeDtypeStruct((g, bsz, kk, nchunks), BF16),
        scratch_shapes=[pltpu.VMEM((m, kk), BF16), pltpu.VMEM((m, 2 * ns2), F32), pltpu.VMEM((m, ns2), F32)],
        compiler_params=_params("arbitrary"),
        name="s5_core",
    )(at, toep, bst, cst, avec)


def _s5_tables(a_re, a_im, b_re, b_im, c_re, c_im, d_skip, log_step):
    step = jnp.exp(log_step)[..., None]
    lr, li = a_re * step, a_im * step
    tau = jnp.arange(CHUNK + 1, dtype=F32)[:, None]
    mag = jnp.exp(tau * lr[..., None, :])
    pw_re = mag * jnp.cos(tau * li[..., None, :])
    pw_im = mag * jnp.sin(tau * li[..., None, :])
    num_re, num_im = pw_re[..., 1, :] - 1.0, pw_im[..., 1, :]
    den = a_re * a_re + a_im * a_im
    cf_re = (num_re * a_re + num_im * a_im) / den
    cf_im = (num_im * a_re - num_re * a_im) / den
    bt_re, bt_im = jnp.swapaxes(b_re, -1, -2), jnp.swapaxes(b_im, -1, -2)
    bb_re = cf_re[..., None, :] * bt_re - cf_im[..., None, :] * bt_im
    bb_im = cf_re[..., None, :] * bt_im + cf_im[..., None, :] * bt_re
    e_re = pw_re[..., None, :] * bb_re[..., None, :, :] - pw_im[..., None, :] * bb_im[..., None, :, :]
    e_im = pw_re[..., None, :] * bb_im[..., None, :, :] + pw_im[..., None, :] * bb_re[..., None, :, :]
    kern = jnp.sum(c_re[:, :, None, :, None, :] * e_re[:, :, :, None, :, :]
                   - c_im[:, :, None, :, None, :] * e_im[:, :, :, None, :, :], axis=-1)
    eye = jnp.eye(SSM_GROUP, dtype=F32)
    kern = kern.at[:, :, 0].add(d_skip[..., :, None] * eye)
    idx = jnp.arange(CHUNK)
    lag = idx[None, :] - idx[:, None]
    blk = jnp.where((lag >= 0)[..., None, None], kern[:, :, jnp.clip(lag, 0)], 0.0)
    dp, g = a_re.shape[:2]
    kk = CHUNK * SSM_GROUP
    toep = blk.transpose(0, 1, 2, 5, 3, 4).reshape(dp, g, kk, kk)
    back = CHUNK - 1 - idx
    bs_re = e_re[:, :, back].reshape(dp, g, kk, SSM_STATE)
    bs_im = e_im[:, :, back].reshape(dp, g, kk, SSM_STATE)
    bst = jnp.concatenate([bs_re, bs_im, bs_im, bs_re], axis=-1)
    fw_re, fw_im = pw_re[:, :, 1:], pw_im[:, :, 1:]
    ca_re = c_re[:, :, None] * fw_re[..., None, :] - c_im[:, :, None] * fw_im[..., None, :]
    ca_im = c_re[:, :, None] * fw_im[..., None, :] + c_im[:, :, None] * fw_re[..., None, :]
    cst = jnp.concatenate([ca_re.transpose(0, 1, 4, 2, 3).reshape(dp, g, SSM_STATE, kk),
                           -ca_im.transpose(0, 1, 4, 2, 3).reshape(dp, g, SSM_STATE, kk)], axis=2)
    ar, ai = pw_re[:, :, CHUNK], pw_im[:, :, CHUNK]
    avec = jnp.stack([jnp.concatenate([ar, ar], -1), jnp.concatenate([-ai, ai], -1),
                      jnp.concatenate([ai, -ai], -1), jnp.zeros_like(jnp.concatenate([ar, ar], -1))], axis=2)
    return toep.astype(BF16), bst.astype(BF16), cst.astype(BF16), avec


def _glu_kernel(gt_ref, zs_ref, w_ref, b_ref, o_ref, *, width, nc):
    gt = gt_ref[...]
    gy = jnp.concatenate([gt[:, t * SSM_GROUP:(t + 1) * SSM_GROUP, :].reshape(width, nc).T for t in range(2)],
                         axis=0)
    for lo in range(0, width, PAIR):
        a = jnp.dot(gy, w_ref[:, lo:lo + PAIR], preferred_element_type=F32) + b_ref[:, lo:lo + PAIR]
        g = jnp.dot(gy, w_ref[:, width + lo:width + lo + PAIR], preferred_element_type=F32)
        out = a * jax.nn.sigmoid(g + b_ref[:, width + lo:width + lo + PAIR])
        for t in range(2):
            z = zs_ref[t, :, lo:lo + PAIR].astype(F32)
            o_ref[t, :, lo:lo + PAIR] = (out[t * nc:(t + 1) * nc, :] * (z * jax.nn.sigmoid(z))).astype(BF16)


def _glu(gt, proj, w_glu_b, b_glu, layer):
    groups, bsz, kk, nc = gt.shape
    width = groups * SSM_GROUP
    return pl.pallas_call(
        functools.partial(_glu_kernel, width=width, nc=nc),
        grid=(bsz, CHUNK // 2),
        in_specs=[pl.BlockSpec((groups, None, 2 * SSM_GROUP, nc), lambda b, ip: (0, b, ip, 0)),
                  pl.BlockSpec((None, None, 2, nc, width), lambda b, ip: (0, b, ip, 0, 0)),
                  pl.BlockSpec((None, width, 2 * width), lambda b, ip: (layer, 0, 0)),
                  pl.BlockSpec((None, 1, 2 * width), lambda b, ip: (layer, 0, 0))],
        out_specs=pl.BlockSpec((None, 2, nc, width), lambda b, ip: (b, ip, 0, 0)),
        out_shape=jax.ShapeDtypeStruct((bsz, CHUNK, nc, width), BF16),
        compiler_params=_params("arbitrary", "arbitrary"),
        name="s5_glu",
    )(gt, proj, w_glu_b, b_glu)


def _attn_kernel(linit_ref, q_ref, k_ref, v_ref, za_ref, lq1_ref, lk1_ref, lq2_ref, lk2_ref, sg_ref,
                 o_ref, km_ref, vt_ref, qt_ref, m_ref, l_ref, acc_ref, s0_ref, s1_ref, p0_ref, p1_ref,
                 a0_ref, a1_ref, c0_ref, c1_ref, *, tq):
    qi = pl.program_id(2)
    nq = tq // TK

    @pl.when(qi == 0)
    def _():
        kk = _natural(k_ref[...])
        grp = (lax.broadcasted_iota(jnp.int32, kk.shape, 1) % LANES) // (HEAD_DIM // 2)
        for combo in range(4):
            km_ref[combo] = jnp.where(grp == combo, kk, jnp.zeros_like(kk))
        vt = _natural(v_ref[...]).astype(F32).T.astype(BF16)
        for hp in range(2):
            vt_ref[hp, :LANES, :] = vt[hp * LANES:(hp + 1) * LANES]
            vt_ref[hp, LANES:, :] = jnp.ones((ONES_ROWS, vt.shape[1]), BF16)

    qt_ref[...] = _natural(q_ref[...]).astype(F32).T.astype(BF16)
    m_ref[...] = jnp.full(m_ref.shape, -jnp.inf, F32)
    l_ref[...] = jnp.zeros(l_ref.shape, F32)
    acc_ref[...] = jnp.zeros(acc_ref.shape, F32)

    def scores(kt, s_ref, cm_ref, c0):
        ks = pl.multiple_of(kt * TK, TK)
        qt = qt_ref[:, c0:]
        for combo in range(4):
            s = jnp.dot(km_ref[combo, pl.ds(ks, TK), :], qt, preferred_element_type=F32)
            s_ref[combo, :, c0:] = s
            cm_ref[combo, :, c0:] = jnp.max(s, axis=0, keepdims=True)

    def softmax(s_ref, cm_ref, p_ref, a_ref, c0, shift):
        for combo in range(4):
            for c in range(c0, tq, LANES):
                cs = slice(c, c + LANES)
                if shift is not None:
                    row = lax.broadcasted_iota(jnp.int32, (TK, LANES), 0)
                    col = lax.broadcasted_iota(jnp.int32, (TK, LANES), 1)
                    s = jnp.where(row + (shift - c) <= col, s_ref[combo, :, cs], -jnp.inf)
                    cmax = jnp.max(s, axis=0, keepdims=True)
                else:
                    s = s_ref[combo, :, cs]
                    cmax = cm_ref[combo, :, cs]
                m_prev = m_ref[combo, :, cs]
                m_new = jnp.maximum(m_prev, cmax)
                alpha = jnp.exp2(m_prev - m_new)
                p = jnp.exp2(s - m_new)
                p_ref[combo, :, cs] = p.astype(BF16)
                a_ref[combo, :, cs] = alpha
                m_ref[combo, :, cs] = m_new

    def values(kt, p_ref, a_ref, c0):
        ks = pl.multiple_of(kt * TK, TK)
        cs = slice(c0, tq)
        for combo in range(4):
            hp = combo // 2
            pv = jnp.dot(vt_ref[hp, :, pl.ds(ks, TK)], p_ref[combo, :, cs], preferred_element_type=F32)
            alpha = a_ref[combo, :, cs]
            acc_ref[combo, :, cs] = alpha * acc_ref[combo, :, cs] + pv[:LANES]
            l_ref[combo, :, cs] = alpha * l_ref[combo, :, cs] + pv[LANES:LANES + 1]

    kd = qi * nq

    def band():
        scores(kd + 1, s1_ref, c1_ref, TK)
        softmax(s0_ref, c0_ref, p0_ref, a0_ref, 0, 0)
        values(kd, p0_ref, a0_ref, 0)
        softmax(s1_ref, c1_ref, p1_ref, a1_ref, TK, TK)
        values(kd + 1, p1_ref, a1_ref, TK)

    @pl.when(qi == 0)
    def _():
        scores(0, s0_ref, c0_ref, 0)
        band()

    @pl.when(qi > 0)
    def _():
        scores(0, s0_ref, c0_ref, 0)
        scores(1, s1_ref, c1_ref, 0)
        softmax(s0_ref, c0_ref, p0_ref, a0_ref, 0, None)

        def pair(i, carry):
            kt = 2 * i
            scores(kt + 2, s0_ref, c0_ref, 0)
            softmax(s1_ref, c1_ref, p1_ref, a1_ref, 0, None)
            values(kt, p0_ref, a0_ref, 0)
            scores(kt + 3, s1_ref, c1_ref, 0)
            softmax(s0_ref, c0_ref, p0_ref, a0_ref, 0, None)
            values(kt + 1, p1_ref, a1_ref, 0)
            return carry

        lax.fori_loop(0, qi - 1, pair, 0)
        softmax(s1_ref, c1_ref, p1_ref, a1_ref, 0, None)
        values(kd - 2, p0_ref, a0_ref, 0)
        values(kd - 1, p1_ref, a1_ref, 0)
        scores(kd, s0_ref, c0_ref, 0)
        band()

    linit = linit_ref[0]
    lam = (jnp.exp(jnp.sum(lq1_ref[...] * lk1_ref[...], axis=-1, keepdims=True))
           - jnp.exp(jnp.sum(lq2_ref[...] * lk2_ref[...], axis=-1, keepdims=True)) + linit)
    za = _natural(za_ref[...]).astype(F32)
    for hp in range(2):
        ot = acc_ref[2 * hp] / l_ref[2 * hp] - lam * (acc_ref[2 * hp + 1] / l_ref[2 * hp + 1])
        o = ot.T
        y = o * lax.rsqrt(jnp.mean(o * o, axis=-1, keepdims=True) + NORM_EPS) * sg_ref[...]
        y = y * (1.0 - linit)
        z = za[:, hp * LANES:(hp + 1) * LANES]
        y = y * (z * jax.nn.sigmoid(z))
        o_ref[:, :, hp * LANES:(hp + 1) * LANES] = jnp.swapaxes(
            y.reshape(tq // CHUNK, CHUNK, LANES), 0, 1).astype(BF16)


def _natural(v):
    return jnp.swapaxes(v, 0, 1).reshape(v.shape[0] * v.shape[1], v.shape[2])


def _attention(proj, lam_q1, lam_k1, lam_q2, lam_k2, sub_g, layer, lambda_init, tq):
    _, bsz, _, nc, width = proj.shape
    s = nc * CHUNK
    tc = tq // CHUNK
    npairs = width // PAIR
    assert tq == 2 * TK
    vec = lambda: pl.BlockSpec((None, 1, HEAD_DIM), lambda b, j, i: (layer, 0, 0))
    sbuf = lambda: pltpu.VMEM((4, TK, tq), F32)
    pbuf = lambda: pltpu.VMEM((4, TK, tq), BF16)
    rowv = lambda: pltpu.VMEM((4, 1, tq), F32)
    return pl.pallas_call(
        functools.partial(_attn_kernel, tq=tq),
        grid=(bsz, npairs, s // tq),
        in_specs=[pl.BlockSpec(memory_space=pltpu.SMEM),
                  pl.BlockSpec((None, None, CHUNK, tc, PAIR), lambda b, j, i: (1, b, 0, i, j)),
                  pl.BlockSpec((None, None, CHUNK, nc, PAIR), lambda b, j, i: (2, b, 0, 0, j)),
                  pl.BlockSpec((None, None, CHUNK, nc, PAIR), lambda b, j, i: (3, b, 0, 0, j)),
                  pl.BlockSpec((None, None, CHUNK, tc, PAIR), lambda b, j, i: (4, b, 0, i, j)),
                  vec(), vec(), vec(), vec(),
                  pl.BlockSpec((None, 1, 2 * HEAD_DIM), lambda b, j, i: (layer, 0, 0))],
        out_specs=pl.BlockSpec((None, CHUNK, tc, PAIR), lambda b, j, i: (b, 0, i, j)),
        out_shape=jax.ShapeDtypeStruct((bsz, CHUNK, nc, width), BF16),
        scratch_shapes=[pltpu.VMEM((4, s, PAIR), BF16), pltpu.VMEM((2, LANES + ONES_ROWS, s), BF16),
                        pltpu.VMEM((PAIR, tq), BF16),
                        rowv(), rowv(), pltpu.VMEM((4, LANES, tq), F32),
                        sbuf(), sbuf(), pbuf(), pbuf(), rowv(), rowv(), rowv(), rowv()],
        compiler_params=_params("arbitrary", "arbitrary", "arbitrary"),
        name="diff_attn",
    )(jnp.full((1,), lambda_init, F32), proj, proj, proj, proj, lam_q1, lam_k1, lam_q2, lam_k2, sub_g)


def _out_proj_kernel(ys_ref, ya_ref, ws_ref, wa_ref, x_ref, gate_ref, fg_ref, o_ref, *, final):
    pb, cb, width = ys_ref.shape
    rows = pb * cb
    y = jnp.dot(ys_ref[...].reshape(rows, width), ws_ref[...], preferred_element_type=F32)
    y = y + jnp.dot(ya_ref[...].reshape(rows, width), wa_ref[...], preferred_element_type=F32)
    xn = x_ref[...].reshape(rows, -1) + gate_ref[...] * y
    if final:
        xn = xn * lax.rsqrt(jnp.mean(xn * xn, axis=-1, keepdims=True) + NORM_EPS) * fg_ref[...]
        o_ref[...] = _natural(xn.reshape(pb, cb, -1))
    else:
        o_ref[...] = xn.reshape(pb, cb, -1)


def _out_proj(ys, ya, w_out_b, x, gate, final_g, layer, final):
    bsz, _, nc, d = x.shape
    width = ys.shape[-1]
    if final:
        pb, cb = CHUNK, TOKEN_TILE // CHUNK
        out_spec = pl.BlockSpec((None, TOKEN_TILE, d), lambda b, p, m: (b, m, 0))
        out_shape = jax.ShapeDtypeStruct((bsz, nc * CHUNK, d), F32)
    else:
        pb, cb = TOKEN_TILE // nc, nc
        out_spec = pl.BlockSpec((None, pb, cb, d), lambda b, p, m: (b, p, m, 0))
        out_shape = jax.ShapeDtypeStruct(x.shape, F32)
    tile = lambda last: pl.BlockSpec((None, pb, cb, last), lambda b, p, m: (b, p, m, 0))
    return pl.pallas_call(
        functools.partial(_out_proj_kernel, final=final),
        grid=(bsz, CHUNK // pb, nc // cb),
        in_specs=[tile(width), tile(width),
                  pl.BlockSpec((None, width, d), lambda b, p, m: (layer, 0, 0)),
                  pl.BlockSpec((None, width, d), lambda b, p, m: (layer, 1, 0)),
                  tile(d),
                  pl.BlockSpec((None, 1, d), lambda b, p, m: (b, 0, 0)),
                  pl.BlockSpec((1, d), lambda b, p, m: (0, 0))],
        out_specs=out_spec,
        out_shape=out_shape,
        compiler_params=_params("arbitrary", "arbitrary", "arbitrary"),
        name="out_proj",
    )(ys, ya, w_out_b, w_out_b, x, gate, final_g)


def _to_phase_major_kernel(x_ref, o_ref):
    rows, d = x_ref.shape
    o_ref[...] = jnp.swapaxes(x_ref[...].reshape(rows // CHUNK, CHUNK, d), 0, 1)


def _to_phase_major(x):
    bsz, s, d = x.shape
    tc = TOKEN_TILE // CHUNK
    return pl.pallas_call(
        _to_phase_major_kernel,
        grid=(bsz, s // TOKEN_TILE),
        in_specs=[pl.BlockSpec((None, TOKEN_TILE, d), lambda b, m: (b, m, 0))],
        out_specs=pl.BlockSpec((None, CHUNK, tc, d), lambda b, m: (b, 0, m, 0)),
        out_shape=jax.ShapeDtypeStruct((bsz, CHUNK, s // CHUNK, d), F32),
        compiler_params=_params("arbitrary", "arbitrary"),
        name="to_phase_major",
    )(x)


def _qk_relayout(w):
    half = HEAD_DIM // 2
    lead = w.shape[:-1]
    w = w.reshape(*lead, -1, 2, 2, 2, half)
    nd = len(lead)
    w = w.transpose(*range(nd), nd, nd + 3, nd + 1, nd + 2, nd + 4)
    return w.reshape(*lead, -1)


def _tile(n, target):
    t = min(n, target)
    while n % t or t % LANES:
        t -= LANES
    return t


def kernel(x, c, positions, norm_g, w_ada, b_ada, w_in, w_out, ssm_a_re, ssm_a_im, ssm_b_re, ssm_b_im,
           ssm_c_re, ssm_c_im, ssm_d, ssm_log_step, w_glu, b_glu, lam_q1, lam_k1, lam_q2, lam_k2,
           sub_g, final_g):
    bsz, s, d = x.shape
    depth = w_in.shape[0]
    width = d // 2
    nchunks = s // CHUNK
    assert w_in.shape[-1] == 6 * width and width % PAIR == 0 and bsz % SUBLANES == 0
    assert s % TOKEN_TILE == 0 and TOKEN_TILE % nchunks == 0

    w_in_b = w_in.astype(BF16)
    w_in_p = jnp.concatenate(
        [w_in_b[:, :, :2 * width], _qk_relayout(w_in_b[:, :, 2 * width:3 * width]),
         _qk_relayout(w_in_b[:, :, 3 * width:4 * width]), w_in_b[:, :, 4 * width:]], axis=-1)
    w_ut = jnp.swapaxes(w_in_b[:, :, :width], 1, 2)
    w_out_b = w_out.astype(BF16)
    w_glu_b = w_glu.astype(BF16)
    toep, bst, cst, avec = _s5_tables(ssm_a_re, ssm_a_im, ssm_b_re, ssm_b_im, ssm_c_re, ssm_c_im,
                                      ssm_d, ssm_log_step)
    half = HEAD_DIM // 2
    inv_freq = ROPE_THETA ** (-jnp.arange(half, dtype=F32) / half)
    pos_pm = jnp.swapaxes(positions.reshape(bsz, nchunks, CHUNK), 1, 2)
    ang = pos_pm.astype(F32)[..., None] * inv_freq
    cos_t = jnp.tile(jnp.cos(ang), (1, 1, 1, LANES // half))
    sin_t = jnp.tile(jnp.sin(ang), (1, 1, 1, LANES // half))

    mod = _ada_mod(c, w_ada, b_ada).reshape(depth, bsz, 1, 3 * d)
    norm_g3 = norm_g.reshape(depth, 1, d)
    b_glu3 = b_glu.reshape(depth, 1, 2 * width)
    lam3 = [v.reshape(depth, 1, HEAD_DIM) for v in (lam_q1, lam_k1, lam_q2, lam_k2)]
    sub_g3 = sub_g.reshape(depth, 1, 2 * HEAD_DIM)
    final_g2 = final_g.reshape(1, d)

    x = _to_phase_major(x)
    for l in range(depth):
        lambda_init = 0.8 - 0.6 * math.exp(-0.3 * l)
        shift, scale, gate = mod[l, :, :, :d], mod[l, :, :, d:2 * d], mod[l, :, :, 2 * d:]
        at, proj = _in_proj(x, shift, scale, norm_g3, w_in_p, w_ut, l, cos_t, sin_t)
        gt = _s5_core(at, toep, bst, cst, avec, l)
        ys = _glu(gt, proj, w_glu_b, b_glu3, l)
        ya = _attention(proj, *lam3, sub_g3, l, lambda_init, TOKEN_TILE)
        x = _out_proj(ys, ya, w_out_b, x, gate, final_g2, l, l == depth - 1)
    return x
```

```python
import functools
import math

import jax
import jax.numpy as jnp
import numpy as np
from jax import lax
from jax.experimental import pallas as pl
from jax.experimental.pallas import tpu as pltpu

LANES = 128
SUBLANES = 8
V7X_VMEM_REQUEST_BYTES = 56 * 1024 * 1024

SSM_GROUP = 16
SSM_STATE = 64
HEAD_DIM = 64
ROPE_THETA = 10000.0
NORM_EPS = 1e-6
CHUNK = 16
PAIR = 4 * HEAD_DIM
TK = 256
TOKEN_TILE = 512
ONES_ROWS = 16
Q_SCALE = HEAD_DIM ** -0.5 * math.log2(math.e)

F32 = jnp.float32
BF16 = jnp.bfloat16


def _params(*sem):
    return pltpu.CompilerParams(dimension_semantics=sem, vmem_limit_bytes=V7X_VMEM_REQUEST_BYTES)


def _ada_kernel(c_ref, w_ref, b_ref, o_ref):
    c = c_ref[...]
    act = c * jax.nn.sigmoid(c)
    o_ref[...] = jnp.dot(act, w_ref[...], preferred_element_type=F32,
                         precision=lax.Precision.HIGHEST) + b_ref[...]


def _ada_mod(c, w_ada, b_ada):
    depth, d, n3 = w_ada.shape
    bsz = c.shape[0]
    tn = _tile(n3, 1024)
    return pl.pallas_call(
        _ada_kernel,
        grid=(depth, n3 // tn),
        in_specs=[pl.BlockSpec((bsz, d), lambda l, n: (0, 0)),
                  pl.BlockSpec((None, d, tn), lambda l, n: (l, 0, n)),
                  pl.BlockSpec((None, 1, tn), lambda l, n: (l, 0, n))],
        out_specs=pl.BlockSpec((None, bsz, tn), lambda l, n: (l, 0, n)),
        out_shape=jax.ShapeDtypeStruct((depth, bsz, n3), F32),
        compiler_params=_params("arbitrary", "arbitrary"),
        name="ada_mod",
    )(c, w_ada, b_ada.reshape(depth, 1, n3))


def _in_proj_kernel(x_ref, shift_ref, scale_ref, g_ref, w_ref, wut_ref, cos_ref, sin_ref, at_ref, o_ref, h_ref,
                    *, npairs, nc):
    n = pl.program_id(1)
    ip = pl.program_id(2)
    width = w_ref.shape[-1]

    @pl.when(n == 0)
    def _():
        for t in range(2):
            xf = x_ref[t]
            y = xf * lax.rsqrt(jnp.mean(xf * xf, axis=-1, keepdims=True) + NORM_EPS) * g_ref[...]
            h = (y * (1.0 + scale_ref[...]) + shift_ref[...]).astype(BF16)
            h_ref[ip, t * nc:(t + 1) * nc, :] = h
            ut = lax.dot_general(wut_ref[...], h, (((1,), (1,)), ((), ())), preferred_element_type=F32)
            at_ref[:, t * SSM_GROUP:(t + 1) * SSM_GROUP, :] = ut.reshape(-1, SSM_GROUP, nc).astype(BF16)

    @pl.when(n > 0)
    def _():
        is_rope = jnp.logical_or(n == 2, n == 3)
        qs = jnp.where(n == 2, Q_SCALE, 1.0).astype(F32)
        cs = [jnp.where(is_rope, cos_ref[t] * qs, 1.0) for t in range(2)]
        sn = [jnp.where(is_rope, sin_ref[t] * qs, 0.0) for t in range(2)]
        h = h_ref[ip]
        for j in range(npairs):
            lo = j * PAIR
            acc = jnp.dot(h, w_ref[:, lo:lo + PAIR], preferred_element_type=F32)
            for t in range(2):
                t1 = acc[t * nc:(t + 1) * nc, :LANES]
                t2 = acc[t * nc:(t + 1) * nc, LANES:]
                o_ref[t, :, lo:lo + LANES] = (t1 * cs[t] - t2 * sn[t]).astype(BF16)
                o_ref[t, :, lo + LANES:lo + PAIR] = (t2 * cs[t] + t1 * sn[t]).astype(BF16)


def _in_proj(x, shift, scale, norm_g, w_in_p, w_ut, layer, cos_t, sin_t):
    bsz, _, nc, d = x.shape
    width = w_in_p.shape[-1] // 5
    npairs = width // PAIR
    groups = width // SSM_GROUP
    nip = CHUNK // 2
    seg = lambda n: jnp.maximum(n, 1)
    u_step = lambda n, ip: jnp.where(n == 0, ip, nip - 1)
    o_step = lambda n, ip: jnp.where(n == 0, 0, ip)
    return pl.pallas_call(
        functools.partial(_in_proj_kernel, npairs=npairs, nc=nc),
        grid=(bsz, 6, nip),
        in_specs=[pl.BlockSpec((None, 2, nc, d), lambda b, n, ip: (b, u_step(n, ip), 0, 0)),
                  pl.BlockSpec((None, 1, d), lambda b, n, ip: (b, 0, 0)),
                  pl.BlockSpec((None, 1, d), lambda b, n, ip: (b, 0, 0)),
                  pl.BlockSpec((None, 1, d), lambda b, n, ip: (layer, 0, 0)),
                  pl.BlockSpec((None, d, width), lambda b, n, ip: (layer, 0, seg(n) - 1)),
                  pl.BlockSpec((None, width, d), lambda b, n, ip: (layer, 0, 0)),
                  pl.BlockSpec((None, 2, nc, LANES), lambda b, n, ip: (b, ip, 0, 0)),
                  pl.BlockSpec((None, 2, nc, LANES), lambda b, n, ip: (b, ip, 0, 0))],
        out_specs=[pl.BlockSpec((groups, None, 2 * SSM_GROUP, nc), lambda b, n, ip: (0, b, u_step(n, ip), 0)),
                   pl.BlockSpec((None, None, 2, nc, width), lambda b, n, ip: (seg(n) - 1, b, o_step(n, ip), 0, 0))],
        out_shape=[jax.ShapeDtypeStruct((groups, bsz, CHUNK * SSM_GROUP, nc), BF16),
                   jax.ShapeDtypeStruct((5, bsz, CHUNK, nc, width), BF16)],
        scratch_shapes=[pltpu.VMEM((nip, 2 * nc, d), BF16)],
        compiler_params=_params("arbitrary", "arbitrary", "arbitrary"),
        name="in_proj",
    )(x, shift, scale, norm_g, w_in_p, w_ut, cos_t, sin_t)


def _gelu_tanh(y):
    return 0.5 * y * (1.0 + jnp.tanh(math.sqrt(2.0 / math.pi) * (y + 0.044715 * (y * y * y))))


def _s5_kernel(at_ref, toep_ref, bst_ref, cst_ref, av_ref, gt_ref, a_ref, sb_ref, xp_ref, *, bsz, nchunks):
    ns2 = 2 * SSM_STATE
    kk = CHUNK * SSM_GROUP
    for b in range(bsz):
        a_ref[b * nchunks:(b + 1) * nchunks, :] = at_ref[b].T
    a = a_ref[...]
    sb = jnp.dot(a, bst_ref[...], preferred_element_type=F32)
    sb_ref[...] = jnp.swapaxes(sb.reshape(bsz, nchunks, 2 * ns2), 0, 1).reshape(nchunks * bsz, 2 * ns2)
    ar = av_ref[0:1, :]
    ai1 = av_ref[1:2, :]
    ai2 = av_ref[2:3, :]

    def body(c, carry):
        xs, xw = carry
        r = pl.multiple_of(c * bsz, bsz)
        xp_ref[pl.ds(r, bsz), :] = xs
        sb = sb_ref[pl.ds(r, bsz), :]
        return (ar * xs + ai1 * xw + sb[:, :ns2], ar * xw + ai2 * xs + sb[:, ns2:])

    zero = jnp.zeros((bsz, ns2), F32)
    lax.fori_loop(0, nchunks, body, (zero, zero), unroll=8)
    xp = jnp.swapaxes(xp_ref[...].reshape(nchunks, bsz, ns2), 0, 1).reshape(bsz * nchunks, ns2)
    y = jnp.dot(a, toep_ref[...], preferred_element_type=F32)
    y = y + jnp.dot(xp.astype(BF16), cst_ref[...], preferred_element_type=F32)
    gy = _gelu_tanh(y).astype(BF16)
    for b in range(bsz):
        gt_ref[b] = gy[b * nchunks:(b + 1) * nchunks, :].T


def _s5_core(at, toep, bst, cst, avec, layer):
    g, bsz, kk, nchunks = at.shape
    ns2 = 2 * SSM_STATE
    m = bsz * nchunks
    return pl.pallas_call(
        functools.partial(_s5_kernel, bsz=bsz, nchunks=nchunks),
        grid=(g,),
        in_specs=[pl.BlockSpec((None, bsz, kk, nchunks), lambda i: (i, 0, 0, 0)),
                  pl.BlockSpec((None, None, kk, kk), lambda i: (layer, i, 0, 0)),
                  pl.BlockSpec((None, None, kk, 2 * ns2), lambda i: (layer, i, 0, 0)),
                  pl.BlockSpec((None, None, ns2, kk), lambda i: (layer, i, 0, 0)),
                  pl.BlockSpec((None, None, 4, ns2), lambda i: (layer, i, 0, 0))],
        out_specs=pl.BlockSpec((None, bsz, kk, nchunks), lambda i: (i, 0, 0, 0)),
        out_shape=jax.ShapeDtypeStruct((g, bsz, kk, nchunks), BF16),
        scratch_shapes=[pltpu.VMEM((m, kk), BF16), pltpu.VMEM((m, 2 * ns2), F32), pltpu.VMEM((m, ns2), F32)],
        compiler_params=_params("arbitrary"),
        name="s5_core",
    )(at, toep, bst, cst, avec)


def _s5_tables(a_re, a_im, b_re, b_im, c_re, c_im, d_skip, log_step):
    step = jnp.exp(log_step)[..., None]
    lr, li = a_re * step, a_im * step
    tau = jnp.arange(CHUNK + 1, dtype=F32)[:, None]
    mag = jnp.exp(tau * lr[..., None, :])
    pw_re = mag * jnp.cos(tau * li[..., None, :])
    pw_im = mag * jnp.sin(tau * li[..., None, :])
    num_re, num_im = pw_re[..., 1, :] - 1.0, pw_im[..., 1, :]
    den = a_re * a_re + a_im * a_im
    cf_re = (num_re * a_re + num_im * a_im) / den
    cf_im = (num_im * a_re - num_re * a_im) / den
    bt_re, bt_im = jnp.swapaxes(b_re, -1, -2), jnp.swapaxes(b_im, -1, -2)
    bb_re = cf_re[..., None, :] * bt_re - cf_im[..., None, :] * bt_im
    bb_im = cf_re[..., None, :] * bt_im + cf_im[..., None, :] * bt_re
    e_re = pw_re[..., None, :] * bb_re[..., None, :, :] - pw_im[..., None, :] * bb_im[..., None, :, :]
    e_im = pw_re[..., None, :] * bb_im[..., None, :, :] + pw_im[..., None, :] * bb_re[..., None, :, :]
    kern = jnp.sum(c_re[:, :, None, :, None, :] * e_re[:, :, :, None, :, :]
                   - c_im[:, :, None, :, None, :] * e_im[:, :, :, None, :, :], axis=-1)
    eye = jnp.eye(SSM_GROUP, dtype=F32)
    kern = kern.at[:, :, 0].add(d_skip[..., :, None] * eye)
    idx = jnp.arange(CHUNK)
    lag = idx[None, :] - idx[:, None]
    blk = jnp.where((lag >= 0)[..., None, None], kern[:, :, jnp.clip(lag, 0)], 0.0)
    dp, g = a_re.shape[:2]
    kk = CHUNK * SSM_GROUP
    toep = blk.transpose(0, 1, 2, 5, 3, 4).reshape(dp, g, kk, kk)
    back = CHUNK - 1 - idx
    bs_re = e_re[:, :, back].reshape(dp, g, kk, SSM_STATE)
    bs_im = e_im[:, :, back].reshape(dp, g, kk, SSM_STATE)
    bst = jnp.concatenate([bs_re, bs_im, bs_im, bs_re], axis=-1)
    fw_re, fw_im = pw_re[:, :, 1:], pw_im[:, :, 1:]
    ca_re = c_re[:, :, None] * fw_re[..., None, :] - c_im[:, :, None] * fw_im[..., None, :]
    ca_im = c_re[:, :, None] * fw_im[..., None, :] + c_im[:, :, None] * fw_re[..., None, :]
    cst = jnp.concatenate([ca_re.transpose(0, 1, 4, 2, 3).reshape(dp, g, SSM_STATE, kk),
                           -ca_im.transpose(0, 1, 4, 2, 3).reshape(dp, g, SSM_STATE, kk)], axis=2)
    ar, ai = pw_re[:, :, CHUNK], pw_im[:, :, CHUNK]
    avec = jnp.stack([jnp.concatenate([ar, ar], -1), jnp.concatenate([-ai, ai], -1),
                      jnp.concatenate([ai, -ai], -1), jnp.zeros_like(jnp.concatenate([ar, ar], -1))], axis=2)
    return toep.astype(BF16), bst.astype(BF16), cst.astype(BF16), avec


def _glu_kernel(gt_ref, zs_ref, w_ref, b_ref, o_ref, *, width, nc):
    gt = gt_ref[...]
    gy = jnp.concatenate([gt[:, t * SSM_GROUP:(t + 1) * SSM_GROUP, :].reshape(width, nc).T for t in range(2)],
                         axis=0)
    for lo in range(0, width, PAIR):
        a = jnp.dot(gy, w_ref[:, lo:lo + PAIR], preferred_element_type=F32) + b_ref[:, lo:lo + PAIR]
        g = jnp.dot(gy, w_ref[:, width + lo:width + lo + PAIR], preferred_element_type=F32)
        out = a * jax.nn.sigmoid(g + b_ref[:, width + lo:width + lo + PAIR])
        for t in range(2):
            z = zs_ref[t, :, lo:lo + PAIR].astype(F32)
            o_ref[t, :, lo:lo + PAIR] = (out[t * nc:(t + 1) * nc, :] * (z * jax.nn.sigmoid(z))).astype(BF16)


def _glu(gt, proj, w_glu_b, b_glu, layer):
    groups, bsz, kk, nc = gt.shape
    width = groups * SSM_GROUP
    return pl.pallas_call(
        functools.partial(_glu_kernel, width=width, nc=nc),
        grid=(bsz, CHUNK // 2),
        in_specs=[pl.BlockSpec((groups, None, 2 * SSM_GROUP, nc), lambda b, ip: (0, b, ip, 0)),
                  pl.BlockSpec((None, None, 2, nc, width), lambda b, ip: (0, b, ip, 0, 0)),
                  pl.BlockSpec((None, width, 2 * width), lambda b, ip: (layer, 0, 0)),
                  pl.BlockSpec((None, 1, 2 * width), lambda b, ip: (layer, 0, 0))],
        out_specs=pl.BlockSpec((None, 2, nc, width), lambda b, ip: (b, ip, 0, 0)),
        out_shape=jax.ShapeDtypeStruct((bsz, CHUNK, nc, width), BF16),
        compiler_params=_params("arbitrary", "arbitrary"),
        name="s5_glu",
    )(gt, proj, w_glu_b, b_glu)


def _attn_kernel(linit_ref, q_ref, k_ref, v_ref, za_ref, lq1_ref, lk1_ref, lq2_ref, lk2_ref, sg_ref,
                 o_ref, km_ref, vt_ref, qt_ref, m_ref, l_ref, acc_ref, s0_ref, s1_ref, p0_ref, p1_ref,
                 a0_ref, a1_ref, c0_ref, c1_ref, *, tq):
    qi = pl.program_id(2)
    nq = tq // TK

    @pl.when(qi == 0)
    def _():
        kk = _natural(k_ref[...])
        grp = (lax.broadcasted_iota(jnp.int32, kk.shape, 1) % LANES) // (HEAD_DIM // 2)
        for combo in range(4):
            km_ref[combo] = jnp.where(grp == combo, kk, jnp.zeros_like(kk))
        vt = _natural(v_ref[...]).astype(F32).T.astype(BF16)
        for hp in range(2):
            vt_ref[hp, :LANES, :] = vt[hp * LANES:(hp + 1) * LANES]
            vt_ref[hp, LANES:, :] = jnp.ones((ONES_ROWS, vt.shape[1]), BF16)

    qt_ref[...] = _natural(q_ref[...]).astype(F32).T.astype(BF16)
    m_ref[...] = jnp.full(m_ref.shape, -jnp.inf, F32)
    l_ref[...] = jnp.zeros(l_ref.shape, F32)
    acc_ref[...] = jnp.zeros(acc_ref.shape, F32)

    def scores(kt, s_ref, cm_ref, c0):
        ks = pl.multiple_of(kt * TK, TK)
        qt = qt_ref[:, c0:]
        for combo in range(4):
            s = jnp.dot(km_ref[combo, pl.ds(ks, TK), :], qt, preferred_element_type=F32)
            s_ref[combo, :, c0:] = s
            cm_ref[combo, :, c0:] = jnp.max(s, axis=0, keepdims=True)

    def softmax(s_ref, cm_ref, p_ref, a_ref, c0, shift):
        for combo in range(4):
            for c in range(c0, tq, LANES):
                cs = slice(c, c + LANES)
                if shift is not None:
                    row = lax.broadcasted_iota(jnp.int32, (TK, LANES), 0)
                    col = lax.broadcasted_iota(jnp.int32, (TK, LANES), 1)
                    s = jnp.where(row + (shift - c) <= col, s_ref[combo, :, cs], -jnp.inf)
                    cmax = jnp.max(s, axis=0, keepdims=True)
                else:
                    s = s_ref[combo, :, cs]
                    cmax = cm_ref[combo, :, cs]
                m_prev = m_ref[combo, :, cs]
                m_new = jnp.maximum(m_prev, cmax)
                alpha = jnp.exp2(m_prev - m_new)
                p = jnp.exp2(s - m_new)
                p_ref[combo, :, cs] = p.astype(BF16)
                a_ref[combo, :, cs] = alpha
                m_ref[combo, :, cs] = m_new

    def values(kt, p_ref, a_ref, c0):
        ks = pl.multiple_of(kt * TK, TK)
        cs = slice(c0, tq)
        for combo in range(4):
            hp = combo // 2
            pv = jnp.dot(vt_ref[hp, :, pl.ds(ks, TK)], p_ref[combo, :, cs], preferred_element_type=F32)
            alpha = a_ref[combo, :, cs]
            acc_ref[combo, :, cs] = alpha * acc_ref[combo, :, cs] + pv[:LANES]
            l_ref[combo, :, cs] = alpha * l_ref[combo, :, cs] + pv[LANES:LANES + 1]

    kd = qi * nq

    def band():
        scores(kd + 1, s1_ref, c1_ref, TK)
        softmax(s0_ref, c0_ref, p0_ref, a0_ref, 0, 0)
        values(kd, p0_ref, a0_ref, 0)
        softmax(s1_ref, c1_ref, p1_ref, a1_ref, TK, TK)
        values(kd + 1, p1_ref, a1_ref, TK)

    @pl.when(qi == 0)
    def _():
        scores(0, s0_ref, c0_ref, 0)
        band()

    @pl.when(qi > 0)
    def _():
        scores(0, s0_ref, c0_ref, 0)
        scores(1, s1_ref, c1_ref, 0)
        softmax(s0_ref, c0_ref, p0_ref, a0_ref, 0, None)

        def pair(i, carry):
            kt = 2 * i
            scores(kt + 2, s0_ref, c0_ref, 0)
            softmax(s1_ref, c1_ref, p1_ref, a1_ref, 0, None)
            values(kt, p0_ref, a0_ref, 0)
            scores(kt + 3, s1_ref, c1_ref, 0)
            softmax(s0_ref, c0_ref, p0_ref, a0_ref, 0, None)
            values(kt + 1, p1_ref, a1_ref, 0)
            return carry

        lax.fori_loop(0, qi - 1, pair, 0)
        softmax(s1_ref, c1_ref, p1_ref, a1_ref, 0, None)
        values(kd - 2, p0_ref, a0_ref, 0)
        values(kd - 1, p1_ref, a1_ref, 0)
        scores(kd, s0_ref, c0_ref, 0)
        band()

    linit = linit_ref[0]
    lam = (jnp.exp(jnp.sum(lq1_ref[...] * lk1_ref[...], axis=-1, keepdims=True))
           - jnp.exp(jnp.sum(lq2_ref[...] * lk2_ref[...], axis=-1, keepdims=True)) + linit)
    za = _natural(za_ref[...]).astype(F32)
    for hp in range(2):
        r1 = 1.0 / l_ref[2 * hp]
        r2 = lam / l_ref[2 * hp + 1]
        ot = acc_ref[2 * hp] * r1 - acc_ref[2 * hp + 1] * r2
        o = ot.T
        y = o * lax.rsqrt(jnp.mean(o * o, axis=-1, keepdims=True) + NORM_EPS) * sg_ref[...]
        y = y * (1.0 - linit)
        z = za[:, hp * LANES:(hp + 1) * LANES]
        y = y * (z * jax.nn.sigmoid(z))
        o_ref[:, :, hp * LANES:(hp + 1) * LANES] = jnp.swapaxes(
            y.reshape(tq // CHUNK, CHUNK, LANES), 0, 1).astype(BF16)


def _natural(v):
    return jnp.swapaxes(v, 0, 1).reshape(v.shape[0] * v.shape[1], v.shape[2])


def _attention(proj, lam_q1, lam_k1, lam_q2, lam_k2, sub_g, layer, lambda_init, tq):
    _, bsz, _, nc, width = proj.shape
    s = nc * CHUNK
    tc = tq // CHUNK
    npairs = width // PAIR
    assert tq == 2 * TK
    vec = lambda: pl.BlockSpec((None, 1, HEAD_DIM), lambda b, j, i: (layer, 0, 0))
    sbuf = lambda: pltpu.VMEM((4, TK, tq), F32)
    pbuf = lambda: pltpu.VMEM((4, TK, tq), BF16)
    rowv = lambda: pltpu.VMEM((4, 1, tq), F32)
    return pl.pallas_call(
        functools.partial(_attn_kernel, tq=tq),
        grid=(bsz, npairs, s // tq),
        in_specs=[pl.BlockSpec(memory_space=pltpu.SMEM),
                  pl.BlockSpec((None, None, CHUNK, tc, PAIR), lambda b, j, i: (1, b, 0, i, j)),
                  pl.BlockSpec((None, None, CHUNK, nc, PAIR), lambda b, j, i: (2, b, 0, 0, j)),
                  pl.BlockSpec((None, None, CHUNK, nc, PAIR), lambda b, j, i: (3, b, 0, 0, j)),
                  pl.BlockSpec((None, None, CHUNK, tc, PAIR), lambda b, j, i: (4, b, 0, i, j)),
                  vec(), vec(), vec(), vec(),
                  pl.BlockSpec((None, 1, 2 * HEAD_DIM), lambda b, j, i: (layer, 0, 0))],
        out_specs=pl.BlockSpec((None, CHUNK, tc, PAIR), lambda b, j, i: (b, 0, i, j)),
        out_shape=jax.ShapeDtypeStruct((bsz, CHUNK, nc, width), BF16),
        scratch_shapes=[pltpu.VMEM((4, s, PAIR), BF16), pltpu.VMEM((2, LANES + ONES_ROWS, s), BF16),
                        pltpu.VMEM((PAIR, tq), BF16),
                        rowv(), rowv(), pltpu.VMEM((4, LANES, tq), F32),
                        sbuf(), sbuf(), pbuf(), pbuf(), rowv(), rowv(), rowv(), rowv()],
        compiler_params=_params("arbitrary", "arbitrary", "arbitrary"),
        name="diff_attn",
    )(jnp.full((1,), lambda_init, F32), proj, proj, proj, proj, lam_q1, lam_k1, lam_q2, lam_k2, sub_g)


def _out_proj_kernel(ys_ref, ya_ref, ws_ref, wa_ref, x_ref, gate_ref, fg_ref, o_ref, *, final):
    pb, cb, width = ys_ref.shape
    rows = pb * cb
    y = jnp.dot(ys_ref[...].reshape(rows, width), ws_ref[...], preferred_element_type=F32)
    y = y + jnp.dot(ya_ref[...].reshape(rows, width), wa_ref[...], preferred_element_type=F32)
    xn = x_ref[...].reshape(rows, -1) + gate_ref[...] * y
    if final:
        xn = xn * lax.rsqrt(jnp.mean(xn * xn, axis=-1, keepdims=True) + NORM_EPS) * fg_ref[...]
        o_ref[...] = _natural(xn.reshape(pb, cb, -1))
    else:
        o_ref[...] = xn.reshape(pb, cb, -1)


def _out_proj(ys, ya, w_out_b, x, gate, final_g, layer, final):
    bsz, _, nc, d = x.shape
    width = ys.shape[-1]
    if final:
        pb, cb = CHUNK, TOKEN_TILE // CHUNK
        out_spec = pl.BlockSpec((None, TOKEN_TILE, d), lambda b, p, m: (b, m, 0))
        out_shape = jax.ShapeDtypeStruct((bsz, nc * CHUNK, d), F32)
    else:
        pb, cb = TOKEN_TILE // nc, nc
        out_spec = pl.BlockSpec((None, pb, cb, d), lambda b, p, m: (b, p, m, 0))
        out_shape = jax.ShapeDtypeStruct(x.shape, F32)
    tile = lambda last: pl.BlockSpec((None, pb, cb, last), lambda b, p, m: (b, p, m, 0))
    return pl.pallas_call(
        functools.partial(_out_proj_kernel, final=final),
        grid=(bsz, CHUNK // pb, nc // cb),
        in_specs=[tile(width), tile(width),
                  pl.BlockSpec((None, width, d), lambda b, p, m: (layer, 0, 0)),
                  pl.BlockSpec((None, width, d), lambda b, p, m: (layer, 1, 0)),
                  tile(d),
                  pl.BlockSpec((None, 1, d), lambda b, p, m: (b, 0, 0)),
                  pl.BlockSpec((1, d), lambda b, p, m: (0, 0))],
        out_specs=out_spec,
        out_shape=out_shape,
        compiler_params=_params("arbitrary", "arbitrary", "arbitrary"),
        name="out_proj",
    )(ys, ya, w_out_b, w_out_b, x, gate, final_g)


def _to_phase_major_kernel(x_ref, o_ref):
    rows, d = x_ref.shape
    o_ref[...] = jnp.swapaxes(x_ref[...].reshape(rows // CHUNK, CHUNK, d), 0, 1)


def _to_phase_major(x):
    bsz, s, d = x.shape
    tc = TOKEN_TILE // CHUNK
    return pl.pallas_call(
        _to_phase_major_kernel,
        grid=(bsz, s // TOKEN_TILE),
        in_specs=[pl.BlockSpec((None, TOKEN_TILE, d), lambda b, m: (b, m, 0))],
        out_specs=pl.BlockSpec((None, CHUNK, tc, d), lambda b, m: (b, 0, m, 0)),
        out_shape=jax.ShapeDtypeStruct((bsz, CHUNK, s // CHUNK, d), F32),
        compiler_params=_params("arbitrary", "arbitrary"),
        name="to_phase_major",
    )(x)


def _qk_relayout(w):
    half = HEAD_DIM // 2
    lead = w.shape[:-1]
    w = w.reshape(*lead, -1, 2, 2, 2, half)
    nd = len(lead)
    w = w.transpose(*range(nd), nd, nd + 3, nd + 1, nd + 2, nd + 4)
    return w.reshape(*lead, -1)


def _tile(n, target):
    t = min(n, target)
    while n % t or t % LANES:
        t -= LANES
    return t


def kernel(x, c, positions, norm_g, w_ada, b_ada, w_in, w_out, ssm_a_re, ssm_a_im, ssm_b_re, ssm_b_im,
           ssm_c_re, ssm_c_im, ssm_d, ssm_log_step, w_glu, b_glu, lam_q1, lam_k1, lam_q2, lam_k2,
           sub_g, final_g):
    bsz, s, d = x.shape
    depth = w_in.shape[0]
    width = d // 2
    nchunks = s // CHUNK
    assert w_in.shape[-1] == 6 * width and width % PAIR == 0 and bsz % SUBLANES == 0
    assert s % TOKEN_TILE == 0 and TOKEN_TILE % nchunks == 0

    w_in_p = jnp.concatenate(
        [w_in[:, :, width:2 * width].astype(BF16), _qk_relayout(w_in[:, :, 2 * width:3 * width]).astype(BF16),
         _qk_relayout(w_in[:, :, 3 * width:4 * width]).astype(BF16), w_in[:, :, 4 * width:].astype(BF16)], axis=-1)
    w_ut = jnp.swapaxes(w_in[:, :, :width], 1, 2).astype(BF16)
    w_out_b = w_out.astype(BF16)
    w_glu_b = w_glu.astype(BF16)
    toep, bst, cst, avec = _s5_tables(ssm_a_re, ssm_a_im, ssm_b_re, ssm_b_im, ssm_c_re, ssm_c_im,
                                      ssm_d, ssm_log_step)
    half = HEAD_DIM // 2
    inv_freq = jnp.tile(ROPE_THETA ** (-jnp.arange(half, dtype=F32) / half), LANES // half)
    pos_pm = jnp.swapaxes(positions.reshape(bsz, nchunks, CHUNK), 1, 2)
    ang = pos_pm.astype(F32)[..., None] * inv_freq
    cos_t, sin_t = jnp.cos(ang), jnp.sin(ang)

    mod = _ada_mod(c, w_ada, b_ada).reshape(depth, bsz, 1, 3 * d)
    norm_g3 = norm_g.reshape(depth, 1, d)
    b_glu3 = b_glu.reshape(depth, 1, 2 * width)
    lam3 = [v.reshape(depth, 1, HEAD_DIM) for v in (lam_q1, lam_k1, lam_q2, lam_k2)]
    sub_g3 = sub_g.reshape(depth, 1, 2 * HEAD_DIM)
    final_g2 = final_g.reshape(1, d)

    x = _to_phase_major(x)
    for l in range(depth):
        lambda_init = 0.8 - 0.6 * math.exp(-0.3 * l)
        shift, scale, gate = mod[l, :, :, :d], mod[l, :, :, d:2 * d], mod[l, :, :, 2 * d:]
        at, proj = _in_proj(x, shift, scale, norm_g3, w_in_p, w_ut, l, cos_t, sin_t)
        gt = _s5_core(at, toep, bst, cst, avec, l)
        ys = _glu(gt, proj, w_glu_b, b_glu3, l)
        ya = _attention(proj, *lam3, sub_g3, l, lambda_init, TOKEN_TILE)
        x = _out_proj(ys, ya, w_out_b, x, gate, final_g2, l, l == depth - 1)
    return x
```

```python
import functools
import math

import jax
import jax.numpy as jnp
import numpy as np
from jax import lax
from jax.experimental import pallas as pl
from jax.experimental.pallas import tpu as pltpu

LANES = 128
SUBLANES = 8
V7X_VMEM_REQUEST_BYTES = 56 * 1024 * 1024

SSM_GROUP = 16
SSM_STATE = 64
HEAD_DIM = 64
ROPE_THETA = 10000.0
NORM_EPS = 1e-6
CHUNK = 16
PAIR = 4 * HEAD_DIM
TK = 256
TOKEN_TILE = 512
ONES_ROWS = 16
Q_SCALE = HEAD_DIM ** -0.5 * math.log2(math.e)

F32 = jnp.float32
BF16 = jnp.bfloat16


def _params(*sem):
    return pltpu.CompilerParams(dimension_semantics=sem, vmem_limit_bytes=V7X_VMEM_REQUEST_BYTES)


def _ada_kernel(c_ref, w_ref, b_ref, o_ref):
    c = c_ref[...]
    act = c * jax.nn.sigmoid(c)
    o_ref[...] = jnp.dot(act, w_ref[...], preferred_element_type=F32,
                         precision=lax.Precision.HIGHEST) + b_ref[...]


def _ada_mod(c, w_ada, b_ada):
    depth, d, n3 = w_ada.shape
    bsz = c.shape[0]
    tn = _tile(n3, 1024)
    return pl.pallas_call(
        _ada_kernel,
        grid=(depth, n3 // tn),
        in_specs=[pl.BlockSpec((bsz, d), lambda l, n: (0, 0)),
                  pl.BlockSpec((None, d, tn), lambda l, n: (l, 0, n)),
                  pl.BlockSpec((None, 1, tn), lambda l, n: (l, 0, n))],
        out_specs=pl.BlockSpec((None, bsz, tn), lambda l, n: (l, 0, n)),
        out_shape=jax.ShapeDtypeStruct((depth, bsz, n3), F32),
        compiler_params=_params("arbitrary", "arbitrary"),
        name="ada_mod",
    )(c, w_ada, b_ada.reshape(depth, 1, n3))


def _in_proj_kernel(x_ref, shift_ref, scale_ref, g_ref, w_ref, wut_ref, cos_ref, sin_ref, at_ref, o_ref, h_ref,
                    *, npairs, nc):
    n = pl.program_id(1)
    ip = pl.program_id(2)
    width = w_ref.shape[-1]

    @pl.when(n == 0)
    def _():
        for t in range(2):
            xf = x_ref[t]
            y = xf * lax.rsqrt(jnp.mean(xf * xf, axis=-1, keepdims=True) + NORM_EPS) * g_ref[...]
            h = (y * (1.0 + scale_ref[...]) + shift_ref[...]).astype(BF16)
            h_ref[ip, t * nc:(t + 1) * nc, :] = h
            ut = lax.dot_general(wut_ref[...], h, (((1,), (1,)), ((), ())), preferred_element_type=F32)
            at_ref[:, t * SSM_GROUP:(t + 1) * SSM_GROUP, :] = ut.reshape(-1, SSM_GROUP, nc).astype(BF16)

    @pl.when(n > 0)
    def _():
        is_rope = jnp.logical_or(n == 2, n == 3)
        qs = jnp.where(n == 2, Q_SCALE, 1.0).astype(F32)
        cs = [jnp.where(is_rope, cos_ref[t] * qs, 1.0) for t in range(2)]
        sn = [jnp.where(is_rope, sin_ref[t] * qs, 0.0) for t in range(2)]
        h = h_ref[ip]
        for j in range(npairs):
            lo = j * PAIR
            acc = jnp.dot(h, w_ref[:, lo:lo + PAIR], preferred_element_type=F32)
            for t in range(2):
                t1 = acc[t * nc:(t + 1) * nc, :LANES]
                t2 = acc[t * nc:(t + 1) * nc, LANES:]
                o_ref[t, :, lo:lo + LANES] = (t1 * cs[t] - t2 * sn[t]).astype(BF16)
                o_ref[t, :, lo + LANES:lo + PAIR] = (t2 * cs[t] + t1 * sn[t]).astype(BF16)


def _in_proj(x, shift, scale, norm_g, w_in_p, w_ut, layer, cos_t, sin_t):
    bsz, _, nc, d = x.shape
    width = w_in_p.shape[-1] // 5
    npairs = width // PAIR
    groups = width // SSM_GROUP
    nip = CHUNK // 2
    seg = lambda n: jnp.maximum(n, 1)
    u_step = lambda n, ip: jnp.where(n == 0, ip, nip - 1)
    o_step = lambda n, ip: jnp.where(n == 0, 0, ip)
    return pl.pallas_call(
        functools.partial(_in_proj_kernel, npairs=npairs, nc=nc),
        grid=(bsz, 6, nip),
        in_specs=[pl.BlockSpec((None, 2, nc, d), lambda b, n, ip: (b, u_step(n, ip), 0, 0)),
                  pl.BlockSpec((None, 1, d), lambda b, n, ip: (b, 0, 0)),
                  pl.BlockSpec((None, 1, d), lambda b, n, ip: (b, 0, 0)),
                  pl.BlockSpec((None, 1, d), lambda b, n, ip: (layer, 0, 0)),
                  pl.BlockSpec((None, d, width), lambda b, n, ip: (layer, 0, seg(n) - 1)),
                  pl.BlockSpec((None, width, d), lambda b, n, ip: (layer, 0, 0)),
                  pl.BlockSpec((None, 2, nc, LANES), lambda b, n, ip: (b, ip, 0, 0)),
                  pl.BlockSpec((None, 2, nc, LANES), lambda b, n, ip: (b, ip, 0, 0))],
        out_specs=[pl.BlockSpec((groups, None, 2 * SSM_GROUP, nc), lambda b, n, ip: (0, b, u_step(n, ip), 0)),
                   pl.BlockSpec((None, None, 2, nc, width), lambda b, n, ip: (seg(n) - 1, b, o_step(n, ip), 0, 0))],
        out_shape=[jax.ShapeDtypeStruct((groups, bsz, CHUNK * SSM_GROUP, nc), BF16),
                   jax.ShapeDtypeStruct((5, bsz, CHUNK, nc, width), BF16)],
        scratch_shapes=[pltpu.VMEM((nip, 2 * nc, d), BF16)],
        compiler_params=_params("arbitrary", "arbitrary", "arbitrary"),
        name="in_proj",
    )(x, shift, scale, norm_g, w_in_p, w_ut, cos_t, sin_t)


def _gelu_tanh(y):
    return 0.5 * y * (1.0 + jnp.tanh(math.sqrt(2.0 / math.pi) * (y + 0.044715 * (y * y * y))))


def _s5_kernel(at_ref, toep_ref, bst_ref, cst_ref, av_ref, gt_ref, a_ref, sb_ref, xp_ref, *, bsz, nchunks):
    ns2 = 2 * SSM_STATE
    kk = CHUNK * SSM_GROUP
    for b in range(bsz):
        a_ref[b * nchunks:(b + 1) * nchunks, :] = at_ref[b].T
    a = a_ref[...]
    sb = jnp.dot(a, bst_ref[...], preferred_element_type=F32)
    sb_ref[...] = jnp.swapaxes(sb.reshape(bsz, nchunks, 2 * ns2), 0, 1).reshape(nchunks * bsz, 2 * ns2)
    ar = av_ref[0:1, :]
    ai1 = av_ref[1:2, :]
    ai2 = av_ref[2:3, :]

    def body(c, carry):
        xs, xw = carry
        r = pl.multiple_of(c * bsz, bsz)
        xp_ref[pl.ds(r, bsz), :] = xs
        sb = sb_ref[pl.ds(r, bsz), :]
        return (ar * xs + ai1 * xw + sb[:, :ns2], ar * xw + ai2 * xs + sb[:, ns2:])

    zero = jnp.zeros((bsz, ns2), F32)
    lax.fori_loop(0, nchunks, body, (zero, zero), unroll=8)
    xp = jnp.swapaxes(xp_ref[...].reshape(nchunks, bsz, ns2), 0, 1).reshape(bsz * nchunks, ns2)
    y = jnp.dot(a, toep_ref[...], preferred_element_type=F32)
    y = y + jnp.dot(xp.astype(BF16), cst_ref[...], preferred_element_type=F32)
    gy = _gelu_tanh(y).astype(BF16)
    for b in range(bsz):
        gt_ref[b] = gy[b * nchunks:(b + 1) * nchunks, :].T


def _s5_core(at, toep, bst, cst, avec, layer):
    g, bsz, kk, nchunks = at.shape
    ns2 = 2 * SSM_STATE
    m = bsz * nchunks
    return pl.pallas_call(
        functools.partial(_s5_kernel, bsz=bsz, nchunks=nchunks),
        grid=(g,),
        in_specs=[pl.BlockSpec((None, bsz, kk, nchunks), lambda i: (i, 0, 0, 0)),
                  pl.BlockSpec((None, None, kk, kk), lambda i: (layer, i, 0, 0)),
                  pl.BlockSpec((None, None, kk, 2 * ns2), lambda i: (layer, i, 0, 0)),
                  pl.BlockSpec((None, None, ns2, kk), lambda i: (layer, i, 0, 0)),
                  pl.BlockSpec((None, None, 4, ns2), lambda i: (layer, i, 0, 0))],
        out_specs=pl.BlockSpec((None, bsz, kk, nchunks), lambda i: (i, 0, 0, 0)),
        out_shape=jax.ShapeDtypeStruct((g, bsz, kk, nchunks), BF16),
        scratch_shapes=[pltpu.VMEM((m, kk), BF16), pltpu.VMEM((m, 2 * ns2), F32), pltpu.VMEM((m, ns2), F32)],
        compiler_params=_params("arbitrary"),
        name="s5_core",
    )(at, toep, bst, cst, avec)


def _s5_tables(a_re, a_im, b_re, b_im, c_re, c_im, d_skip, log_step):
    step = jnp.exp(log_step)[..., None]
    lr, li = a_re * step, a_im * step
    tau = jnp.arange(CHUNK + 1, dtype=F32)[:, None]
    mag = jnp.exp(tau * lr[..., None, :])
    pw_re = mag * jnp.cos(tau * li[..., None, :])
    pw_im = mag * jnp.sin(tau * li[..., None, :])
    num_re, num_im = pw_re[..., 1, :] - 1.0, pw_im[..., 1, :]
    den = a_re * a_re + a_im * a_im
    cf_re = (num_re * a_re + num_im * a_im) / den
    cf_im = (num_im * a_re - num_re * a_im) / den
    bt_re, bt_im = jnp.swapaxes(b_re, -1, -2), jnp.swapaxes(b_im, -1, -2)
    bb_re = cf_re[..., None, :] * bt_re - cf_im[..., None, :] * bt_im
    bb_im = cf_re[..., None, :] * bt_im + cf_im[..., None, :] * bt_re
    e_re = pw_re[..., None, :] * bb_re[..., None, :, :] - pw_im[..., None, :] * bb_im[..., None, :, :]
    e_im = pw_re[..., None, :] * bb_im[..., None, :, :] + pw_im[..., None, :] * bb_re[..., None, :, :]
    kern = jnp.sum(c_re[:, :, None, :, None, :] * e_re[:, :, :, None, :, :]
                   - c_im[:, :, None, :, None, :] * e_im[:, :, :, None, :, :], axis=-1)
    eye = jnp.eye(SSM_GROUP, dtype=F32)
    kern = kern.at[:, :, 0].add(d_skip[..., :, None] * eye)
    idx = jnp.arange(CHUNK)
    lag = idx[None, :] - idx[:, None]
    blk = jnp.where((lag >= 0)[..., None, None], kern[:, :, jnp.clip(lag, 0)], 0.0)
    dp, g = a_re.shape[:2]
    kk = CHUNK * SSM_GROUP
    toep = blk.transpose(0, 1, 2, 5, 3, 4).reshape(dp, g, kk, kk)
    back = CHUNK - 1 - idx
    bs_re = e_re[:, :, back].reshape(dp, g, kk, SSM_STATE)
    bs_im = e_im[:, :, back].reshape(dp, g, kk, SSM_STATE)
    bst = jnp.concatenate([bs_re, bs_im, bs_im, bs_re], axis=-1)
    fw_re, fw_im = pw_re[:, :, 1:], pw_im[:, :, 1:]
    ca_re = c_re[:, :, None] * fw_re[..., None, :] - c_im[:, :, None] * fw_im[..., None, :]
    ca_im = c_re[:, :, None] * fw_im[..., None, :] + c_im[:, :, None] * fw_re[..., None, :]
    cst = jnp.concatenate([ca_re.transpose(0, 1, 4, 2, 3).reshape(dp, g, SSM_STATE, kk),
                           -ca_im.transpose(0, 1, 4, 2, 3).reshape(dp, g, SSM_STATE, kk)], axis=2)
    ar, ai = pw_re[:, :, CHUNK], pw_im[:, :, CHUNK]
    avec = jnp.stack([jnp.concatenate([ar, ar], -1), jnp.concatenate([-ai, ai], -1),
                      jnp.concatenate([ai, -ai], -1), jnp.zeros_like(jnp.concatenate([ar, ar], -1))], axis=2)
    return toep.astype(BF16), bst.astype(BF16), cst.astype(BF16), avec


def _glu_kernel(gt_ref, zs_ref, w_ref, b_ref, o_ref, *, width, nc):
    gt = gt_ref[...]
    gy = jnp.concatenate([gt[:, t * SSM_GROUP:(t + 1) * SSM_GROUP, :].reshape(width, nc).T for t in range(2)],
                         axis=0)
    for lo in range(0, width, PAIR):
        a = jnp.dot(gy, w_ref[:, lo:lo + PAIR], preferred_element_type=F32) + b_ref[:, lo:lo + PAIR]
        g = jnp.dot(gy, w_ref[:, width + lo:width + lo + PAIR], preferred_element_type=F32)
        out = a * jax.nn.sigmoid(g + b_ref[:, width + lo:width + lo + PAIR])
        for t in range(2):
            z = zs_ref[t, :, lo:lo + PAIR].astype(F32)
            o_ref[t, :, lo:lo + PAIR] = (out[t * nc:(t + 1) * nc, :] * (z * jax.nn.sigmoid(z))).astype(BF16)


def _glu(gt, proj, w_glu_b, b_glu, layer):
    groups, bsz, kk, nc = gt.shape
    width = groups * SSM_GROUP
    return pl.pallas_call(
        functools.partial(_glu_kernel, width=width, nc=nc),
        grid=(bsz, CHUNK // 2),
        in_specs=[pl.BlockSpec((groups, None, 2 * SSM_GROUP, nc), lambda b, ip: (0, b, ip, 0)),
                  pl.BlockSpec((None, None, 2, nc, width), lambda b, ip: (0, b, ip, 0, 0)),
                  pl.BlockSpec((None, width, 2 * width), lambda b, ip: (layer, 0, 0)),
                  pl.BlockSpec((None, 1, 2 * width), lambda b, ip: (layer, 0, 0))],
        out_specs=pl.BlockSpec((None, 2, nc, width), lambda b, ip: (b, ip, 0, 0)),
        out_shape=jax.ShapeDtypeStruct((bsz, CHUNK, nc, width), BF16),
        compiler_params=_params("arbitrary", "arbitrary"),
        name="s5_glu",
    )(gt, proj, w_glu_b, b_glu)


def _attn_kernel(linit_ref, q_ref, k_ref, v_ref, za_ref, lq1_ref, lk1_ref, lq2_ref, lk2_ref, sg_ref,
                 o_ref, km_ref, vt_ref, qt_ref, m_ref, l_ref, acc_ref, s0_ref, s1_ref, p0_ref, p1_ref,
                 a0_ref, a1_ref, c0_ref, c1_ref, *, tq):
    qi = pl.program_id(2)
    nq = tq // TK

    @pl.when(qi == 0)
    def _():
        kk = _natural(k_ref[...])
        grp = (lax.broadcasted_iota(jnp.int32, kk.shape, 1) % LANES) // (HEAD_DIM // 2)
        for combo in range(4):
            km_ref[combo] = jnp.where(grp == combo, kk, jnp.zeros_like(kk))
        vt = _natural(v_ref[...]).astype(F32).T.astype(BF16)
        for hp in range(2):
            vt_ref[hp, :LANES, :] = vt[hp * LANES:(hp + 1) * LANES]
            vt_ref[hp, LANES:, :] = jnp.ones((ONES_ROWS, vt.shape[1]), BF16)

    qt_ref[...] = _natural(q_ref[...]).astype(F32).T.astype(BF16)
    m_ref[...] = jnp.full(m_ref.shape, -jnp.inf, F32)
    l_ref[...] = jnp.zeros(l_ref.shape, F32)
    acc_ref[...] = jnp.zeros(acc_ref.shape, F32)

    def scores(kt, s_ref, cm_ref, c0):
        ks = pl.multiple_of(kt * TK, TK)
        qt = qt_ref[:, c0:]
        for combo in range(4):
            s = jnp.dot(km_ref[combo, pl.ds(ks, TK), :], qt, preferred_element_type=F32)
            for c in range(c0, tq, LANES):
                s_ref[combo, c // LANES] = s[:, c - c0:c - c0 + LANES]
            cm_ref[combo, :, c0:] = jnp.max(s, axis=0, keepdims=True)

    def softmax(s_ref, cm_ref, p_ref, a_ref, c0, shift):
        for combo in range(4):
            for c in range(c0, tq, LANES):
                cs = slice(c, c + LANES)
                if shift is not None:
                    row = lax.broadcasted_iota(jnp.int32, (TK, LANES), 0)
                    col = lax.broadcasted_iota(jnp.int32, (TK, LANES), 1)
                    s = jnp.where(row + (shift - c) <= col, s_ref[combo, c // LANES], -jnp.inf)
                    cmax = jnp.max(s, axis=0, keepdims=True)
                else:
                    s = s_ref[combo, c // LANES]
                    cmax = cm_ref[combo, :, cs]
                m_prev = m_ref[combo, :, cs]
                m_new = jnp.maximum(m_prev, cmax)
                alpha = jnp.exp2(m_prev - m_new)
                p = jnp.exp2(s - m_new)
                p_ref[combo, c // LANES] = p.astype(BF16)
                a_ref[combo, :, cs] = alpha
                m_ref[combo, :, cs] = m_new

    def values(kt, p_ref, a_ref, c0):
        ks = pl.multiple_of(kt * TK, TK)
        cs = slice(c0, tq)
        for combo in range(4):
            hp = combo // 2
            p = jnp.concatenate([p_ref[combo, c // LANES] for c in range(c0, tq, LANES)], axis=1)
            pv = jnp.dot(vt_ref[hp, :, pl.ds(ks, TK)], p, preferred_element_type=F32)
            alpha = a_ref[combo, :, cs]
            acc_ref[combo, :, cs] = alpha * acc_ref[combo, :, cs] + pv[:LANES]
            l_ref[combo, :, cs] = alpha * l_ref[combo, :, cs] + pv[LANES:LANES + 1]

    kd = qi * nq

    def band():
        scores(kd + 1, s1_ref, c1_ref, TK)
        softmax(s0_ref, c0_ref, p0_ref, a0_ref, 0, 0)
        values(kd, p0_ref, a0_ref, 0)
        softmax(s1_ref, c1_ref, p1_ref, a1_ref, TK, TK)
        values(kd + 1, p1_ref, a1_ref, TK)

    @pl.when(qi == 0)
    def _():
        scores(0, s0_ref, c0_ref, 0)
        band()

    @pl.when(qi > 0)
    def _():
        scores(0, s0_ref, c0_ref, 0)
        scores(1, s1_ref, c1_ref, 0)
        softmax(s0_ref, c0_ref, p0_ref, a0_ref, 0, None)

        def pair(i, carry):
            kt = 2 * i
            scores(kt + 2, s0_ref, c0_ref, 0)
            softmax(s1_ref, c1_ref, p1_ref, a1_ref, 0, None)
            values(kt, p0_ref, a0_ref, 0)
            scores(kt + 3, s1_ref, c1_ref, 0)
            softmax(s0_ref, c0_ref, p0_ref, a0_ref, 0, None)
            values(kt + 1, p1_ref, a1_ref, 0)
            return carry

        lax.fori_loop(0, qi - 1, pair, 0)
        softmax(s1_ref, c1_ref, p1_ref, a1_ref, 0, None)
        values(kd - 2, p0_ref, a0_ref, 0)
        values(kd - 1, p1_ref, a1_ref, 0)
        scores(kd, s0_ref, c0_ref, 0)
        band()

    linit = linit_ref[0]
    lam = (jnp.exp(jnp.sum(lq1_ref[...] * lk1_ref[...], axis=-1, keepdims=True))
           - jnp.exp(jnp.sum(lq2_ref[...] * lk2_ref[...], axis=-1, keepdims=True)) + linit)
    za = _natural(za_ref[...]).astype(F32)
    for hp in range(2):
        r1 = 1.0 / l_ref[2 * hp]
        r2 = lam / l_ref[2 * hp + 1]
        ot = acc_ref[2 * hp] * r1 - acc_ref[2 * hp + 1] * r2
        o = ot.T
        y = o * lax.rsqrt(jnp.mean(o * o, axis=-1, keepdims=True) + NORM_EPS) * sg_ref[...]
        y = y * (1.0 - linit)
        z = za[:, hp * LANES:(hp + 1) * LANES]
        y = y * (z * jax.nn.sigmoid(z))
        o_ref[:, :, hp * LANES:(hp + 1) * LANES] = jnp.swapaxes(
            y.reshape(tq // CHUNK, CHUNK, LANES), 0, 1).astype(BF16)


def _natural(v):
    return jnp.swapaxes(v, 0, 1).reshape(v.shape[0] * v.shape[1], v.shape[2])


def _attention(proj, lam_q1, lam_k1, lam_q2, lam_k2, sub_g, layer, lambda_init, tq):
    _, bsz, _, nc, width = proj.shape
    s = nc * CHUNK
    tc = tq // CHUNK
    npairs = width // PAIR
    assert tq == 2 * TK
    vec = lambda: pl.BlockSpec((None, 1, HEAD_DIM), lambda b, j, i: (layer, 0, 0))
    sbuf = lambda: pltpu.VMEM((4, tq // LANES, TK, LANES), F32)
    pbuf = lambda: pltpu.VMEM((4, tq // LANES, TK, LANES), BF16)
    rowv = lambda: pltpu.VMEM((4, 1, tq), F32)
    return pl.pallas_call(
        functools.partial(_attn_kernel, tq=tq),
        grid=(bsz, npairs, s // tq),
        in_specs=[pl.BlockSpec(memory_space=pltpu.SMEM),
                  pl.BlockSpec((None, None, CHUNK, tc, PAIR), lambda b, j, i: (1, b, 0, i, j)),
                  pl.BlockSpec((None, None, CHUNK, nc, PAIR), lambda b, j, i: (2, b, 0, 0, j)),
                  pl.BlockSpec((None, None, CHUNK, nc, PAIR), lambda b, j, i: (3, b, 0, 0, j)),
                  pl.BlockSpec((None, None, CHUNK, tc, PAIR), lambda b, j, i: (4, b, 0, i, j)),
                  vec(), vec(), vec(), vec(),
                  pl.BlockSpec((None, 1, 2 * HEAD_DIM), lambda b, j, i: (layer, 0, 0))],
        out_specs=pl.BlockSpec((None, CHUNK, tc, PAIR), lambda b, j, i: (b, 0, i, j)),
        out_shape=jax.ShapeDtypeStruct((bsz, CHUNK, nc, width), BF16),
        scratch_shapes=[pltpu.VMEM((4, s, PAIR), BF16), pltpu.VMEM((2, LANES + ONES_ROWS, s), BF16),
                        pltpu.VMEM((PAIR, tq), BF16),
                        rowv(), rowv(), pltpu.VMEM((4, LANES, tq), F32),
                        sbuf(), sbuf(), pbuf(), pbuf(), rowv(), rowv(), rowv(), rowv()],
        compiler_params=_params("arbitrary", "arbitrary", "arbitrary"),
        name="diff_attn",
    )(jnp.full((1,), lambda_init, F32), proj, proj, proj, proj, lam_q1, lam_k1, lam_q2, lam_k2, sub_g)


def _out_proj_kernel(ys_ref, ya_ref, ws_ref, wa_ref, x_ref, gate_ref, fg_ref, o_ref, *, final):
    pb, cb, width = ys_ref.shape
    rows = pb * cb
    y = jnp.dot(ys_ref[...].reshape(rows, width), ws_ref[...], preferred_element_type=F32)
    y = y + jnp.dot(ya_ref[...].reshape(rows, width), wa_ref[...], preferred_element_type=F32)
    xn = x_ref[...].reshape(rows, -1) + gate_ref[...] * y
    if final:
        xn = xn * lax.rsqrt(jnp.mean(xn * xn, axis=-1, keepdims=True) + NORM_EPS) * fg_ref[...]
        o_ref[...] = _natural(xn.reshape(pb, cb, -1))
    else:
        o_ref[...] = xn.reshape(pb, cb, -1)


def _out_proj(ys, ya, w_out_b, x, gate, final_g, layer, final):
    bsz, _, nc, d = x.shape
    width = ys.shape[-1]
    if final:
        pb, cb = CHUNK, TOKEN_TILE // CHUNK
        out_spec = pl.BlockSpec((None, TOKEN_TILE, d), lambda b, p, m: (b, m, 0))
        out_shape = jax.ShapeDtypeStruct((bsz, nc * CHUNK, d), F32)
    else:
        pb, cb = TOKEN_TILE // nc, nc
        out_spec = pl.BlockSpec((None, pb, cb, d), lambda b, p, m: (b, p, m, 0))
        out_shape = jax.ShapeDtypeStruct(x.shape, F32)
    tile = lambda last: pl.BlockSpec((None, pb, cb, last), lambda b, p, m: (b, p, m, 0))
    return pl.pallas_call(
        functools.partial(_out_proj_kernel, final=final),
        grid=(bsz, CHUNK // pb, nc // cb),
        in_specs=[tile(width), tile(width),
                  pl.BlockSpec((None, width, d), lambda b, p, m: (layer, 0, 0)),
                  pl.BlockSpec((None, width, d), lambda b, p, m: (layer, 1, 0)),
                  tile(d),
                  pl.BlockSpec((None, 1, d), lambda b, p, m: (b, 0, 0)),
                  pl.BlockSpec((1, d), lambda b, p, m: (0, 0))],
        out_specs=out_spec,
        out_shape=out_shape,
        compiler_params=_params("arbitrary", "arbitrary", "arbitrary"),
        name="out_proj",
    )(ys, ya, w_out_b, w_out_b, x, gate, final_g)


def _to_phase_major_kernel(x_ref, o_ref):
    rows, d = x_ref.shape
    o_ref[...] = jnp.swapaxes(x_ref[...].reshape(rows // CHUNK, CHUNK, d), 0, 1)


def _to_phase_major(x):
    bsz, s, d = x.shape
    tc = TOKEN_TILE // CHUNK
    return pl.pallas_call(
        _to_phase_major_kernel,
        grid=(bsz, s // TOKEN_TILE),
        in_specs=[pl.BlockSpec((None, TOKEN_TILE, d), lambda b, m: (b, m, 0))],
        out_specs=pl.BlockSpec((None, CHUNK, tc, d), lambda b, m: (b, 0, m, 0)),
        out_shape=jax.ShapeDtypeStruct((bsz, CHUNK, s // CHUNK, d), F32),
        compiler_params=_params("arbitrary", "arbitrary"),
        name="to_phase_major",
    )(x)


def _qk_relayout(w):
    half = HEAD_DIM // 2
    lead = w.shape[:-1]
    w = w.reshape(*lead, -1, 2, 2, 2, half)
    nd = len(lead)
    w = w.transpose(*range(nd), nd, nd + 3, nd + 1, nd + 2, nd + 4)
    return w.reshape(*lead, -1)


def _tile(n, target):
    t = min(n, target)
    while n % t or t % LANES:
        t -= LANES
    return t


def kernel(x, c, positions, norm_g, w_ada, b_ada, w_in, w_out, ssm_a_re, ssm_a_im, ssm_b_re, ssm_b_im,
           ssm_c_re, ssm_c_im, ssm_d, ssm_log_step, w_glu, b_glu, lam_q1, lam_k1, lam_q2, lam_k2,
           sub_g, final_g):
    bsz, s, d = x.shape
    depth = w_in.shape[0]
    width = d // 2
    nchunks = s // CHUNK
    assert w_in.shape[-1] == 6 * width and width % PAIR == 0 and bsz % SUBLANES == 0
    assert s % TOKEN_TILE == 0 and TOKEN_TILE % nchunks == 0

    w_in_p = jnp.concatenate(
        [w_in[:, :, width:2 * width].astype(BF16), _qk_relayout(w_in[:, :, 2 * width:3 * width]).astype(BF16),
         _qk_relayout(w_in[:, :, 3 * width:4 * width]).astype(BF16), w_in[:, :, 4 * width:].astype(BF16)], axis=-1)
    w_ut = jnp.swapaxes(lax.optimization_barrier(w_in[:, :, :width]), 1, 2).astype(BF16)
    w_out_b = w_out.astype(BF16)
    w_glu_b = w_glu.astype(BF16)
    toep, bst, cst, avec = _s5_tables(ssm_a_re, ssm_a_im, ssm_b_re, ssm_b_im, ssm_c_re, ssm_c_im,
                                      ssm_d, ssm_log_step)
    half = HEAD_DIM // 2
    inv_freq = jnp.tile(ROPE_THETA ** (-jnp.arange(half, dtype=F32) / half), LANES // half)
    pos_pm = jnp.swapaxes(positions.reshape(bsz, nchunks, CHUNK), 1, 2)
    ang = pos_pm.astype(F32)[..., None] * inv_freq
    cos_t, sin_t = jnp.cos(ang), jnp.sin(ang)

    mod = _ada_mod(c, w_ada, b_ada).reshape(depth, bsz, 1, 3 * d)
    norm_g3 = norm_g.reshape(depth, 1, d)
    b_glu3 = b_glu.reshape(depth, 1, 2 * width)
    lam3 = [v.reshape(depth, 1, HEAD_DIM) for v in (lam_q1, lam_k1, lam_q2, lam_k2)]
    sub_g3 = sub_g.reshape(depth, 1, 2 * HEAD_DIM)
    final_g2 = final_g.reshape(1, d)

    x = _to_phase_major(x)
    for l in range(depth):
        lambda_init = 0.8 - 0.6 * math.exp(-0.3 * l)
        shift, scale, gate = mod[l, :, :, :d], mod[l, :, :, d:2 * d], mod[l, :, :, 2 * d:]
        at, proj = _in_proj(x, shift, scale, norm_g3, w_in_p, w_ut, l, cos_t, sin_t)
        gt = _s5_core(at, toep, bst, cst, avec, l)
        ys = _glu(gt, proj, w_glu_b, b_glu3, l)
        ya = _attention(proj, *lam3, sub_g3, l, lambda_init, TOKEN_TILE)
        x = _out_proj(ys, ya, w_out_b, x, gate, final_g2, l, l == depth - 1)
    return x
```

```python
import functools
import math

import jax
import jax.numpy as jnp
import numpy as np
from jax import lax
from jax.experimental import pallas as pl
from jax.experimental.pallas import tpu as pltpu

LANES = 128
SUBLANES = 8
V7X_VMEM_REQUEST_BYTES = 60 * 1024 * 1024

SSM_GROUP = 16
SSM_STATE = 64
HEAD_DIM = 64
ROPE_THETA = 10000.0
NORM_EPS = 1e-6
CHUNK = 16
PAIR = 4 * HEAD_DIM
TK = 256
TOKEN_TILE = 512
ONES_ROWS = 16
IN_PROJ_PHASES = 4
Q_SCALE = HEAD_DIM ** -0.5 * math.log2(math.e)

F32 = jnp.float32
BF16 = jnp.bfloat16


def _params(*sem):
    return pltpu.CompilerParams(dimension_semantics=sem, vmem_limit_bytes=V7X_VMEM_REQUEST_BYTES)


def _ada_kernel(c_ref, w_ref, b_ref, o_ref):
    c = c_ref[...]
    act = c * jax.nn.sigmoid(c)
    o_ref[...] = jnp.dot(act, w_ref[...], preferred_element_type=F32,
                         precision=lax.Precision.HIGHEST) + b_ref[...]


def _ada_mod(c, w_ada, b_ada):
    depth, d, n3 = w_ada.shape
    bsz = c.shape[0]
    tn = _tile(n3, 1024)
    return pl.pallas_call(
        _ada_kernel,
        grid=(depth, n3 // tn),
        in_specs=[pl.BlockSpec((bsz, d), lambda l, n: (0, 0)),
                  pl.BlockSpec((None, d, tn), lambda l, n: (l, 0, n)),
                  pl.BlockSpec((None, 1, tn), lambda l, n: (l, 0, n))],
        out_specs=pl.BlockSpec((None, bsz, tn), lambda l, n: (l, 0, n)),
        out_shape=jax.ShapeDtypeStruct((depth, bsz, n3), F32),
        compiler_params=_params("arbitrary", "arbitrary"),
        name="ada_mod",
    )(c, w_ada, b_ada.reshape(depth, 1, n3))


def _in_proj_kernel(x_ref, shift_ref, scale_ref, g_ref, w_ref, wut_ref, cos_ref, sin_ref, at_ref, o_ref, h_ref,
                    *, npairs, nc, ph):
    n = pl.program_id(1)
    ip = pl.program_id(2)
    width = w_ref.shape[-1]

    @pl.when(n == 0)
    def _():
        for t in range(ph):
            xf = x_ref[t]
            y = xf * lax.rsqrt(jnp.mean(xf * xf, axis=-1, keepdims=True) + NORM_EPS) * g_ref[...]
            h = (y * (1.0 + scale_ref[...]) + shift_ref[...]).astype(BF16)
            h_ref[ip, t * nc:(t + 1) * nc, :] = h
            ut = lax.dot_general(wut_ref[...], h, (((1,), (1,)), ((), ())), preferred_element_type=F32)
            at_ref[:, t * SSM_GROUP:(t + 1) * SSM_GROUP, :] = ut.reshape(-1, SSM_GROUP, nc).astype(BF16)

    @pl.when(n > 0)
    def _():
        is_rope = jnp.logical_or(n == 2, n == 3)
        qs = jnp.where(n == 2, Q_SCALE, 1.0).astype(F32)
        cs = [jnp.where(is_rope, cos_ref[t] * qs, 1.0) for t in range(ph)]
        sn = [jnp.where(is_rope, sin_ref[t] * qs, 0.0) for t in range(ph)]
        h = h_ref[ip]
        for j in range(npairs):
            lo = j * PAIR
            acc = jnp.dot(h, w_ref[:, lo:lo + PAIR], preferred_element_type=F32)
            for t in range(ph):
                t1 = acc[t * nc:(t + 1) * nc, :LANES]
                t2 = acc[t * nc:(t + 1) * nc, LANES:]
                o_ref[t, :, lo:lo + LANES] = (t1 * cs[t] - t2 * sn[t]).astype(BF16)
                o_ref[t, :, lo + LANES:lo + PAIR] = (t2 * cs[t] + t1 * sn[t]).astype(BF16)


def _in_proj(x, shift, scale, norm_g, w_in_p, w_ut, layer, cos_t, sin_t):
    bsz, _, nc, d = x.shape
    width = w_in_p.shape[-1] // 5
    npairs = width // PAIR
    groups = width // SSM_GROUP
    ph = IN_PROJ_PHASES
    nip = CHUNK // ph
    seg = lambda n: jnp.maximum(n, 1)
    u_step = lambda n, ip: jnp.where(n == 0, ip, nip - 1)
    o_step = lambda n, ip: jnp.where(n == 0, 0, ip)
    return pl.pallas_call(
        functools.partial(_in_proj_kernel, npairs=npairs, nc=nc, ph=ph),
        grid=(bsz, 6, nip),
        in_specs=[pl.BlockSpec((None, ph, nc, d), lambda b, n, ip: (b, u_step(n, ip), 0, 0)),
                  pl.BlockSpec((None, 1, d), lambda b, n, ip: (b, 0, 0)),
                  pl.BlockSpec((None, 1, d), lambda b, n, ip: (b, 0, 0)),
                  pl.BlockSpec((None, 1, d), lambda b, n, ip: (layer, 0, 0)),
                  pl.BlockSpec((None, d, width), lambda b, n, ip: (layer, 0, seg(n) - 1)),
                  pl.BlockSpec((None, width, d), lambda b, n, ip: (layer, 0, 0), pipeline_mode=pl.Buffered(1)),
                  pl.BlockSpec((None, ph, nc, LANES), lambda b, n, ip: (b, ip, 0, 0)),
                  pl.BlockSpec((None, ph, nc, LANES), lambda b, n, ip: (b, ip, 0, 0))],
        out_specs=[pl.BlockSpec((groups, None, ph * SSM_GROUP, nc), lambda b, n, ip: (0, b, u_step(n, ip), 0)),
                   pl.BlockSpec((None, None, ph, nc, width), lambda b, n, ip: (seg(n) - 1, b, o_step(n, ip), 0, 0))],
        out_shape=[jax.ShapeDtypeStruct((groups, bsz, CHUNK * SSM_GROUP, nc), BF16),
                   jax.ShapeDtypeStruct((5, bsz, CHUNK, nc, width), BF16)],
        scratch_shapes=[pltpu.VMEM((nip, ph * nc, d), BF16)],
        compiler_params=_params("arbitrary", "arbitrary", "arbitrary"),
        name="in_proj",
    )(x, shift, scale, norm_g, w_in_p, w_ut, cos_t, sin_t)


def _gelu_tanh(y):
    return 0.5 * y * (1.0 + jnp.tanh(math.sqrt(2.0 / math.pi) * (y + 0.044715 * (y * y * y))))


def _s5_kernel(at_ref, toep_ref, bst_ref, cst_ref, av_ref, gt_ref, a_ref, sb_ref, xp_ref, *, bsz, nchunks):
    ns2 = 2 * SSM_STATE
    kk = CHUNK * SSM_GROUP
    for b in range(bsz):
        a_ref[b * nchunks:(b + 1) * nchunks, :] = at_ref[b].T
    a = a_ref[...]
    sb = jnp.dot(a, bst_ref[...], preferred_element_type=F32)
    sb_ref[...] = jnp.swapaxes(sb.reshape(bsz, nchunks, 2 * ns2), 0, 1).reshape(nchunks * bsz, 2 * ns2)
    ar = av_ref[0:1, :]
    ai1 = av_ref[1:2, :]
    ai2 = av_ref[2:3, :]

    def body(c, carry):
        xs, xw = carry
        r = pl.multiple_of(c * bsz, bsz)
        xp_ref[pl.ds(r, bsz), :] = xs
        sb = sb_ref[pl.ds(r, bsz), :]
        return (ar * xs + ai1 * xw + sb[:, :ns2], ar * xw + ai2 * xs + sb[:, ns2:])

    zero = jnp.zeros((bsz, ns2), F32)
    lax.fori_loop(0, nchunks, body, (zero, zero), unroll=8)
    xp = jnp.swapaxes(xp_ref[...].reshape(nchunks, bsz, ns2), 0, 1).reshape(bsz * nchunks, ns2)
    y = jnp.dot(a, toep_ref[...], preferred_element_type=F32)
    y = y + jnp.dot(xp.astype(BF16), cst_ref[...], preferred_element_type=F32)
    gy = _gelu_tanh(y).astype(BF16)
    for b in range(bsz):
        gt_ref[b] = gy[b * nchunks:(b + 1) * nchunks, :].T


def _s5_core(at, toep, bst, cst, avec, layer):
    g, bsz, kk, nchunks = at.shape
    ns2 = 2 * SSM_STATE
    m = bsz * nchunks
    return pl.pallas_call(
        functools.partial(_s5_kernel, bsz=bsz, nchunks=nchunks),
        grid=(g,),
        in_specs=[pl.BlockSpec((None, bsz, kk, nchunks), lambda i: (i, 0, 0, 0)),
                  pl.BlockSpec((None, None, kk, kk), lambda i: (layer, i, 0, 0)),
                  pl.BlockSpec((None, None, kk, 2 * ns2), lambda i: (layer, i, 0, 0)),
                  pl.BlockSpec((None, None, ns2, kk), lambda i: (layer, i, 0, 0)),
                  pl.BlockSpec((None, None, 4, ns2), lambda i: (layer, i, 0, 0))],
        out_specs=pl.BlockSpec((None, bsz, kk, nchunks), lambda i: (i, 0, 0, 0)),
        out_shape=jax.ShapeDtypeStruct((g, bsz, kk, nchunks), BF16),
        scratch_shapes=[pltpu.VMEM((m, kk), BF16), pltpu.VMEM((m, 2 * ns2), F32), pltpu.VMEM((m, ns2), F32)],
        compiler_params=_params("arbitrary"),
        name="s5_core",
    )(at, toep, bst, cst, avec)


def _s5_tables(a_re, a_im, b_re, b_im, c_re, c_im, d_skip, log_step):
    step = jnp.exp(log_step)[..., None]
    lr, li = a_re * step, a_im * step
    tau = jnp.arange(CHUNK + 1, dtype=F32)[:, None]
    mag = jnp.exp(tau * lr[..., None, :])
    pw_re = mag * jnp.cos(tau * li[..., None, :])
    pw_im = mag * jnp.sin(tau * li[..., None, :])
    num_re, num_im = pw_re[..., 1, :] - 1.0, pw_im[..., 1, :]
    den = a_re * a_re + a_im * a_im
    cf_re = (num_re * a_re + num_im * a_im) / den
    cf_im = (num_im * a_re - num_re * a_im) / den
    bt_re, bt_im = jnp.swapaxes(b_re, -1, -2), jnp.swapaxes(b_im, -1, -2)
    bb_re = cf_re[..., None, :] * bt_re - cf_im[..., None, :] * bt_im
    bb_im = cf_re[..., None, :] * bt_im + cf_im[..., None, :] * bt_re
    e_re = pw_re[..., None, :] * bb_re[..., None, :, :] - pw_im[..., None, :] * bb_im[..., None, :, :]
    e_im = pw_re[..., None, :] * bb_im[..., None, :, :] + pw_im[..., None, :] * bb_re[..., None, :, :]
    cn_re, cn_im = jnp.moveaxis(c_re, -1, 2), jnp.moveaxis(c_im, -1, 2)
    en_re, en_im = jnp.moveaxis(e_re, -1, 2), jnp.moveaxis(e_im, -1, 2)
    kern = jnp.sum(cn_re[:, :, :, None, :, None] * en_re[:, :, :, :, None, :]
                   - cn_im[:, :, :, None, :, None] * en_im[:, :, :, :, None, :], axis=2)
    eye = jnp.eye(SSM_GROUP, dtype=F32)
    kern = kern.at[:, :, 0].add(d_skip[..., :, None] * eye)
    idx = jnp.arange(CHUNK)
    lag = idx[None, :] - idx[:, None]
    blk = jnp.where((lag >= 0)[..., None, None], kern[:, :, jnp.clip(lag, 0)], 0.0)
    dp, g = a_re.shape[:2]
    kk = CHUNK * SSM_GROUP
    toep = blk.transpose(0, 1, 2, 5, 3, 4).reshape(dp, g, kk, kk)
    back = CHUNK - 1 - idx
    bs_re = e_re[:, :, back].reshape(dp, g, kk, SSM_STATE)
    bs_im = e_im[:, :, back].reshape(dp, g, kk, SSM_STATE)
    bst = jnp.concatenate([bs_re, bs_im, bs_im, bs_re], axis=-1)
    fw_re, fw_im = pw_re[:, :, 1:], pw_im[:, :, 1:]
    ca_re = c_re[:, :, None] * fw_re[..., None, :] - c_im[:, :, None] * fw_im[..., None, :]
    ca_im = c_re[:, :, None] * fw_im[..., None, :] + c_im[:, :, None] * fw_re[..., None, :]
    cst = jnp.concatenate([ca_re.transpose(0, 1, 4, 2, 3).reshape(dp, g, SSM_STATE, kk),
                           -ca_im.transpose(0, 1, 4, 2, 3).reshape(dp, g, SSM_STATE, kk)], axis=2)
    ar, ai = pw_re[:, :, CHUNK], pw_im[:, :, CHUNK]
    avec = jnp.stack([jnp.concatenate([ar, ar], -1), jnp.concatenate([-ai, ai], -1),
                      jnp.concatenate([ai, -ai], -1), jnp.zeros_like(jnp.concatenate([ar, ar], -1))], axis=2)
    return toep.astype(BF16), bst.astype(BF16), cst.astype(BF16), avec


def _glu_kernel(gt_ref, zs_ref, w_ref, b_ref, o_ref, *, width, nc):
    gt = gt_ref[...]
    gy = jnp.concatenate([gt[:, t * SSM_GROUP:(t + 1) * SSM_GROUP, :].reshape(width, nc).T for t in range(2)],
                         axis=0)
    for lo in range(0, width, PAIR):
        a = jnp.dot(gy, w_ref[:, lo:lo + PAIR], preferred_element_type=F32) + b_ref[:, lo:lo + PAIR]
        g = jnp.dot(gy, w_ref[:, width + lo:width + lo + PAIR], preferred_element_type=F32)
        out = a * jax.nn.sigmoid(g + b_ref[:, width + lo:width + lo + PAIR])
        for t in range(2):
            z = zs_ref[t, :, lo:lo + PAIR].astype(F32)
            o_ref[t, :, lo:lo + PAIR] = (out[t * nc:(t + 1) * nc, :] * (z * jax.nn.sigmoid(z))).astype(BF16)


def _glu(gt, proj, w_glu_b, b_glu, layer):
    groups, bsz, kk, nc = gt.shape
    width = groups * SSM_GROUP
    return pl.pallas_call(
        functools.partial(_glu_kernel, width=width, nc=nc),
        grid=(bsz, CHUNK // 2),
        in_specs=[pl.BlockSpec((groups, None, 2 * SSM_GROUP, nc), lambda b, ip: (0, b, ip, 0)),
                  pl.BlockSpec((None, None, 2, nc, width), lambda b, ip: (0, b, ip, 0, 0)),
                  pl.BlockSpec((None, width, 2 * width), lambda b, ip: (layer, 0, 0)),
                  pl.BlockSpec((None, 1, 2 * width), lambda b, ip: (layer, 0, 0))],
        out_specs=pl.BlockSpec((None, 2, nc, width), lambda b, ip: (b, ip, 0, 0)),
        out_shape=jax.ShapeDtypeStruct((bsz, CHUNK, nc, width), BF16),
        compiler_params=_params("arbitrary", "arbitrary"),
        name="s5_glu",
    )(gt, proj, w_glu_b, b_glu)


def _attn_kernel(linit_ref, q_ref, k_ref, v_ref, za_ref, lq1_ref, lk1_ref, lq2_ref, lk2_ref, sg_ref,
                 o_ref, km_ref, vt_ref, qt_ref, m_ref, l_ref, acc_ref, s0_ref, s1_ref, p0_ref, p1_ref,
                 a0_ref, a1_ref, c0_ref, c1_ref, *, tq):
    qi = pl.program_id(2)
    nq = tq // TK

    @pl.when(qi == 0)
    def _():
        kk = _natural(k_ref[...])
        grp = (lax.broadcasted_iota(jnp.int32, kk.shape, 1) % LANES) // (HEAD_DIM // 2)
        for combo in range(4):
            km_ref[combo] = jnp.where(grp == combo, kk, jnp.zeros_like(kk))
        vt = _natural(v_ref[...]).astype(F32).T.astype(BF16)
        for hp in range(2):
            vt_ref[hp, :LANES, :] = vt[hp * LANES:(hp + 1) * LANES]
            vt_ref[hp, LANES:, :] = jnp.ones((ONES_ROWS, vt.shape[1]), BF16)

    qt_ref[...] = _natural(q_ref[...]).astype(F32).T.astype(BF16)
    m_ref[...] = jnp.full(m_ref.shape, -jnp.inf, F32)
    l_ref[...] = jnp.zeros(l_ref.shape, F32)
    acc_ref[...] = jnp.zeros(acc_ref.shape, F32)

    def scores(kt, s_ref, cm_ref, c0):
        ks = pl.multiple_of(kt * TK, TK)
        qt = qt_ref[:, c0:]
        for combo in range(4):
            s = jnp.dot(km_ref[combo, pl.ds(ks, TK), :], qt, preferred_element_type=F32)
            for c in range(c0, tq, LANES):
                s_ref[combo, c // LANES] = s[:, c - c0:c - c0 + LANES]
            cm_ref[combo, :, c0:] = jnp.max(s, axis=0, keepdims=True)

    def softmax(s_ref, cm_ref, p_ref, a_ref, c0, shift):
        for combo in range(4):
            for c in range(c0, tq, LANES):
                cs = slice(c, c + LANES)
                if shift is not None:
                    row = lax.broadcasted_iota(jnp.int32, (TK, LANES), 0)
                    col = lax.broadcasted_iota(jnp.int32, (TK, LANES), 1)
                    s = jnp.where(row + (shift - c) <= col, s_ref[combo, c // LANES], -jnp.inf)
                    cmax = jnp.max(s, axis=0, keepdims=True)
                else:
                    s = s_ref[combo, c // LANES]
                    cmax = cm_ref[combo, :, cs]
                m_prev = m_ref[combo, :, cs]
                m_new = jnp.maximum(m_prev, cmax)
                alpha = jnp.exp2(m_prev - m_new)
                p = jnp.exp2(s - m_new)
                p_ref[combo, c // LANES] = p.astype(BF16)
                a_ref[combo, :, cs] = alpha
                m_ref[combo, :, cs] = m_new

    def values(kt, p_ref, a_ref, c0):
        ks = pl.multiple_of(kt * TK, TK)
        cs = slice(c0, tq)
        for combo in range(4):
            hp = combo // 2
            p = jnp.concatenate([p_ref[combo, c // LANES] for c in range(c0, tq, LANES)], axis=1)
            pv = jnp.dot(vt_ref[hp, :, pl.ds(ks, TK)], p, preferred_element_type=F32)
            alpha = a_ref[combo, :, cs]
            acc_ref[combo, :, cs] = alpha * acc_ref[combo, :, cs] + pv[:LANES]
            l_ref[combo, :, cs] = alpha * l_ref[combo, :, cs] + pv[LANES:LANES + 1]

    kd = qi * nq

    def band():
        scores(kd + 1, s1_ref, c1_ref, TK)
        softmax(s0_ref, c0_ref, p0_ref, a0_ref, 0, 0)
        values(kd, p0_ref, a0_ref, 0)
        softmax(s1_ref, c1_ref, p1_ref, a1_ref, TK, TK)
        values(kd + 1, p1_ref, a1_ref, TK)

    @pl.when(qi == 0)
    def _():
        scores(0, s0_ref, c0_ref, 0)
        band()

    @pl.when(qi > 0)
    def _():
        scores(0, s0_ref, c0_ref, 0)
        scores(1, s1_ref, c1_ref, 0)
        softmax(s0_ref, c0_ref, p0_ref, a0_ref, 0, None)

        def pair(i, carry):
            kt = 2 * i
            scores(kt + 2, s0_ref, c0_ref, 0)
            softmax(s1_ref, c1_ref, p1_ref, a1_ref, 0, None)
            values(kt, p0_ref, a0_ref, 0)
            scores(kt + 3, s1_ref, c1_ref, 0)
            softmax(s0_ref, c0_ref, p0_ref, a0_ref, 0, None)
            values(kt + 1, p1_ref, a1_ref, 0)
            return carry

        lax.fori_loop(0, qi - 1, pair, 0)
        softmax(s1_ref, c1_ref, p1_ref, a1_ref, 0, None)
        values(kd - 2, p0_ref, a0_ref, 0)
        values(kd - 1, p1_ref, a1_ref, 0)
        scores(kd, s0_ref, c0_ref, 0)
        band()

    linit = linit_ref[0]
    lam = (jnp.exp(jnp.sum(lq1_ref[...] * lk1_ref[...], axis=-1, keepdims=True))
           - jnp.exp(jnp.sum(lq2_ref[...] * lk2_ref[...], axis=-1, keepdims=True)) + linit)
    za = _natural(za_ref[...]).astype(F32)
    for hp in range(2):
        r1 = 1.0 / l_ref[2 * hp]
        r2 = lam / l_ref[2 * hp + 1]
        ot = acc_ref[2 * hp] * r1 - acc_ref[2 * hp + 1] * r2
        o = ot.T
        y = o * lax.rsqrt(jnp.mean(o * o, axis=-1, keepdims=True) + NORM_EPS) * sg_ref[...]
        y = y * (1.0 - linit)
        z = za[:, hp * LANES:(hp + 1) * LANES]
        y = y * (z * jax.nn.sigmoid(z))
        o_ref[:, :, hp * LANES:(hp + 1) * LANES] = jnp.swapaxes(
            y.reshape(tq // CHUNK, CHUNK, LANES), 0, 1).astype(BF16)


def _natural(v):
    return jnp.swapaxes(v, 0, 1).reshape(v.shape[0] * v.shape[1], v.shape[2])


def _attention(proj, lam_q1, lam_k1, lam_q2, lam_k2, sub_g, layer, lambda_init, tq):
    _, bsz, _, nc, width = proj.shape
    s = nc * CHUNK
    tc = tq // CHUNK
    npairs = width // PAIR
    assert tq == 2 * TK
    vec = lambda: pl.BlockSpec((None, 1, HEAD_DIM), lambda b, j, i: (layer, 0, 0))
    sbuf = lambda: pltpu.VMEM((4, tq // LANES, TK, LANES), F32)
    pbuf = lambda: pltpu.VMEM((4, tq // LANES, TK, LANES), BF16)
    rowv = lambda: pltpu.VMEM((4, 1, tq), F32)
    return pl.pallas_call(
        functools.partial(_attn_kernel, tq=tq),
        grid=(bsz, npairs, s // tq),
        in_specs=[pl.BlockSpec(memory_space=pltpu.SMEM),
                  pl.BlockSpec((None, None, CHUNK, tc, PAIR), lambda b, j, i: (1, b, 0, i, j)),
                  pl.BlockSpec((None, None, CHUNK, nc, PAIR), lambda b, j, i: (2, b, 0, 0, j)),
                  pl.BlockSpec((None, None, CHUNK, nc, PAIR), lambda b, j, i: (3, b, 0, 0, j)),
                  pl.BlockSpec((None, None, CHUNK, tc, PAIR), lambda b, j, i: (4, b, 0, i, j)),
                  vec(), vec(), vec(), vec(),
                  pl.BlockSpec((None, 1, 2 * HEAD_DIM), lambda b, j, i: (layer, 0, 0))],
        out_specs=pl.BlockSpec((None, CHUNK, tc, PAIR), lambda b, j, i: (b, 0, i, j)),
        out_shape=jax.ShapeDtypeStruct((bsz, CHUNK, nc, width), BF16),
        scratch_shapes=[pltpu.VMEM((4, s, PAIR), BF16), pltpu.VMEM((2, LANES + ONES_ROWS, s), BF16),
                        pltpu.VMEM((PAIR, tq), BF16),
                        rowv(), rowv(), pltpu.VMEM((4, LANES, tq), F32),
                        sbuf(), sbuf(), pbuf(), pbuf(), rowv(), rowv(), rowv(), rowv()],
        compiler_params=_params("arbitrary", "arbitrary", "arbitrary"),
        name="diff_attn",
    )(jnp.full((1,), lambda_init, F32), proj, proj, proj, proj, lam_q1, lam_k1, lam_q2, lam_k2, sub_g)


def _out_proj_kernel(ys_ref, ya_ref, ws_ref, wa_ref, x_ref, gate_ref, fg_ref, o_ref, *, final):
    pb, cb, width = ys_ref.shape
    rows = pb * cb
    y = jnp.dot(ys_ref[...].reshape(rows, width), ws_ref[...], preferred_element_type=F32)
    y = y + jnp.dot(ya_ref[...].reshape(rows, width), wa_ref[...], preferred_element_type=F32)
    xn = x_ref[...].reshape(rows, -1) + gate_ref[...] * y
    if final:
        xn = xn * lax.rsqrt(jnp.mean(xn * xn, axis=-1, keepdims=True) + NORM_EPS) * fg_ref[...]
        o_ref[...] = _natural(xn.reshape(pb, cb, -1))
    else:
        o_ref[...] = xn.reshape(pb, cb, -1)


def _out_proj(ys, ya, w_out_b, x, gate, final_g, layer, final):
    bsz, _, nc, d = x.shape
    width = ys.shape[-1]
    if final:
        pb, cb = CHUNK, TOKEN_TILE // CHUNK
        out_spec = pl.BlockSpec((None, TOKEN_TILE, d), lambda b, p, m: (b, m, 0))
        out_shape = jax.ShapeDtypeStruct((bsz, nc * CHUNK, d), F32)
    else:
        pb, cb = TOKEN_TILE // nc, nc
        out_spec = pl.BlockSpec((None, pb, cb, d), lambda b, p, m: (b, p, m, 0))
        out_shape = jax.ShapeDtypeStruct(x.shape, F32)
    tile = lambda last: pl.BlockSpec((None, pb, cb, last), lambda b, p, m: (b, p, m, 0))
    return pl.pallas_call(
        functools.partial(_out_proj_kernel, final=final),
        grid=(bsz, CHUNK // pb, nc // cb),
        in_specs=[tile(width), tile(width),
                  pl.BlockSpec((None, width, d), lambda b, p, m: (layer, 0, 0)),
                  pl.BlockSpec((None, width, d), lambda b, p, m: (layer, 1, 0)),
                  tile(d),
                  pl.BlockSpec((None, 1, d), lambda b, p, m: (b, 0, 0)),
                  pl.BlockSpec((1, d), lambda b, p, m: (0, 0))],
        out_specs=out_spec,
        out_shape=out_shape,
        compiler_params=_params("arbitrary", "arbitrary", "arbitrary"),
        name="out_proj",
    )(ys, ya, w_out_b, w_out_b, x, gate, final_g)


def _to_phase_major_kernel(x_ref, o_ref):
    rows, d = x_ref.shape
    o_ref[...] = jnp.swapaxes(x_ref[...].reshape(rows // CHUNK, CHUNK, d), 0, 1)


def _to_phase_major(x):
    bsz, s, d = x.shape
    tc = TOKEN_TILE // CHUNK
    return pl.pallas_call(
        _to_phase_major_kernel,
        grid=(bsz, s // TOKEN_TILE),
        in_specs=[pl.BlockSpec((None, TOKEN_TILE, d), lambda b, m: (b, m, 0))],
        out_specs=pl.BlockSpec((None, CHUNK, tc, d), lambda b, m: (b, 0, m, 0)),
        out_shape=jax.ShapeDtypeStruct((bsz, CHUNK, s // CHUNK, d), F32),
        compiler_params=_params("arbitrary", "arbitrary"),
        name="to_phase_major",
    )(x)


def _qk_relayout(w):
    half = HEAD_DIM // 2
    lead = w.shape[:-1]
    w = w.reshape(*lead, -1, 2, 2, 2, half)
    nd = len(lead)
    w = w.transpose(*range(nd), nd, nd + 3, nd + 1, nd + 2, nd + 4)
    return w.reshape(*lead, -1)


def _tile(n, target):
    t = min(n, target)
    while n % t or t % LANES:
        t -= LANES
    return t


def kernel(x, c, positions, norm_g, w_ada, b_ada, w_in, w_out, ssm_a_re, ssm_a_im, ssm_b_re, ssm_b_im,
           ssm_c_re, ssm_c_im, ssm_d, ssm_log_step, w_glu, b_glu, lam_q1, lam_k1, lam_q2, lam_k2,
           sub_g, final_g):
    bsz, s, d = x.shape
    depth = w_in.shape[0]
    width = d // 2
    nchunks = s // CHUNK
    assert w_in.shape[-1] == 6 * width and width % PAIR == 0 and bsz % SUBLANES == 0
    assert s % TOKEN_TILE == 0 and TOKEN_TILE % nchunks == 0

    w_in_p = jnp.concatenate(
        [w_in[:, :, width:2 * width].astype(BF16), _qk_relayout(w_in[:, :, 2 * width:3 * width]).astype(BF16),
         _qk_relayout(w_in[:, :, 3 * width:4 * width]).astype(BF16), w_in[:, :, 4 * width:].astype(BF16)], axis=-1)
    w_ut = jnp.swapaxes(lax.optimization_barrier(w_in[:, :, :width]), 1, 2).astype(BF16)
    w_out_b = w_out.astype(BF16)
    w_glu_b = w_glu.astype(BF16)
    toep, bst, cst, avec = _s5_tables(ssm_a_re, ssm_a_im, ssm_b_re, ssm_b_im, ssm_c_re, ssm_c_im,
                                      ssm_d, ssm_log_step)
    half = HEAD_DIM // 2
    inv_freq = jnp.tile(ROPE_THETA ** (-jnp.arange(half, dtype=F32) / half), LANES // half)
    pos_pm = jnp.swapaxes(positions.reshape(bsz, nchunks, CHUNK), 1, 2)
    ang = pos_pm.astype(F32)[..., None] * inv_freq
    cos_t, sin_t = jnp.cos(ang), jnp.sin(ang)

    mod = _ada_mod(c, w_ada, b_ada).reshape(depth, bsz, 1, 3 * d)
    norm_g3 = norm_g.reshape(depth, 1, d)
    b_glu3 = b_glu.reshape(depth, 1, 2 * width)
    lam3 = [v.reshape(depth, 1, HEAD_DIM) for v in (lam_q1, lam_k1, lam_q2, lam_k2)]
    sub_g3 = sub_g.reshape(depth, 1, 2 * HEAD_DIM)
    final_g2 = final_g.reshape(1, d)

    x = _to_phase_major(x)
    for l in range(depth):
        lambda_init = 0.8 - 0.6 * math.exp(-0.3 * l)
        shift, scale, gate = mod[l, :, :, :d], mod[l, :, :, d:2 * d], mod[l, :, :, 2 * d:]
        at, proj = _in_proj(x, shift, scale, norm_g3, w_in_p, w_ut, l, cos_t, sin_t)
        gt = _s5_core(at, toep, bst, cst, avec, l)
        ys = _glu(gt, proj, w_glu_b, b_glu3, l)
        ya = _attention(proj, *lam3, sub_g3, l, lambda_init, TOKEN_TILE)
        x = _out_proj(ys, ya, w_out_b, x, gate, final_g2, l, l == depth - 1)
    return x
```

```python
import functools
import math

import jax
import jax.numpy as jnp
import numpy as np
from jax import lax
from jax.experimental import pallas as pl
from jax.experimental.pallas import tpu as pltpu

LANES = 128
SUBLANES = 8
V7X_VMEM_REQUEST_BYTES = 60 * 1024 * 1024

SSM_GROUP = 16
SSM_STATE = 64
HEAD_DIM = 64
ROPE_THETA = 10000.0
NORM_EPS = 1e-6
CHUNK = 16
PAIR = 4 * HEAD_DIM
TK = 256
TOKEN_TILE = 512
ONES_ROWS = 16
IN_PROJ_PHASES = 4
ATTN_TILES_PER_STEP = 2
Q_SCALE = HEAD_DIM ** -0.5 * math.log2(math.e)

F32 = jnp.float32
BF16 = jnp.bfloat16


def _params(*sem):
    return pltpu.CompilerParams(dimension_semantics=sem, vmem_limit_bytes=V7X_VMEM_REQUEST_BYTES)


def _ada_kernel(c_ref, w_ref, b_ref, o_ref):
    c = c_ref[...]
    act = c * jax.nn.sigmoid(c)
    o_ref[...] = jnp.dot(act, w_ref[...], preferred_element_type=F32,
                         precision=lax.Precision.HIGHEST) + b_ref[...]


def _ada_mod(c, w_ada, b_ada):
    depth, d, n3 = w_ada.shape
    bsz = c.shape[0]
    tn = _tile(n3, 1024)
    return pl.pallas_call(
        _ada_kernel,
        grid=(depth, n3 // tn),
        in_specs=[pl.BlockSpec((bsz, d), lambda l, n: (0, 0)),
                  pl.BlockSpec((None, d, tn), lambda l, n: (l, 0, n)),
                  pl.BlockSpec((None, 1, tn), lambda l, n: (l, 0, n))],
        out_specs=pl.BlockSpec((None, bsz, tn), lambda l, n: (l, 0, n)),
        out_shape=jax.ShapeDtypeStruct((depth, bsz, n3), F32),
        compiler_params=_params("arbitrary", "arbitrary"),
        name="ada_mod",
    )(c, w_ada, b_ada.reshape(depth, 1, n3))


def _in_proj_kernel(x_ref, shift_ref, scale_ref, g_ref, w_ref, wut_ref, cos_ref, sin_ref, at_ref, o_ref, h_ref,
                    *, npairs, nc, ph):
    n = pl.program_id(1)
    ip = pl.program_id(2)
    width = w_ref.shape[-1]

    @pl.when(n == 0)
    def _():
        for t in range(ph):
            xf = x_ref[t]
            y = xf * lax.rsqrt(jnp.mean(xf * xf, axis=-1, keepdims=True) + NORM_EPS) * g_ref[...]
            h = (y * (1.0 + scale_ref[...]) + shift_ref[...]).astype(BF16)
            h_ref[ip, t * nc:(t + 1) * nc, :] = h
            ut = lax.dot_general(wut_ref[...], h, (((1,), (1,)), ((), ())), preferred_element_type=F32)
            at_ref[:, t * SSM_GROUP:(t + 1) * SSM_GROUP, :] = ut.reshape(-1, SSM_GROUP, nc).astype(BF16)

    @pl.when(n > 0)
    def _():
        is_rope = jnp.logical_or(n == 2, n == 3)
        qs = jnp.where(n == 2, Q_SCALE, 1.0).astype(F32)
        cs = [jnp.where(is_rope, cos_ref[t] * qs, 1.0) for t in range(ph)]
        sn = [jnp.where(is_rope, sin_ref[t] * qs, 0.0) for t in range(ph)]
        h = h_ref[ip]
        for j in range(npairs):
            lo = j * PAIR
            acc = jnp.dot(h, w_ref[:, lo:lo + PAIR], preferred_element_type=F32)
            for t in range(ph):
                t1 = acc[t * nc:(t + 1) * nc, :LANES]
                t2 = acc[t * nc:(t + 1) * nc, LANES:]
                o_ref[t, :, lo:lo + LANES] = (t1 * cs[t] - t2 * sn[t]).astype(BF16)
                o_ref[t, :, lo + LANES:lo + PAIR] = (t2 * cs[t] + t1 * sn[t]).astype(BF16)


def _in_proj(x, shift, scale, norm_g, w_in_p, w_ut, layer, cos_t, sin_t):
    bsz, _, nc, d = x.shape
    width = w_in_p.shape[-1] // 5
    npairs = width // PAIR
    groups = width // SSM_GROUP
    ph = IN_PROJ_PHASES
    nip = CHUNK // ph
    seg = lambda n: jnp.maximum(n, 1)
    u_step = lambda n, ip: jnp.where(n == 0, ip, nip - 1)
    o_step = lambda n, ip: jnp.where(n == 0, 0, ip)
    return pl.pallas_call(
        functools.partial(_in_proj_kernel, npairs=npairs, nc=nc, ph=ph),
        grid=(bsz, 6, nip),
        in_specs=[pl.BlockSpec((None, ph, nc, d), lambda b, n, ip: (b, u_step(n, ip), 0, 0)),
                  pl.BlockSpec((None, 1, d), lambda b, n, ip: (b, 0, 0)),
                  pl.BlockSpec((None, 1, d), lambda b, n, ip: (b, 0, 0)),
                  pl.BlockSpec((None, 1, d), lambda b, n, ip: (layer, 0, 0)),
                  pl.BlockSpec((None, d, width), lambda b, n, ip: (layer, 0, seg(n) - 1)),
                  pl.BlockSpec((None, width, d), lambda b, n, ip: (layer, 0, 0), pipeline_mode=pl.Buffered(1)),
                  pl.BlockSpec((None, ph, nc, LANES), lambda b, n, ip: (b, ip, 0, 0)),
                  pl.BlockSpec((None, ph, nc, LANES), lambda b, n, ip: (b, ip, 0, 0))],
        out_specs=[pl.BlockSpec((groups, None, ph * SSM_GROUP, nc), lambda b, n, ip: (0, b, u_step(n, ip), 0)),
                   pl.BlockSpec((None, None, ph, nc, width), lambda b, n, ip: (seg(n) - 1, b, o_step(n, ip), 0, 0))],
        out_shape=[jax.ShapeDtypeStruct((groups, bsz, CHUNK * SSM_GROUP, nc), BF16),
                   jax.ShapeDtypeStruct((5, bsz, CHUNK, nc, width), BF16)],
        scratch_shapes=[pltpu.VMEM((nip, ph * nc, d), BF16)],
        compiler_params=_params("arbitrary", "arbitrary", "arbitrary"),
        name="in_proj",
    )(x, shift, scale, norm_g, w_in_p, w_ut, cos_t, sin_t)


def _gelu_tanh(y):
    return 0.5 * y * (1.0 + jnp.tanh(math.sqrt(2.0 / math.pi) * (y + 0.044715 * (y * y * y))))


def _s5_kernel(at_ref, toep_ref, bst_ref, cst_ref, av_ref, gt_ref, a_ref, sb_ref, xp_ref, *, bsz, nchunks):
    ns2 = 2 * SSM_STATE
    kk = CHUNK * SSM_GROUP
    for b in range(bsz):
        a_ref[b * nchunks:(b + 1) * nchunks, :] = at_ref[b].T
    a = a_ref[...]
    sb = jnp.dot(a, bst_ref[...], preferred_element_type=F32)
    sb_ref[...] = jnp.swapaxes(sb.reshape(bsz, nchunks, 2 * ns2), 0, 1).reshape(nchunks * bsz, 2 * ns2)
    ar = av_ref[0:1, :]
    ai1 = av_ref[1:2, :]
    ai2 = av_ref[2:3, :]

    def body(c, carry):
        xs, xw = carry
        r = pl.multiple_of(c * bsz, bsz)
        xp_ref[pl.ds(r, bsz), :] = xs
        sb = sb_ref[pl.ds(r, bsz), :]
        return (ar * xs + ai1 * xw + sb[:, :ns2], ar * xw + ai2 * xs + sb[:, ns2:])

    zero = jnp.zeros((bsz, ns2), F32)
    lax.fori_loop(0, nchunks, body, (zero, zero), unroll=8)
    xp = jnp.swapaxes(xp_ref[...].reshape(nchunks, bsz, ns2), 0, 1).reshape(bsz * nchunks, ns2)
    y = jnp.dot(a, toep_ref[...], preferred_element_type=F32)
    y = y + jnp.dot(xp.astype(BF16), cst_ref[...], preferred_element_type=F32)
    gy = _gelu_tanh(y).astype(BF16)
    for b in range(bsz):
        gt_ref[b] = gy[b * nchunks:(b + 1) * nchunks, :].T


def _s5_core(at, toep, bst, cst, avec, layer):
    g, bsz, kk, nchunks = at.shape
    ns2 = 2 * SSM_STATE
    m = bsz * nchunks
    return pl.pallas_call(
        functools.partial(_s5_kernel, bsz=bsz, nchunks=nchunks),
        grid=(g,),
        in_specs=[pl.BlockSpec((None, bsz, kk, nchunks), lambda i: (i, 0, 0, 0)),
                  pl.BlockSpec((None, None, kk, kk), lambda i: (layer, i, 0, 0)),
                  pl.BlockSpec((None, None, kk, 2 * ns2), lambda i: (layer, i, 0, 0)),
                  pl.BlockSpec((None, None, ns2, kk), lambda i: (layer, i, 0, 0)),
                  pl.BlockSpec((None, None, 4, ns2), lambda i: (layer, i, 0, 0))],
        out_specs=pl.BlockSpec((None, bsz, kk, nchunks), lambda i: (i, 0, 0, 0)),
        out_shape=jax.ShapeDtypeStruct((g, bsz, kk, nchunks), BF16),
        scratch_shapes=[pltpu.VMEM((m, kk), BF16), pltpu.VMEM((m, 2 * ns2), F32), pltpu.VMEM((m, ns2), F32)],
        compiler_params=_params("arbitrary"),
        name="s5_core",
    )(at, toep, bst, cst, avec)


def _s5_tables(a_re, a_im, b_re, b_im, c_re, c_im, d_skip, log_step):
    step = jnp.exp(log_step)[..., None]
    lr, li = a_re * step, a_im * step
    tau = jnp.arange(CHUNK + 1, dtype=F32)[:, None]
    mag = jnp.exp(tau * lr[..., None, :])
    pw_re = mag * jnp.cos(tau * li[..., None, :])
    pw_im = mag * jnp.sin(tau * li[..., None, :])
    num_re, num_im = pw_re[..., 1, :] - 1.0, pw_im[..., 1, :]
    den = a_re * a_re + a_im * a_im
    cf_re = (num_re * a_re + num_im * a_im) / den
    cf_im = (num_im * a_re - num_re * a_im) / den
    bt_re, bt_im = jnp.swapaxes(b_re, -1, -2), jnp.swapaxes(b_im, -1, -2)
    bb_re = cf_re[..., None, :] * bt_re - cf_im[..., None, :] * bt_im
    bb_im = cf_re[..., None, :] * bt_im + cf_im[..., None, :] * bt_re
    e_re = pw_re[..., None, :] * bb_re[..., None, :, :] - pw_im[..., None, :] * bb_im[..., None, :, :]
    e_im = pw_re[..., None, :] * bb_im[..., None, :, :] + pw_im[..., None, :] * bb_re[..., None, :, :]
    cn_re, cn_im = jnp.moveaxis(c_re, -1, 2), jnp.moveaxis(c_im, -1, 2)
    en_re, en_im = jnp.moveaxis(e_re, -1, 2), jnp.moveaxis(e_im, -1, 2)
    kern = jnp.sum(cn_re[:, :, :, None, :, None] * en_re[:, :, :, :, None, :]
                   - cn_im[:, :, :, None, :, None] * en_im[:, :, :, :, None, :], axis=2)
    eye = jnp.eye(SSM_GROUP, dtype=F32)
    kern = kern.at[:, :, 0].add(d_skip[..., :, None] * eye)
    idx = jnp.arange(CHUNK)
    lag = idx[None, :] - idx[:, None]
    blk = jnp.where((lag >= 0)[..., None, None], kern[:, :, jnp.clip(lag, 0)], 0.0)
    dp, g = a_re.shape[:2]
    kk = CHUNK * SSM_GROUP
    toep = blk.transpose(0, 1, 2, 5, 3, 4).reshape(dp, g, kk, kk)
    back = CHUNK - 1 - idx
    bs_re = e_re[:, :, back].reshape(dp, g, kk, SSM_STATE)
    bs_im = e_im[:, :, back].reshape(dp, g, kk, SSM_STATE)
    bst = jnp.concatenate([bs_re, bs_im, bs_im, bs_re], axis=-1)
    fw_re, fw_im = pw_re[:, :, 1:], pw_im[:, :, 1:]
    ca_re = c_re[:, :, None] * fw_re[..., None, :] - c_im[:, :, None] * fw_im[..., None, :]
    ca_im = c_re[:, :, None] * fw_im[..., None, :] + c_im[:, :, None] * fw_re[..., None, :]
    cst = jnp.concatenate([ca_re.transpose(0, 1, 4, 2, 3).reshape(dp, g, SSM_STATE, kk),
                           -ca_im.transpose(0, 1, 4, 2, 3).reshape(dp, g, SSM_STATE, kk)], axis=2)
    ar, ai = pw_re[:, :, CHUNK], pw_im[:, :, CHUNK]
    avec = jnp.stack([jnp.concatenate([ar, ar], -1), jnp.concatenate([-ai, ai], -1),
                      jnp.concatenate([ai, -ai], -1), jnp.zeros_like(jnp.concatenate([ar, ar], -1))], axis=2)
    return toep.astype(BF16), bst.astype(BF16), cst.astype(BF16), avec


def _glu_kernel(gt_ref, zs_ref, w_ref, b_ref, o_ref, *, width, nc):
    gt = gt_ref[...]
    gy = jnp.concatenate([gt[:, t * SSM_GROUP:(t + 1) * SSM_GROUP, :].reshape(width, nc).T for t in range(2)],
                         axis=0)
    for lo in range(0, width, PAIR):
        a = jnp.dot(gy, w_ref[:, lo:lo + PAIR], preferred_element_type=F32) + b_ref[:, lo:lo + PAIR]
        g = jnp.dot(gy, w_ref[:, width + lo:width + lo + PAIR], preferred_element_type=F32)
        out = a * jax.nn.sigmoid(g + b_ref[:, width + lo:width + lo + PAIR])
        for t in range(2):
            z = zs_ref[t, :, lo:lo + PAIR].astype(F32)
            o_ref[t, :, lo:lo + PAIR] = (out[t * nc:(t + 1) * nc, :] * (z * jax.nn.sigmoid(z))).astype(BF16)


def _glu(gt, proj, w_glu_b, b_glu, layer):
    groups, bsz, kk, nc = gt.shape
    width = groups * SSM_GROUP
    return pl.pallas_call(
        functools.partial(_glu_kernel, width=width, nc=nc),
        grid=(bsz, CHUNK // 2),
        in_specs=[pl.BlockSpec((groups, None, 2 * SSM_GROUP, nc), lambda b, ip: (0, b, ip, 0)),
                  pl.BlockSpec((None, None, 2, nc, width), lambda b, ip: (0, b, ip, 0, 0)),
                  pl.BlockSpec((None, width, 2 * width), lambda b, ip: (layer, 0, 0)),
                  pl.BlockSpec((None, 1, 2 * width), lambda b, ip: (layer, 0, 0))],
        out_specs=pl.BlockSpec((None, 2, nc, width), lambda b, ip: (b, ip, 0, 0)),
        out_shape=jax.ShapeDtypeStruct((bsz, CHUNK, nc, width), BF16),
        compiler_params=_params("arbitrary", "arbitrary"),
        name="s5_glu",
    )(gt, proj, w_glu_b, b_glu)


def _attn_kernel(linit_ref, q_ref, k_ref, v_ref, za_ref, lq1_ref, lk1_ref, lq2_ref, lk2_ref, sg_ref,
                 o_ref, km_ref, vt_ref, *scratch, tq, tiles):
    step = pl.program_id(2)

    @pl.when(step == 0)
    def _():
        kk = _natural(k_ref[...])
        grp = (lax.broadcasted_iota(jnp.int32, kk.shape, 1) % LANES) // (HEAD_DIM // 2)
        for combo in range(4):
            km_ref[combo] = jnp.where(grp == combo, kk, jnp.zeros_like(kk))
        vt = _natural(v_ref[...]).astype(F32).T.astype(BF16)
        for hp in range(2):
            vt_ref[hp, :LANES, :] = vt[hp * LANES:(hp + 1) * LANES]
            vt_ref[hp, LANES:, :] = jnp.ones((ONES_ROWS, vt.shape[1]), BF16)

    tc = tq // CHUNK
    for u in range(tiles):
        rows = slice(u * tc, (u + 1) * tc)
        _attn_tile(step * tiles + u, linit_ref, q_ref.at[:, rows, :], za_ref.at[:, rows, :], lq1_ref, lk1_ref,
                   lq2_ref, lk2_ref, sg_ref, o_ref.at[:, rows, :], km_ref, vt_ref, *scratch, tq=tq)


def _attn_tile(qi, linit_ref, q_ref, za_ref, lq1_ref, lk1_ref, lq2_ref, lk2_ref, sg_ref, o_ref, km_ref, vt_ref,
               qt_ref, m_ref, l_ref, acc_ref, s0_ref, s1_ref, p0_ref, p1_ref, a0_ref, a1_ref, c0_ref, c1_ref,
               *, tq):
    nq = tq // TK
    qt_ref[...] = _natural(q_ref[...]).astype(F32).T.astype(BF16)
    m_ref[...] = jnp.full(m_ref.shape, -jnp.inf, F32)
    l_ref[...] = jnp.zeros(l_ref.shape, F32)
    acc_ref[...] = jnp.zeros(acc_ref.shape, F32)

    def scores(kt, s_ref, cm_ref, c0):
        ks = pl.multiple_of(kt * TK, TK)
        qt = qt_ref[:, c0:]
        for combo in range(4):
            s = jnp.dot(km_ref[combo, pl.ds(ks, TK), :], qt, preferred_element_type=F32)
            for c in range(c0, tq, LANES):
                s_ref[combo, c // LANES] = s[:, c - c0:c - c0 + LANES]
            cm_ref[combo, :, c0:] = jnp.max(s, axis=0, keepdims=True)

    def softmax(s_ref, cm_ref, p_ref, a_ref, c0, shift):
        for combo in range(4):
            for c in range(c0, tq, LANES):
                cs = slice(c, c + LANES)
                if shift is not None:
                    row = lax.broadcasted_iota(jnp.int32, (TK, LANES), 0)
                    col = lax.broadcasted_iota(jnp.int32, (TK, LANES), 1)
                    s = jnp.where(row + (shift - c) <= col, s_ref[combo, c // LANES], -jnp.inf)
                    cmax = jnp.max(s, axis=0, keepdims=True)
                else:
                    s = s_ref[combo, c // LANES]
                    cmax = cm_ref[combo, :, cs]
                m_prev = m_ref[combo, :, cs]
                m_new = jnp.maximum(m_prev, cmax)
                alpha = jnp.exp2(m_prev - m_new)
                p = jnp.exp2(s - m_new)
                p_ref[combo, c // LANES] = p.astype(BF16)
                a_ref[combo, :, cs] = alpha
                m_ref[combo, :, cs] = m_new

    def values(kt, p_ref, a_ref, c0):
        ks = pl.multiple_of(kt * TK, TK)
        cs = slice(c0, tq)
        for combo in range(4):
            hp = combo // 2
            p = jnp.concatenate([p_ref[combo, c // LANES] for c in range(c0, tq, LANES)], axis=1)
            pv = jnp.dot(vt_ref[hp, :, pl.ds(ks, TK)], p, preferred_element_type=F32)
            alpha = a_ref[combo, :, cs]
            acc_ref[combo, :, cs] = alpha * acc_ref[combo, :, cs] + pv[:LANES]
            l_ref[combo, :, cs] = alpha * l_ref[combo, :, cs] + pv[LANES:LANES + 1]

    kd = qi * nq

    def band():
        scores(kd + 1, s1_ref, c1_ref, TK)
        softmax(s0_ref, c0_ref, p0_ref, a0_ref, 0, 0)
        values(kd, p0_ref, a0_ref, 0)
        softmax(s1_ref, c1_ref, p1_ref, a1_ref, TK, TK)
        values(kd + 1, p1_ref, a1_ref, TK)

    @pl.when(qi == 0)
    def _():
        scores(0, s0_ref, c0_ref, 0)
        band()

    @pl.when(qi > 0)
    def _():
        scores(0, s0_ref, c0_ref, 0)
        scores(1, s1_ref, c1_ref, 0)
        softmax(s0_ref, c0_ref, p0_ref, a0_ref, 0, None)

        def pair(i, carry):
            kt = 2 * i
            scores(kt + 2, s0_ref, c0_ref, 0)
            softmax(s1_ref, c1_ref, p1_ref, a1_ref, 0, None)
            values(kt, p0_ref, a0_ref, 0)
            scores(kt + 3, s1_ref, c1_ref, 0)
            softmax(s0_ref, c0_ref, p0_ref, a0_ref, 0, None)
            values(kt + 1, p1_ref, a1_ref, 0)
            return carry

        lax.fori_loop(0, qi - 1, pair, 0)
        softmax(s1_ref, c1_ref, p1_ref, a1_ref, 0, None)
        values(kd - 2, p0_ref, a0_ref, 0)
        values(kd - 1, p1_ref, a1_ref, 0)
        scores(kd, s0_ref, c0_ref, 0)
        band()

    linit = linit_ref[0]
    lam = (jnp.exp(jnp.sum(lq1_ref[...] * lk1_ref[...], axis=-1, keepdims=True))
           - jnp.exp(jnp.sum(lq2_ref[...] * lk2_ref[...], axis=-1, keepdims=True)) + linit)
    za = _natural(za_ref[...]).astype(F32)
    for hp in range(2):
        r1 = 1.0 / l_ref[2 * hp]
        r2 = lam / l_ref[2 * hp + 1]
        ot = acc_ref[2 * hp] * r1 - acc_ref[2 * hp + 1] * r2
        o = ot.T
        y = o * lax.rsqrt(jnp.mean(o * o, axis=-1, keepdims=True) + NORM_EPS) * sg_ref[...]
        y = y * (1.0 - linit)
        z = za[:, hp * LANES:(hp + 1) * LANES]
        y = y * (z * jax.nn.sigmoid(z))
        o_ref[:, :, hp * LANES:(hp + 1) * LANES] = jnp.swapaxes(
            y.reshape(tq // CHUNK, CHUNK, LANES), 0, 1).astype(BF16)


def _natural(v):
    return jnp.swapaxes(v, 0, 1).reshape(v.shape[0] * v.shape[1], v.shape[2])


def _attention(proj, lam_q1, lam_k1, lam_q2, lam_k2, sub_g, layer, lambda_init, tq):
    _, bsz, _, nc, width = proj.shape
    s = nc * CHUNK
    tiles = ATTN_TILES_PER_STEP
    tc = tiles * tq // CHUNK
    npairs = width // PAIR
    assert tq == 2 * TK and s % (tiles * tq) == 0
    vec = lambda: pl.BlockSpec((None, 1, HEAD_DIM), lambda b, j, i: (layer, 0, 0))
    sbuf = lambda: pltpu.VMEM((4, tq // LANES, TK, LANES), F32)
    pbuf = lambda: pltpu.VMEM((4, tq // LANES, TK, LANES), BF16)
    rowv = lambda: pltpu.VMEM((4, 1, tq), F32)
    return pl.pallas_call(
        functools.partial(_attn_kernel, tq=tq, tiles=tiles),
        grid=(bsz, npairs, s // (tiles * tq)),
        in_specs=[pl.BlockSpec(memory_space=pltpu.SMEM),
                  pl.BlockSpec((None, None, CHUNK, tc, PAIR), lambda b, j, i: (1, b, 0, i, j)),
                  pl.BlockSpec((None, None, CHUNK, nc, PAIR), lambda b, j, i: (2, b, 0, 0, j)),
                  pl.BlockSpec((None, None, CHUNK, nc, PAIR), lambda b, j, i: (3, b, 0, 0, j)),
                  pl.BlockSpec((None, None, CHUNK, tc, PAIR), lambda b, j, i: (4, b, 0, i, j)),
                  vec(), vec(), vec(), vec(),
                  pl.BlockSpec((None, 1, 2 * HEAD_DIM), lambda b, j, i: (layer, 0, 0))],
        out_specs=pl.BlockSpec((None, CHUNK, tc, PAIR), lambda b, j, i: (b, 0, i, j)),
        out_shape=jax.ShapeDtypeStruct((bsz, CHUNK, nc, width), BF16),
        scratch_shapes=[pltpu.VMEM((4, s, PAIR), BF16), pltpu.VMEM((2, LANES + ONES_ROWS, s), BF16),
                        pltpu.VMEM((PAIR, tq), BF16),
                        rowv(), rowv(), pltpu.VMEM((4, LANES, tq), F32),
                        sbuf(), sbuf(), pbuf(), pbuf(), rowv(), rowv(), rowv(), rowv()],
        compiler_params=_params("arbitrary", "arbitrary", "arbitrary"),
        name="diff_attn",
    )(jnp.full((1,), lambda_init, F32), proj, proj, proj, proj, lam_q1, lam_k1, lam_q2, lam_k2, sub_g)


def _out_proj_kernel(ys_ref, ya_ref, ws_ref, wa_ref, x_ref, gate_ref, fg_ref, o_ref, *, final):
    pb, cb, width = ys_ref.shape
    rows = pb * cb
    y = jnp.dot(ys_ref[...].reshape(rows, width), ws_ref[...], preferred_element_type=F32)
    y = y + jnp.dot(ya_ref[...].reshape(rows, width), wa_ref[...], preferred_element_type=F32)
    xn = x_ref[...].reshape(rows, -1) + gate_ref[...] * y
    if final:
        xn = xn * lax.rsqrt(jnp.mean(xn * xn, axis=-1, keepdims=True) + NORM_EPS) * fg_ref[...]
        o_ref[...] = _natural(xn.reshape(pb, cb, -1))
    else:
        o_ref[...] = xn.reshape(pb, cb, -1)


def _out_proj(ys, ya, w_out_b, x, gate, final_g, layer, final):
    bsz, _, nc, d = x.shape
    width = ys.shape[-1]
    if final:
        pb, cb = CHUNK, TOKEN_TILE // CHUNK
        out_spec = pl.BlockSpec((None, TOKEN_TILE, d), lambda b, p, m: (b, m, 0))
        out_shape = jax.ShapeDtypeStruct((bsz, nc * CHUNK, d), F32)
    else:
        pb, cb = TOKEN_TILE // nc, nc
        out_spec = pl.BlockSpec((None, pb, cb, d), lambda b, p, m: (b, p, m, 0))
        out_shape = jax.ShapeDtypeStruct(x.shape, F32)
    tile = lambda last: pl.BlockSpec((None, pb, cb, last), lambda b, p, m: (b, p, m, 0))
    return pl.pallas_call(
        functools.partial(_out_proj_kernel, final=final),
        grid=(bsz, CHUNK // pb, nc // cb),
        in_specs=[tile(width), tile(width),
                  pl.BlockSpec((None, width, d), lambda b, p, m: (layer, 0, 0)),
                  pl.BlockSpec((None, width, d), lambda b, p, m: (layer, 1, 0)),
                  tile(d),
                  pl.BlockSpec((None, 1, d), lambda b, p, m: (b, 0, 0)),
                  pl.BlockSpec((1, d), lambda b, p, m: (0, 0))],
        out_specs=out_spec,
        out_shape=out_shape,
        compiler_params=_params("arbitrary", "arbitrary", "arbitrary"),
        name="out_proj",
    )(ys, ya, w_out_b, w_out_b, x, gate, final_g)


def _to_phase_major_kernel(x_ref, o_ref):
    rows, d = x_ref.shape
    o_ref[...] = jnp.swapaxes(x_ref[...].reshape(rows // CHUNK, CHUNK, d), 0, 1)


def _to_phase_major(x):
    bsz, s, d = x.shape
    tc = TOKEN_TILE // CHUNK
    return pl.pallas_call(
        _to_phase_major_kernel,
        grid=(bsz, s // TOKEN_TILE),
        in_specs=[pl.BlockSpec((None, TOKEN_TILE, d), lambda b, m: (b, m, 0))],
        out_specs=pl.BlockSpec((None, CHUNK, tc, d), lambda b, m: (b, 0, m, 0)),
        out_shape=jax.ShapeDtypeStruct((bsz, CHUNK, s // CHUNK, d), F32),
        compiler_params=_params("arbitrary", "arbitrary"),
        name="to_phase_major",
    )(x)


def _qk_relayout(w):
    half = HEAD_DIM // 2
    lead = w.shape[:-1]
    w = w.reshape(*lead, -1, 2, 2, 2, half)
    nd = len(lead)
    w = w.transpose(*range(nd), nd, nd + 3, nd + 1, nd + 2, nd + 4)
    return w.reshape(*lead, -1)


def _tile(n, target):
    t = min(n, target)
    while n % t or t % LANES:
        t -= LANES
    return t


def kernel(x, c, positions, norm_g, w_ada, b_ada, w_in, w_out, ssm_a_re, ssm_a_im, ssm_b_re, ssm_b_im,
           ssm_c_re, ssm_c_im, ssm_d, ssm_log_step, w_glu, b_glu, lam_q1, lam_k1, lam_q2, lam_k2,
           sub_g, final_g):
    bsz, s, d = x.shape
    depth = w_in.shape[0]
    width = d // 2
    nchunks = s // CHUNK
    assert w_in.shape[-1] == 6 * width and width % PAIR == 0 and bsz % SUBLANES == 0
    assert s % TOKEN_TILE == 0 and TOKEN_TILE % nchunks == 0

    w_in_p = jnp.concatenate(
        [w_in[:, :, width:2 * width].astype(BF16), _qk_relayout(w_in[:, :, 2 * width:3 * width]).astype(BF16),
         _qk_relayout(w_in[:, :, 3 * width:4 * width]).astype(BF16), w_in[:, :, 4 * width:].astype(BF16)], axis=-1)
    w_ut = jnp.swapaxes(lax.optimization_barrier(w_in[:, :, :width]), 1, 2).astype(BF16)
    w_out_b = w_out.astype(BF16)
    w_glu_b = w_glu.astype(BF16)
    toep, bst, cst, avec = _s5_tables(ssm_a_re, ssm_a_im, ssm_b_re, ssm_b_im, ssm_c_re, ssm_c_im,
                                      ssm_d, ssm_log_step)
    half = HEAD_DIM // 2
    inv_freq = jnp.tile(ROPE_THETA ** (-jnp.arange(half, dtype=F32) / half), LANES // half)
    pos_pm = jnp.swapaxes(positions.reshape(bsz, nchunks, CHUNK), 1, 2)
    ang = pos_pm.astype(F32)[..., None] * inv_freq
    cos_t, sin_t = jnp.cos(ang), jnp.sin(ang)

    mod = _ada_mod(c, w_ada, b_ada).reshape(depth, bsz, 1, 3 * d)
    norm_g3 = norm_g.reshape(depth, 1, d)
    b_glu3 = b_glu.reshape(depth, 1, 2 * width)
    lam3 = [v.reshape(depth, 1, HEAD_DIM) for v in (lam_q1, lam_k1, lam_q2, lam_k2)]
    sub_g3 = sub_g.reshape(depth, 1, 2 * HEAD_DIM)
    final_g2 = final_g.reshape(1, d)

    x = _to_phase_major(x)
    for l in range(depth):
        lambda_init = 0.8 - 0.6 * math.exp(-0.3 * l)
        shift, scale, gate = mod[l, :, :, :d], mod[l, :, :, d:2 * d], mod[l, :, :, 2 * d:]
        at, proj = _in_proj(x, shift, scale, norm_g3, w_in_p, w_ut, l, cos_t, sin_t)
        gt = _s5_core(at, toep, bst, cst, avec, l)
        ys = _glu(gt, proj, w_glu_b, b_glu3, l)
        ya = _attention(proj, *lam3, sub_g3, l, lambda_init, TOKEN_TILE)
        x = _out_proj(ys, ya, w_out_b, x, gate, final_g2, l, l == depth - 1)
    return x
```

```python
import functools
import math

import jax
import jax.numpy as jnp
import numpy as np
from jax import lax
from jax.experimental import pallas as pl
from jax.experimental.pallas import tpu as pltpu

LANES = 128
SUBLANES = 8
V7X_VMEM_REQUEST_BYTES = 60 * 1024 * 1024

SSM_GROUP = 16
SSM_STATE = 64
HEAD_DIM = 64
ROPE_THETA = 10000.0
NORM_EPS = 1e-6
CHUNK = 16
PAIR = 4 * HEAD_DIM
TK = 256
TOKEN_TILE = 512
ONES_ROWS = 16
IN_PROJ_PHASES = 4
ATTN_TILES_PER_STEP = 4
Q_SCALE = HEAD_DIM ** -0.5 * math.log2(math.e)

F32 = jnp.float32
BF16 = jnp.bfloat16


def _params(*sem):
    return pltpu.CompilerParams(dimension_semantics=sem, vmem_limit_bytes=V7X_VMEM_REQUEST_BYTES)


def _ada_kernel(c_ref, w_ref, b_ref, o_ref):
    c = c_ref[...]
    act = c * jax.nn.sigmoid(c)
    o_ref[...] = jnp.dot(act, w_ref[...], preferred_element_type=F32,
                         precision=lax.Precision.HIGHEST) + b_ref[...]


def _ada_mod(c, w_ada, b_ada):
    depth, d, n3 = w_ada.shape
    bsz = c.shape[0]
    tn = _tile(n3, 1024)
    return pl.pallas_call(
        _ada_kernel,
        grid=(depth, n3 // tn),
        in_specs=[pl.BlockSpec((bsz, d), lambda l, n: (0, 0)),
                  pl.BlockSpec((None, d, tn), lambda l, n: (l, 0, n)),
                  pl.BlockSpec((None, 1, tn), lambda l, n: (l, 0, n))],
        out_specs=pl.BlockSpec((None, bsz, tn), lambda l, n: (l, 0, n)),
        out_shape=jax.ShapeDtypeStruct((depth, bsz, n3), F32),
        compiler_params=_params("arbitrary", "arbitrary"),
        name="ada_mod",
    )(c, w_ada, b_ada.reshape(depth, 1, n3))


def _in_proj_kernel(x_ref, shift_ref, scale_ref, g_ref, w_ref, wut_ref, cos_ref, sin_ref, at_ref, o_ref, h_ref,
                    *, npairs, nc, ph):
    n = pl.program_id(1)
    ip = pl.program_id(2)
    width = w_ref.shape[-1]

    @pl.when(n == 0)
    def _():
        for t in range(ph):
            xf = x_ref[t]
            y = xf * lax.rsqrt(jnp.mean(xf * xf, axis=-1, keepdims=True) + NORM_EPS) * g_ref[...]
            h = (y * (1.0 + scale_ref[...]) + shift_ref[...]).astype(BF16)
            h_ref[ip, t * nc:(t + 1) * nc, :] = h
            ut = lax.dot_general(wut_ref[...], h, (((1,), (1,)), ((), ())), preferred_element_type=F32)
            at_ref[:, t * SSM_GROUP:(t + 1) * SSM_GROUP, :] = ut.reshape(-1, SSM_GROUP, nc).astype(BF16)

    @pl.when(n > 0)
    def _():
        is_rope = jnp.logical_or(n == 2, n == 3)
        qs = jnp.where(n == 2, Q_SCALE, 1.0).astype(F32)
        cs = [jnp.where(is_rope, cos_ref[t] * qs, 1.0) for t in range(ph)]
        sn = [jnp.where(is_rope, sin_ref[t] * qs, 0.0) for t in range(ph)]
        h = h_ref[ip]
        for j in range(npairs):
            lo = j * PAIR
            acc = jnp.dot(h, w_ref[:, lo:lo + PAIR], preferred_element_type=F32)
            for t in range(ph):
                t1 = acc[t * nc:(t + 1) * nc, :LANES]
                t2 = acc[t * nc:(t + 1) * nc, LANES:]
                o_ref[t, :, lo:lo + LANES] = (t1 * cs[t] - t2 * sn[t]).astype(BF16)
                o_ref[t, :, lo + LANES:lo + PAIR] = (t2 * cs[t] + t1 * sn[t]).astype(BF16)


def _in_proj(x, shift, scale, norm_g, w_in_p, w_ut, layer, cos_t, sin_t):
    bsz, _, nc, d = x.shape
    width = w_in_p.shape[-1] // 5
    npairs = width // PAIR
    groups = width // SSM_GROUP
    ph = IN_PROJ_PHASES
    nip = CHUNK // ph
    seg = lambda n: jnp.maximum(n, 1)
    u_step = lambda n, ip: jnp.where(n == 0, ip, nip - 1)
    o_step = lambda n, ip: jnp.where(n == 0, 0, ip)
    return pl.pallas_call(
        functools.partial(_in_proj_kernel, npairs=npairs, nc=nc, ph=ph),
        grid=(bsz, 6, nip),
        in_specs=[pl.BlockSpec((None, ph, nc, d), lambda b, n, ip: (b, u_step(n, ip), 0, 0)),
                  pl.BlockSpec((None, 1, d), lambda b, n, ip: (b, 0, 0)),
                  pl.BlockSpec((None, 1, d), lambda b, n, ip: (b, 0, 0)),
                  pl.BlockSpec((None, 1, d), lambda b, n, ip: (layer, 0, 0)),
                  pl.BlockSpec((None, d, width), lambda b, n, ip: (layer, 0, seg(n) - 1)),
                  pl.BlockSpec((None, width, d), lambda b, n, ip: (layer, 0, 0), pipeline_mode=pl.Buffered(1)),
                  pl.BlockSpec((None, ph, nc, LANES), lambda b, n, ip: (b, ip, 0, 0)),
                  pl.BlockSpec((None, ph, nc, LANES), lambda b, n, ip: (b, ip, 0, 0))],
        out_specs=[pl.BlockSpec((groups, None, ph * SSM_GROUP, nc), lambda b, n, ip: (0, b, u_step(n, ip), 0)),
                   pl.BlockSpec((None, None, ph, nc, width), lambda b, n, ip: (seg(n) - 1, b, o_step(n, ip), 0, 0))],
        out_shape=[jax.ShapeDtypeStruct((groups, bsz, CHUNK * SSM_GROUP, nc), BF16),
                   jax.ShapeDtypeStruct((5, bsz, CHUNK, nc, width), BF16)],
        scratch_shapes=[pltpu.VMEM((nip, ph * nc, d), BF16)],
        compiler_params=_params("arbitrary", "arbitrary", "arbitrary"),
        name="in_proj",
    )(x, shift, scale, norm_g, w_in_p, w_ut, cos_t, sin_t)


def _gelu_tanh(y):
    return 0.5 * y * (1.0 + jnp.tanh(math.sqrt(2.0 / math.pi) * (y + 0.044715 * (y * y * y))))


def _s5_kernel(at_ref, toep_ref, bst_ref, cst_ref, av_ref, gt_ref, a_ref, sb_ref, xp_ref, *, bsz, nchunks):
    ns2 = 2 * SSM_STATE
    kk = CHUNK * SSM_GROUP
    for b in range(bsz):
        a_ref[b * nchunks:(b + 1) * nchunks, :] = at_ref[b].T
    a = a_ref[...]
    sb = jnp.dot(a, bst_ref[...], preferred_element_type=F32)
    sb_ref[...] = jnp.swapaxes(sb.reshape(bsz, nchunks, 2 * ns2), 0, 1).reshape(nchunks * bsz, 2 * ns2)
    ar = av_ref[0:1, :]
    ai1 = av_ref[1:2, :]
    ai2 = av_ref[2:3, :]

    def body(c, carry):
        xs, xw = carry
        r = pl.multiple_of(c * bsz, bsz)
        xp_ref[pl.ds(r, bsz), :] = xs
        sb = sb_ref[pl.ds(r, bsz), :]
        return (ar * xs + ai1 * xw + sb[:, :ns2], ar * xw + ai2 * xs + sb[:, ns2:])

    zero = jnp.zeros((bsz, ns2), F32)
    lax.fori_loop(0, nchunks, body, (zero, zero), unroll=8)
    xp = jnp.swapaxes(xp_ref[...].reshape(nchunks, bsz, ns2), 0, 1).reshape(bsz * nchunks, ns2)
    y = jnp.dot(a, toep_ref[...], preferred_element_type=F32)
    y = y + jnp.dot(xp.astype(BF16), cst_ref[...], preferred_element_type=F32)
    gy = _gelu_tanh(y).astype(BF16)
    for b in range(bsz):
        gt_ref[b] = gy[b * nchunks:(b + 1) * nchunks, :].T


def _s5_core(at, toep, bst, cst, avec, layer):
    g, bsz, kk, nchunks = at.shape
    ns2 = 2 * SSM_STATE
    m = bsz * nchunks
    return pl.pallas_call(
        functools.partial(_s5_kernel, bsz=bsz, nchunks=nchunks),
        grid=(g,),
        in_specs=[pl.BlockSpec((None, bsz, kk, nchunks), lambda i: (i, 0, 0, 0)),
                  pl.BlockSpec((None, None, kk, kk), lambda i: (layer, i, 0, 0)),
                  pl.BlockSpec((None, None, kk, 2 * ns2), lambda i: (layer, i, 0, 0)),
                  pl.BlockSpec((None, None, ns2, kk), lambda i: (layer, i, 0, 0)),
                  pl.BlockSpec((None, None, 4, ns2), lambda i: (layer, i, 0, 0))],
        out_specs=pl.BlockSpec((None, bsz, kk, nchunks), lambda i: (i, 0, 0, 0)),
        out_shape=jax.ShapeDtypeStruct((g, bsz, kk, nchunks), BF16),
        scratch_shapes=[pltpu.VMEM((m, kk), BF16), pltpu.VMEM((m, 2 * ns2), F32), pltpu.VMEM((m, ns2), F32)],
        compiler_params=_params("arbitrary"),
        name="s5_core",
    )(at, toep, bst, cst, avec)


def _s5_tables(a_re, a_im, b_re, b_im, c_re, c_im, d_skip, log_step):
    step = jnp.exp(log_step)[..., None]
    lr, li = a_re * step, a_im * step
    tau = jnp.arange(CHUNK + 1, dtype=F32)[:, None]
    mag = jnp.exp(tau * lr[..., None, :])
    pw_re = mag * jnp.cos(tau * li[..., None, :])
    pw_im = mag * jnp.sin(tau * li[..., None, :])
    num_re, num_im = pw_re[..., 1, :] - 1.0, pw_im[..., 1, :]
    den = a_re * a_re + a_im * a_im
    cf_re = (num_re * a_re + num_im * a_im) / den
    cf_im = (num_im * a_re - num_re * a_im) / den
    bt_re, bt_im = jnp.swapaxes(b_re, -1, -2), jnp.swapaxes(b_im, -1, -2)
    bb_re = cf_re[..., None, :] * bt_re - cf_im[..., None, :] * bt_im
    bb_im = cf_re[..., None, :] * bt_im + cf_im[..., None, :] * bt_re
    e_re = pw_re[..., None, :] * bb_re[..., None, :, :] - pw_im[..., None, :] * bb_im[..., None, :, :]
    e_im = pw_re[..., None, :] * bb_im[..., None, :, :] + pw_im[..., None, :] * bb_re[..., None, :, :]
    cn_re, cn_im = jnp.moveaxis(c_re, -1, 2), jnp.moveaxis(c_im, -1, 2)
    en_re, en_im = jnp.moveaxis(e_re, -1, 2), jnp.moveaxis(e_im, -1, 2)
    kern = jnp.sum(cn_re[:, :, :, None, :, None] * en_re[:, :, :, :, None, :]
                   - cn_im[:, :, :, None, :, None] * en_im[:, :, :, :, None, :], axis=2)
    eye = jnp.eye(SSM_GROUP, dtype=F32)
    kern = kern.at[:, :, 0].add(d_skip[..., :, None] * eye)
    idx = jnp.arange(CHUNK)
    lag = idx[None, :] - idx[:, None]
    blk = jnp.where((lag >= 0)[..., None, None], kern[:, :, jnp.clip(lag, 0)], 0.0)
    dp, g = a_re.shape[:2]
    kk = CHUNK * SSM_GROUP
    toep = blk.transpose(0, 1, 2, 5, 3, 4).reshape(dp, g, kk, kk)
    back = CHUNK - 1 - idx
    bs_re = e_re[:, :, back].reshape(dp, g, kk, SSM_STATE)
    bs_im = e_im[:, :, back].reshape(dp, g, kk, SSM_STATE)
    bst = jnp.concatenate([bs_re, bs_im, bs_im, bs_re], axis=-1)
    fw_re, fw_im = pw_re[:, :, 1:], pw_im[:, :, 1:]
    ca_re = c_re[:, :, None] * fw_re[..., None, :] - c_im[:, :, None] * fw_im[..., None, :]
    ca_im = c_re[:, :, None] * fw_im[..., None, :] + c_im[:, :, None] * fw_re[..., None, :]
    cst = jnp.concatenate([ca_re.transpose(0, 1, 4, 2, 3).reshape(dp, g, SSM_STATE, kk),
                           -ca_im.transpose(0, 1, 4, 2, 3).reshape(dp, g, SSM_STATE, kk)], axis=2)
    ar, ai = pw_re[:, :, CHUNK], pw_im[:, :, CHUNK]
    avec = jnp.stack([jnp.concatenate([ar, ar], -1), jnp.concatenate([-ai, ai], -1),
                      jnp.concatenate([ai, -ai], -1), jnp.zeros_like(jnp.concatenate([ar, ar], -1))], axis=2)
    return toep.astype(BF16), bst.astype(BF16), cst.astype(BF16), avec


def _glu_kernel(gt_ref, zs_ref, w_ref, b_ref, o_ref, *, width, nc):
    gt = gt_ref[...]
    gy = jnp.concatenate([gt[:, t * SSM_GROUP:(t + 1) * SSM_GROUP, :].reshape(width, nc).T for t in range(2)],
                         axis=0)
    for lo in range(0, width, PAIR):
        a = jnp.dot(gy, w_ref[:, lo:lo + PAIR], preferred_element_type=F32) + b_ref[:, lo:lo + PAIR]
        g = jnp.dot(gy, w_ref[:, width + lo:width + lo + PAIR], preferred_element_type=F32)
        out = a * jax.nn.sigmoid(g + b_ref[:, width + lo:width + lo + PAIR])
        for t in range(2):
            z = zs_ref[t, :, lo:lo + PAIR].astype(F32)
            o_ref[t, :, lo:lo + PAIR] = (out[t * nc:(t + 1) * nc, :] * (z * jax.nn.sigmoid(z))).astype(BF16)


def _glu(gt, proj, w_glu_b, b_glu, layer):
    groups, bsz, kk, nc = gt.shape
    width = groups * SSM_GROUP
    return pl.pallas_call(
        functools.partial(_glu_kernel, width=width, nc=nc),
        grid=(bsz, CHUNK // 2),
        in_specs=[pl.BlockSpec((groups, None, 2 * SSM_GROUP, nc), lambda b, ip: (0, b, ip, 0)),
                  pl.BlockSpec((None, None, 2, nc, width), lambda b, ip: (0, b, ip, 0, 0)),
                  pl.BlockSpec((None, width, 2 * width), lambda b, ip: (layer, 0, 0)),
                  pl.BlockSpec((None, 1, 2 * width), lambda b, ip: (layer, 0, 0))],
        out_specs=pl.BlockSpec((None, 2, nc, width), lambda b, ip: (b, ip, 0, 0)),
        out_shape=jax.ShapeDtypeStruct((bsz, CHUNK, nc, width), BF16),
        compiler_params=_params("arbitrary", "arbitrary"),
        name="s5_glu",
    )(gt, proj, w_glu_b, b_glu)


def _attn_kernel(linit_ref, q_ref, k_ref, v_ref, za_ref, lq1_ref, lk1_ref, lq2_ref, lk2_ref, sg_ref,
                 o_ref, km_ref, vt_ref, *scratch, tq, tiles):
    step = pl.program_id(2)

    @pl.when(step == 0)
    def _():
        kk = _natural(k_ref[...])
        grp = (lax.broadcasted_iota(jnp.int32, kk.shape, 1) % LANES) // (HEAD_DIM // 2)
        for combo in range(4):
            km_ref[combo] = jnp.where(grp == combo, kk, jnp.zeros_like(kk))
        vt = _natural(v_ref[...]).astype(F32).T.astype(BF16)
        for hp in range(2):
            vt_ref[hp, :LANES, :] = vt[hp * LANES:(hp + 1) * LANES]
            vt_ref[hp, LANES:, :] = jnp.ones((ONES_ROWS, vt.shape[1]), BF16)

    tc = tq // CHUNK
    for u in range(tiles):
        rows = slice(u * tc, (u + 1) * tc)
        _attn_tile(step * tiles + u, linit_ref, q_ref.at[:, rows, :], za_ref.at[:, rows, :], lq1_ref, lk1_ref,
                   lq2_ref, lk2_ref, sg_ref, o_ref.at[:, rows, :], km_ref, vt_ref, *scratch, tq=tq,
                   may_be_first=(u == 0))


def _attn_tile(qi, linit_ref, q_ref, za_ref, lq1_ref, lk1_ref, lq2_ref, lk2_ref, sg_ref, o_ref, km_ref, vt_ref,
               qt_ref, m_ref, l_ref, acc_ref, s0_ref, s1_ref, p0_ref, p1_ref, a0_ref, a1_ref, c0_ref, c1_ref,
               *, tq, may_be_first):
    nq = tq // TK
    qt_ref[...] = _natural(q_ref[...]).astype(F32).T.astype(BF16)
    m_ref[...] = jnp.full(m_ref.shape, -jnp.inf, F32)
    l_ref[...] = jnp.zeros(l_ref.shape, F32)
    acc_ref[...] = jnp.zeros(acc_ref.shape, F32)

    def scores(kt, s_ref, cm_ref, c0):
        ks = pl.multiple_of(kt * TK, TK)
        qt = qt_ref[:, c0:]
        for combo in range(4):
            s = jnp.dot(km_ref[combo, pl.ds(ks, TK), :], qt, preferred_element_type=F32)
            for c in range(c0, tq, LANES):
                s_ref[combo, c // LANES] = s[:, c - c0:c - c0 + LANES]
            cm_ref[combo, :, c0:] = jnp.max(s, axis=0, keepdims=True)

    def softmax(s_ref, cm_ref, p_ref, a_ref, c0, shift):
        for combo in range(4):
            for c in range(c0, tq, LANES):
                cs = slice(c, c + LANES)
                if shift is not None:
                    row = lax.broadcasted_iota(jnp.int32, (TK, LANES), 0)
                    col = lax.broadcasted_iota(jnp.int32, (TK, LANES), 1)
                    s = jnp.where(row + (shift - c) <= col, s_ref[combo, c // LANES], -jnp.inf)
                    cmax = jnp.max(s, axis=0, keepdims=True)
                else:
                    s = s_ref[combo, c // LANES]
                    cmax = cm_ref[combo, :, cs]
                m_prev = m_ref[combo, :, cs]
                m_new = jnp.maximum(m_prev, cmax)
                alpha = jnp.exp2(m_prev - m_new)
                p = jnp.exp2(s - m_new)
                p_ref[combo, c // LANES] = p.astype(BF16)
                a_ref[combo, :, cs] = alpha
                m_ref[combo, :, cs] = m_new

    def values(kt, p_ref, a_ref, c0):
        ks = pl.multiple_of(kt * TK, TK)
        cs = slice(c0, tq)
        for combo in range(4):
            hp = combo // 2
            p = jnp.concatenate([p_ref[combo, c // LANES] for c in range(c0, tq, LANES)], axis=1)
            pv = jnp.dot(vt_ref[hp, :, pl.ds(ks, TK)], p, preferred_element_type=F32)
            alpha = a_ref[combo, :, cs]
            acc_ref[combo, :, cs] = alpha * acc_ref[combo, :, cs] + pv[:LANES]
            l_ref[combo, :, cs] = alpha * l_ref[combo, :, cs] + pv[LANES:LANES + 1]

    kd = qi * nq

    def band():
        scores(kd + 1, s1_ref, c1_ref, TK)
        softmax(s0_ref, c0_ref, p0_ref, a0_ref, 0, 0)
        values(kd, p0_ref, a0_ref, 0)
        softmax(s1_ref, c1_ref, p1_ref, a1_ref, TK, TK)
        values(kd + 1, p1_ref, a1_ref, TK)

    def pipelined():
        scores(0, s0_ref, c0_ref, 0)
        scores(1, s1_ref, c1_ref, 0)
        softmax(s0_ref, c0_ref, p0_ref, a0_ref, 0, None)

        def pair(i, carry):
            kt = 2 * i
            scores(kt + 2, s0_ref, c0_ref, 0)
            softmax(s1_ref, c1_ref, p1_ref, a1_ref, 0, None)
            values(kt, p0_ref, a0_ref, 0)
            scores(kt + 3, s1_ref, c1_ref, 0)
            softmax(s0_ref, c0_ref, p0_ref, a0_ref, 0, None)
            values(kt + 1, p1_ref, a1_ref, 0)
            return carry

        lax.fori_loop(0, qi - 1, pair, 0)
        softmax(s1_ref, c1_ref, p1_ref, a1_ref, 0, None)
        values(kd - 2, p0_ref, a0_ref, 0)
        values(kd - 1, p1_ref, a1_ref, 0)
        scores(kd, s0_ref, c0_ref, 0)
        band()

    if may_be_first:
        @pl.when(qi == 0)
        def _():
            scores(0, s0_ref, c0_ref, 0)
            band()

        pl.when(qi > 0)(pipelined)
    else:
        pipelined()

    linit = linit_ref[0]
    lam = (jnp.exp(jnp.sum(lq1_ref[...] * lk1_ref[...], axis=-1, keepdims=True))
           - jnp.exp(jnp.sum(lq2_ref[...] * lk2_ref[...], axis=-1, keepdims=True)) + linit)
    za = _natural(za_ref[...]).astype(F32)
    for hp in range(2):
        r1 = 1.0 / l_ref[2 * hp]
        r2 = lam / l_ref[2 * hp + 1]
        ot = acc_ref[2 * hp] * r1 - acc_ref[2 * hp + 1] * r2
        o = ot.T
        y = o * lax.rsqrt(jnp.mean(o * o, axis=-1, keepdims=True) + NORM_EPS) * sg_ref[...]
        y = y * (1.0 - linit)
        z = za[:, hp * LANES:(hp + 1) * LANES]
        y = y * (z * jax.nn.sigmoid(z))
        o_ref[:, :, hp * LANES:(hp + 1) * LANES] = jnp.swapaxes(
            y.reshape(tq // CHUNK, CHUNK, LANES), 0, 1).astype(BF16)


def _natural(v):
    return jnp.swapaxes(v, 0, 1).reshape(v.shape[0] * v.shape[1], v.shape[2])


def _attention(proj, lam_q1, lam_k1, lam_q2, lam_k2, sub_g, layer, lambda_init, tq):
    _, bsz, _, nc, width = proj.shape
    s = nc * CHUNK
    tiles = ATTN_TILES_PER_STEP
    tc = tiles * tq // CHUNK
    npairs = width // PAIR
    assert tq == 2 * TK and s % (tiles * tq) == 0
    vec = lambda: pl.BlockSpec((None, 1, HEAD_DIM), lambda b, j, i: (layer, 0, 0))
    sbuf = lambda: pltpu.VMEM((4, tq // LANES, TK, LANES), F32)
    pbuf = lambda: pltpu.VMEM((4, tq // LANES, TK, LANES), BF16)
    rowv = lambda: pltpu.VMEM((4, 1, tq), F32)
    return pl.pallas_call(
        functools.partial(_attn_kernel, tq=tq, tiles=tiles),
        grid=(bsz, npairs, s // (tiles * tq)),
        in_specs=[pl.BlockSpec(memory_space=pltpu.SMEM),
                  pl.BlockSpec((None, None, CHUNK, tc, PAIR), lambda b, j, i: (1, b, 0, i, j)),
                  pl.BlockSpec((None, None, CHUNK, nc, PAIR), lambda b, j, i: (2, b, 0, 0, j)),
                  pl.BlockSpec((None, None, CHUNK, nc, PAIR), lambda b, j, i: (3, b, 0, 0, j)),
                  pl.BlockSpec((None, None, CHUNK, tc, PAIR), lambda b, j, i: (4, b, 0, i, j)),
                  vec(), vec(), vec(), vec(),
                  pl.BlockSpec((None, 1, 2 * HEAD_DIM), lambda b, j, i: (layer, 0, 0))],
        out_specs=pl.BlockSpec((None, CHUNK, tc, PAIR), lambda b, j, i: (b, 0, i, j)),
        out_shape=jax.ShapeDtypeStruct((bsz, CHUNK, nc, width), BF16),
        scratch_shapes=[pltpu.VMEM((4, s, PAIR), BF16), pltpu.VMEM((2, LANES + ONES_ROWS, s), BF16),
                        pltpu.VMEM((PAIR, tq), BF16),
                        rowv(), rowv(), pltpu.VMEM((4, LANES, tq), F32),
                        sbuf(), sbuf(), pbuf(), pbuf(), rowv(), rowv(), rowv(), rowv()],
        compiler_params=_params("arbitrary", "arbitrary", "arbitrary"),
        name="diff_attn",
    )(jnp.full((1,), lambda_init, F32), proj, proj, proj, proj, lam_q1, lam_k1, lam_q2, lam_k2, sub_g)


def _out_proj_kernel(ys_ref, ya_ref, ws_ref, wa_ref, x_ref, gate_ref, fg_ref, o_ref, *, final):
    pb, cb, width = ys_ref.shape
    rows = pb * cb
    y = jnp.dot(ys_ref[...].reshape(rows, width), ws_ref[...], preferred_element_type=F32)
    y = y + jnp.dot(ya_ref[...].reshape(rows, width), wa_ref[...], preferred_element_type=F32)
    xn = x_ref[...].reshape(rows, -1) + gate_ref[...] * y
    if final:
        xn = xn * lax.rsqrt(jnp.mean(xn * xn, axis=-1, keepdims=True) + NORM_EPS) * fg_ref[...]
        o_ref[...] = _natural(xn.reshape(pb, cb, -1))
    else:
        o_ref[...] = xn.reshape(pb, cb, -1)


def _out_proj(ys, ya, w_out_b, x, gate, final_g, layer, final):
    bsz, _, nc, d = x.shape
    width = ys.shape[-1]
    if final:
        pb, cb = CHUNK, TOKEN_TILE // CHUNK
        out_spec = pl.BlockSpec((None, TOKEN_TILE, d), lambda b, p, m: (b, m, 0))
        out_shape = jax.ShapeDtypeStruct((bsz, nc * CHUNK, d), F32)
    else:
        pb, cb = TOKEN_TILE // nc, nc
        out_spec = pl.BlockSpec((None, pb, cb, d), lambda b, p, m: (b, p, m, 0))
        out_shape = jax.ShapeDtypeStruct(x.shape, F32)
    tile = lambda last: pl.BlockSpec((None, pb, cb, last), lambda b, p, m: (b, p, m, 0))
    return pl.pallas_call(
        functools.partial(_out_proj_kernel, final=final),
        grid=(bsz, CHUNK // pb, nc // cb),
        in_specs=[tile(width), tile(width),
                  pl.BlockSpec((None, width, d), lambda b, p, m: (layer, 0, 0)),
                  pl.BlockSpec((None, width, d), lambda b, p, m: (layer, 1, 0)),
                  tile(d),
                  pl.BlockSpec((None, 1, d), lambda b, p, m: (b, 0, 0)),
                  pl.BlockSpec((1, d), lambda b, p, m: (0, 0))],
        out_specs=out_spec,
        out_shape=out_shape,
        compiler_params=_params("arbitrary", "arbitrary", "arbitrary"),
        name="out_proj",
    )(ys, ya, w_out_b, w_out_b, x, gate, final_g)


def _to_phase_major_kernel(x_ref, o_ref):
    rows, d = x_ref.shape
    o_ref[...] = jnp.swapaxes(x_ref[...].reshape(rows // CHUNK, CHUNK, d), 0, 1)


def _to_phase_major(x):
    bsz, s, d = x.shape
    tc = TOKEN_TILE // CHUNK
    return pl.pallas_call(
        _to_phase_major_kernel,
        grid=(bsz, s // TOKEN_TILE),
        in_specs=[pl.BlockSpec((None, TOKEN_TILE, d), lambda b, m: (b, m, 0))],
        out_specs=pl.BlockSpec((None, CHUNK, tc, d), lambda b, m: (b, 0, m, 0)),
        out_shape=jax.ShapeDtypeStruct((bsz, CHUNK, s // CHUNK, d), F32),
        compiler_params=_params("arbitrary", "arbitrary"),
        name="to_phase_major",
    )(x)


def _qk_relayout(w):
    half = HEAD_DIM // 2
    lead = w.shape[:-1]
    w = w.reshape(*lead, -1, 2, 2, 2, half)
    nd = len(lead)
    w = w.transpose(*range(nd), nd, nd + 3, nd + 1, nd + 2, nd + 4)
    return w.reshape(*lead, -1)


def _tile(n, target):
    t = min(n, target)
    while n % t or t % LANES:
        t -= LANES
    return t


def kernel(x, c, positions, norm_g, w_ada, b_ada, w_in, w_out, ssm_a_re, ssm_a_im, ssm_b_re, ssm_b_im,
           ssm_c_re, ssm_c_im, ssm_d, ssm_log_step, w_glu, b_glu, lam_q1, lam_k1, lam_q2, lam_k2,
           sub_g, final_g):
    bsz, s, d = x.shape
    depth = w_in.shape[0]
    width = d // 2
    nchunks = s // CHUNK
    assert w_in.shape[-1] == 6 * width and width % PAIR == 0 and bsz % SUBLANES == 0
    assert s % TOKEN_TILE == 0 and TOKEN_TILE % nchunks == 0

    w_in_p = jnp.concatenate(
        [w_in[:, :, width:2 * width].astype(BF16), _qk_relayout(w_in[:, :, 2 * width:3 * width]).astype(BF16),
         _qk_relayout(w_in[:, :, 3 * width:4 * width]).astype(BF16), w_in[:, :, 4 * width:].astype(BF16)], axis=-1)
    w_ut = jnp.swapaxes(lax.optimization_barrier(w_in[:, :, :width]), 1, 2).astype(BF16)
    w_out_b = w_out.astype(BF16)
    w_glu_b = w_glu.astype(BF16)
    toep, bst, cst, avec = _s5_tables(ssm_a_re, ssm_a_im, ssm_b_re, ssm_b_im, ssm_c_re, ssm_c_im,
                                      ssm_d, ssm_log_step)
    half = HEAD_DIM // 2
    inv_freq = jnp.tile(ROPE_THETA ** (-jnp.arange(half, dtype=F32) / half), LANES // half)
    pos_pm = jnp.swapaxes(positions.reshape(bsz, nchunks, CHUNK), 1, 2)
    ang = pos_pm.astype(F32)[..., None] * inv_freq
    cos_t, sin_t = jnp.cos(ang), jnp.sin(ang)

    mod = _ada_mod(c, w_ada, b_ada).reshape(depth, bsz, 1, 3 * d)
    norm_g3 = norm_g.reshape(depth, 1, d)
    b_glu3 = b_glu.reshape(depth, 1, 2 * width)
    lam3 = [v.reshape(depth, 1, HEAD_DIM) for v in (lam_q1, lam_k1, lam_q2, lam_k2)]
    sub_g3 = sub_g.reshape(depth, 1, 2 * HEAD_DIM)
    final_g2 = final_g.reshape(1, d)

    x = _to_phase_major(x)
    for l in range(depth):
        lambda_init = 0.8 - 0.6 * math.exp(-0.3 * l)
        shift, scale, gate = mod[l, :, :, :d], mod[l, :, :, d:2 * d], mod[l, :, :, 2 * d:]
        at, proj = _in_proj(x, shift, scale, norm_g3, w_in_p, w_ut, l, cos_t, sin_t)
        gt = _s5_core(at, toep, bst, cst, avec, l)
        ys = _glu(gt, proj, w_glu_b, b_glu3, l)
        ya = _attention(proj, *lam3, sub_g3, l, lambda_init, TOKEN_TILE)
        x = _out_proj(ys, ya, w_out_b, x, gate, final_g2, l, l == depth - 1)
    return x
```

```python
import functools
import math

import jax
import jax.numpy as jnp
from jax import lax
from jax.experimental import pallas as pl
from jax.experimental.pallas import tpu as pltpu

LANES = 128
SUBLANES = 8
V7X_VMEM_REQUEST_BYTES = 60 * 1024 * 1024

SSM_GROUP = 16
SSM_STATE = 64
HEAD_DIM = 64
ROPE_THETA = 10000.0
NORM_EPS = 1e-6
CHUNK = 16
PAIR = 4 * HEAD_DIM
TK = 256
TOKEN_TILE = 512
ONES_ROWS = 16
IN_PROJ_PHASES = 4
ATTN_TILES_PER_STEP = 4
Q_SCALE = HEAD_DIM ** -0.5 * math.log2(math.e)

F32 = jnp.float32
BF16 = jnp.bfloat16


def _params(*sem):
    return pltpu.CompilerParams(dimension_semantics=sem, vmem_limit_bytes=V7X_VMEM_REQUEST_BYTES)


def _ada_kernel(c_ref, w_ref, b_ref, o_ref):
    c = c_ref[...]
    act = c * jax.nn.sigmoid(c)
    o_ref[...] = jnp.dot(act, w_ref[...], preferred_element_type=F32,
                         precision=lax.Precision.HIGHEST) + b_ref[...]


def _ada_mod(c, w_ada, b_ada):
    depth, d, n3 = w_ada.shape
    bsz = c.shape[0]
    tn = _tile(n3, 1024)
    return pl.pallas_call(
        _ada_kernel,
        grid=(depth, n3 // tn),
        in_specs=[pl.BlockSpec((bsz, d), lambda l, n: (0, 0)),
                  pl.BlockSpec((None, d, tn), lambda l, n: (l, 0, n)),
                  pl.BlockSpec((None, 1, tn), lambda l, n: (l, 0, n))],
        out_specs=pl.BlockSpec((None, bsz, tn), lambda l, n: (l, 0, n)),
        out_shape=jax.ShapeDtypeStruct((depth, bsz, n3), F32),
        compiler_params=_params("arbitrary", "arbitrary"),
        name="ada_mod",
    )(c, w_ada, b_ada.reshape(depth, 1, n3))


def _in_proj_kernel(x_ref, shift_ref, scale_ref, g_ref, w_ref, wut_ref, cos_ref, sin_ref, at_ref, o_ref, h_ref,
                    *, npairs, nc, ph):
    n = pl.program_id(1)
    ip = pl.program_id(2)
    width = w_ref.shape[-1]

    @pl.when(n == 0)
    def _():
        for t in range(ph):
            xf = x_ref[t]
            y = xf * lax.rsqrt(jnp.mean(xf * xf, axis=-1, keepdims=True) + NORM_EPS) * g_ref[...]
            h = (y * (1.0 + scale_ref[...]) + shift_ref[...]).astype(BF16)
            h_ref[ip, t * nc:(t + 1) * nc, :] = h
            ut = lax.dot_general(wut_ref[...], h, (((1,), (1,)), ((), ())), preferred_element_type=F32)
            at_ref[:, t * SSM_GROUP:(t + 1) * SSM_GROUP, :] = ut.reshape(-1, SSM_GROUP, nc).astype(BF16)

    @pl.when(n > 0)
    def _():
        is_rope = jnp.logical_or(n == 2, n == 3)
        qs = jnp.where(n == 2, Q_SCALE, 1.0).astype(F32)
        cs = [jnp.where(is_rope, cos_ref[t] * qs, 1.0) for t in range(ph)]
        sn = [jnp.where(is_rope, sin_ref[t] * qs, 0.0) for t in range(ph)]
        h = h_ref[ip]
        for j in range(npairs):
            lo = j * PAIR
            acc = jnp.dot(h, w_ref[:, lo:lo + PAIR], preferred_element_type=F32)
            for t in range(ph):
                t1 = acc[t * nc:(t + 1) * nc, :LANES]
                t2 = acc[t * nc:(t + 1) * nc, LANES:]
                o_ref[t, :, lo:lo + LANES] = (t1 * cs[t] - t2 * sn[t]).astype(BF16)
                o_ref[t, :, lo + LANES:lo + PAIR] = (t2 * cs[t] + t1 * sn[t]).astype(BF16)


def _in_proj(x, shift, scale, norm_g, w_in_p, w_ut, layer, cos_t, sin_t):
    bsz, _, nc, d = x.shape
    width = w_in_p.shape[-1] // 5
    npairs = width // PAIR
    groups = width // SSM_GROUP
    ph = IN_PROJ_PHASES
    nip = CHUNK // ph
    seg = lambda n: jnp.maximum(n, 1)
    u_step = lambda n, ip: jnp.where(n == 0, ip, nip - 1)
    o_step = lambda n, ip: jnp.where(n == 0, 0, ip)
    return pl.pallas_call(
        functools.partial(_in_proj_kernel, npairs=npairs, nc=nc, ph=ph),
        grid=(bsz, 6, nip),
        in_specs=[pl.BlockSpec((None, ph, nc, d), lambda b, n, ip: (b, u_step(n, ip), 0, 0)),
                  pl.BlockSpec((None, 1, d), lambda b, n, ip: (b, 0, 0)),
                  pl.BlockSpec((None, 1, d), lambda b, n, ip: (b, 0, 0)),
                  pl.BlockSpec((None, 1, d), lambda b, n, ip: (layer, 0, 0)),
                  pl.BlockSpec((None, d, width), lambda b, n, ip: (layer, 0, seg(n) - 1)),
                  pl.BlockSpec((None, width, d), lambda b, n, ip: (layer, 0, 0), pipeline_mode=pl.Buffered(1)),
                  pl.BlockSpec((None, ph, nc, LANES), lambda b, n, ip: (b, ip, 0, 0)),
                  pl.BlockSpec((None, ph, nc, LANES), lambda b, n, ip: (b, ip, 0, 0))],
        out_specs=[pl.BlockSpec((groups, None, ph * SSM_GROUP, nc), lambda b, n, ip: (0, b, u_step(n, ip), 0)),
                   pl.BlockSpec((None, None, ph, nc, width), lambda b, n, ip: (seg(n) - 1, b, o_step(n, ip), 0, 0))],
        out_shape=[jax.ShapeDtypeStruct((groups, bsz, CHUNK * SSM_GROUP, nc), BF16),
                   jax.ShapeDtypeStruct((5, bsz, CHUNK, nc, width), BF16)],
        scratch_shapes=[pltpu.VMEM((nip, ph * nc, d), BF16)],
        compiler_params=_params("arbitrary", "arbitrary", "arbitrary"),
        name="in_proj",
    )(x, shift, scale, norm_g, w_in_p, w_ut, cos_t, sin_t)


def _gelu_tanh(y):
    return 0.5 * y * (1.0 + jnp.tanh(math.sqrt(2.0 / math.pi) * (y + 0.044715 * (y * y * y))))


def _s5_kernel(at_ref, toep_ref, bst_ref, cst_ref, av_ref, gt_ref, a_ref, sb_ref, xp_ref, *, bsz, nchunks):
    ns2 = 2 * SSM_STATE
    kk = CHUNK * SSM_GROUP
    for b in range(bsz):
        a_ref[b * nchunks:(b + 1) * nchunks, :] = at_ref[b].T
    a = a_ref[...]
    sb = jnp.dot(a, bst_ref[...], preferred_element_type=F32)
    sb_ref[...] = jnp.swapaxes(sb.reshape(bsz, nchunks, 2 * ns2), 0, 1).reshape(nchunks * bsz, 2 * ns2)
    ar = av_ref[0:1, :]
    ai1 = av_ref[1:2, :]
    ai2 = av_ref[2:3, :]

    def body(c, carry):
        xs, xw = carry
        r = pl.multiple_of(c * bsz, bsz)
        xp_ref[pl.ds(r, bsz), :] = xs
        sb = sb_ref[pl.ds(r, bsz), :]
        return (ar * xs + ai1 * xw + sb[:, :ns2], ar * xw + ai2 * xs + sb[:, ns2:])

    zero = jnp.zeros((bsz, ns2), F32)
    lax.fori_loop(0, nchunks, body, (zero, zero), unroll=8)
    xp = jnp.swapaxes(xp_ref[...].reshape(nchunks, bsz, ns2), 0, 1).reshape(bsz * nchunks, ns2)
    y = jnp.dot(a, toep_ref[...], preferred_element_type=F32)
    y = y + jnp.dot(xp.astype(BF16), cst_ref[...], preferred_element_type=F32)
    gy = _gelu_tanh(y).astype(BF16)
    for b in range(bsz):
        gt_ref[b] = gy[b * nchunks:(b + 1) * nchunks, :].T


def _s5_core(at, toep, bst, cst, avec, layer):
    g, bsz, kk, nchunks = at.shape
    ns2 = 2 * SSM_STATE
    m = bsz * nchunks
    return pl.pallas_call(
        functools.partial(_s5_kernel, bsz=bsz, nchunks=nchunks),
        grid=(g,),
        in_specs=[pl.BlockSpec((None, bsz, kk, nchunks), lambda i: (i, 0, 0, 0)),
                  pl.BlockSpec((None, None, kk, kk), lambda i: (layer, i, 0, 0)),
                  pl.BlockSpec((None, None, kk, 2 * ns2), lambda i: (layer, i, 0, 0)),
                  pl.BlockSpec((None, None, ns2, kk), lambda i: (layer, i, 0, 0)),
                  pl.BlockSpec((None, None, 4, ns2), lambda i: (layer, i, 0, 0))],
        out_specs=pl.BlockSpec((None, bsz, kk, nchunks), lambda i: (i, 0, 0, 0)),
        out_shape=jax.ShapeDtypeStruct((g, bsz, kk, nchunks), BF16),
        scratch_shapes=[pltpu.VMEM((m, kk), BF16), pltpu.VMEM((m, 2 * ns2), F32), pltpu.VMEM((m, ns2), F32)],
        compiler_params=_params("arbitrary"),
        name="s5_core",
    )(at, toep, bst, cst, avec)


def _s5_tables(a_re, a_im, b_re, b_im, c_re, c_im, d_skip, log_step):
    step = jnp.exp(log_step)[..., None]
    lr, li = a_re * step, a_im * step
    tau = jnp.arange(CHUNK + 1, dtype=F32)[:, None]
    mag = jnp.exp(tau * lr[..., None, :])
    pw_re = mag * jnp.cos(tau * li[..., None, :])
    pw_im = mag * jnp.sin(tau * li[..., None, :])
    num_re, num_im = pw_re[..., 1, :] - 1.0, pw_im[..., 1, :]
    den = a_re * a_re + a_im * a_im
    cf_re = (num_re * a_re + num_im * a_im) / den
    cf_im = (num_im * a_re - num_re * a_im) / den
    bt_re, bt_im = jnp.swapaxes(b_re, -1, -2), jnp.swapaxes(b_im, -1, -2)
    bb_re = cf_re[..., None, :] * bt_re - cf_im[..., None, :] * bt_im
    bb_im = cf_re[..., None, :] * bt_im + cf_im[..., None, :] * bt_re
    e_re = pw_re[..., None, :] * bb_re[..., None, :, :] - pw_im[..., None, :] * bb_im[..., None, :, :]
    e_im = pw_re[..., None, :] * bb_im[..., None, :, :] + pw_im[..., None, :] * bb_re[..., None, :, :]
    cn_re, cn_im = jnp.moveaxis(c_re, -1, 2), jnp.moveaxis(c_im, -1, 2)
    en_re, en_im = jnp.moveaxis(e_re, -1, 2), jnp.moveaxis(e_im, -1, 2)
    kern = jnp.sum(cn_re[:, :, :, None, :, None] * en_re[:, :, :, :, None, :]
                   - cn_im[:, :, :, None, :, None] * en_im[:, :, :, :, None, :], axis=2)
    eye = jnp.eye(SSM_GROUP, dtype=F32)
    kern = kern.at[:, :, 0].add(d_skip[..., :, None] * eye)
    idx = jnp.arange(CHUNK)
    lag = idx[None, :] - idx[:, None]
    blk = jnp.where((lag >= 0)[..., None, None], kern[:, :, jnp.clip(lag, 0)], 0.0)
    dp, g = a_re.shape[:2]
    kk = CHUNK * SSM_GROUP
    toep = blk.transpose(0, 1, 2, 5, 3, 4).reshape(dp, g, kk, kk)
    back = CHUNK - 1 - idx
    bs_re = e_re[:, :, back].reshape(dp, g, kk, SSM_STATE)
    bs_im = e_im[:, :, back].reshape(dp, g, kk, SSM_STATE)
    bst = jnp.concatenate([bs_re, bs_im, bs_im, bs_re], axis=-1)
    fw_re, fw_im = pw_re[:, :, 1:], pw_im[:, :, 1:]
    ca_re = c_re[:, :, None] * fw_re[..., None, :] - c_im[:, :, None] * fw_im[..., None, :]
    ca_im = c_re[:, :, None] * fw_im[..., None, :] + c_im[:, :, None] * fw_re[..., None, :]
    cst = jnp.concatenate([ca_re.transpose(0, 1, 4, 2, 3).reshape(dp, g, SSM_STATE, kk),
                           -ca_im.transpose(0, 1, 4, 2, 3).reshape(dp, g, SSM_STATE, kk)], axis=2)
    ar, ai = pw_re[:, :, CHUNK], pw_im[:, :, CHUNK]
    avec = jnp.stack([jnp.concatenate([ar, ar], -1), jnp.concatenate([-ai, ai], -1),
                      jnp.concatenate([ai, -ai], -1), jnp.zeros_like(jnp.concatenate([ar, ar], -1))], axis=2)
    return toep.astype(BF16), bst.astype(BF16), cst.astype(BF16), avec


def _glu_kernel(gt_ref, zs_ref, w_ref, b_ref, o_ref, *, width, nc):
    gt = gt_ref[...]
    gy = jnp.concatenate([gt[:, t * SSM_GROUP:(t + 1) * SSM_GROUP, :].reshape(width, nc).T for t in range(2)],
                         axis=0)
    for lo in range(0, width, PAIR):
        a = jnp.dot(gy, w_ref[:, lo:lo + PAIR], preferred_element_type=F32) + b_ref[:, lo:lo + PAIR]
        g = jnp.dot(gy, w_ref[:, width + lo:width + lo + PAIR], preferred_element_type=F32)
        out = a * jax.nn.sigmoid(g + b_ref[:, width + lo:width + lo + PAIR])
        for t in range(2):
            z = zs_ref[t, :, lo:lo + PAIR].astype(F32)
            o_ref[t, :, lo:lo + PAIR] = (out[t * nc:(t + 1) * nc, :] * (z * jax.nn.sigmoid(z))).astype(BF16)


def _glu(gt, proj, w_glu_b, b_glu, layer):
    groups, bsz, kk, nc = gt.shape
    width = groups * SSM_GROUP
    return pl.pallas_call(
        functools.partial(_glu_kernel, width=width, nc=nc),
        grid=(bsz, CHUNK // 2),
        in_specs=[pl.BlockSpec((groups, None, 2 * SSM_GROUP, nc), lambda b, ip: (0, b, ip, 0)),
                  pl.BlockSpec((None, None, 2, nc, width), lambda b, ip: (0, b, ip, 0, 0)),
                  pl.BlockSpec((None, width, 2 * width), lambda b, ip: (layer, 0, 0)),
                  pl.BlockSpec((None, 1, 2 * width), lambda b, ip: (layer, 0, 0))],
        out_specs=pl.BlockSpec((None, 2, nc, width), lambda b, ip: (b, ip, 0, 0)),
        out_shape=jax.ShapeDtypeStruct((bsz, CHUNK, nc, width), BF16),
        compiler_params=_params("arbitrary", "arbitrary"),
        name="s5_glu",
    )(gt, proj, w_glu_b, b_glu)


def _attn_kernel(linit_ref, q_ref, k_ref, v_ref, za_ref, lq1_ref, lk1_ref, lq2_ref, lk2_ref, sg_ref,
                 o_ref, km_ref, vt_ref, *scratch, tq, tiles):
    step = pl.program_id(2)

    @pl.when(step == 0)
    def _():
        kk = _natural(k_ref[...])
        grp = (lax.broadcasted_iota(jnp.int32, kk.shape, 1) % LANES) // (HEAD_DIM // 2)
        for combo in range(4):
            km_ref[combo] = jnp.where(grp == combo, kk, jnp.zeros_like(kk))
        vt = _natural(v_ref[...]).T
        for hp in range(2):
            vt_ref[hp, :LANES, :] = vt[hp * LANES:(hp + 1) * LANES]
            vt_ref[hp, LANES:, :] = jnp.ones((ONES_ROWS, vt.shape[1]), BF16)

    tc = tq // CHUNK
    for u in range(tiles):
        rows = slice(u * tc, (u + 1) * tc)
        _attn_tile(step * tiles + u, linit_ref, q_ref.at[:, rows, :], za_ref.at[:, rows, :], lq1_ref, lk1_ref,
                   lq2_ref, lk2_ref, sg_ref, o_ref.at[:, rows, :], km_ref, vt_ref, *scratch, tq=tq,
                   may_be_first=(u == 0))


def _attn_tile(qi, linit_ref, q_ref, za_ref, lq1_ref, lk1_ref, lq2_ref, lk2_ref, sg_ref, o_ref, km_ref, vt_ref,
               qt_ref, m_ref, l_ref, acc_ref, s0_ref, s1_ref, p0_ref, p1_ref, a0_ref, a1_ref, c0_ref, c1_ref,
               *, tq, may_be_first):
    nq = tq // TK
    qt_ref[...] = _natural(q_ref[...]).T
    m_ref[...] = jnp.full(m_ref.shape, -jnp.inf, F32)
    l_ref[...] = jnp.zeros(l_ref.shape, F32)
    acc_ref[...] = jnp.zeros(acc_ref.shape, F32)

    def scores(kt, s_ref, cm_ref, c0):
        ks = pl.multiple_of(kt * TK, TK)
        qt = qt_ref[:, c0:]
        for combo in range(4):
            s = jnp.dot(km_ref[combo, pl.ds(ks, TK), :], qt, preferred_element_type=F32)
            for c in range(c0, tq, LANES):
                s_ref[combo, c // LANES] = s[:, c - c0:c - c0 + LANES]
            cm_ref[combo, :, c0:] = jnp.max(s, axis=0, keepdims=True)

    def softmax(s_ref, cm_ref, p_ref, a_ref, c0, shift):
        for combo in range(4):
            for c in range(c0, tq, LANES):
                cs = slice(c, c + LANES)
                if shift is not None:
                    row = lax.broadcasted_iota(jnp.int32, (TK, LANES), 0)
                    col = lax.broadcasted_iota(jnp.int32, (TK, LANES), 1)
                    s = jnp.where(row + (shift - c) <= col, s_ref[combo, c // LANES], -jnp.inf)
                    cmax = jnp.max(s, axis=0, keepdims=True)
                else:
                    s = s_ref[combo, c // LANES]
                    cmax = cm_ref[combo, :, cs]
                m_prev = m_ref[combo, :, cs]
                m_new = jnp.maximum(m_prev, cmax)
                alpha = jnp.exp2(m_prev - m_new)
                p = jnp.exp2(s - m_new)
                p_ref[combo, c // LANES] = p.astype(BF16)
                a_ref[combo, :, cs] = alpha
                m_ref[combo, :, cs] = m_new

    def values(kt, p_ref, a_ref, c0):
        ks = pl.multiple_of(kt * TK, TK)
        cs = slice(c0, tq)
        for combo in range(4):
            hp = combo // 2
            p = jnp.concatenate([p_ref[combo, c // LANES] for c in range(c0, tq, LANES)], axis=1)
            pv = jnp.dot(vt_ref[hp, :, pl.ds(ks, TK)], p, preferred_element_type=F32)
            alpha = a_ref[combo, :, cs]
            acc_ref[combo, :, cs] = alpha * acc_ref[combo, :, cs] + pv[:LANES]
            l_ref[combo, :, cs] = alpha * l_ref[combo, :, cs] + pv[LANES:LANES + 1]

    kd = qi * nq

    def band():
        scores(kd + 1, s1_ref, c1_ref, TK)
        softmax(s0_ref, c0_ref, p0_ref, a0_ref, 0, 0)
        values(kd, p0_ref, a0_ref, 0)
        softmax(s1_ref, c1_ref, p1_ref, a1_ref, TK, TK)
        values(kd + 1, p1_ref, a1_ref, TK)

    def pipelined():
        scores(0, s0_ref, c0_ref, 0)
        scores(1, s1_ref, c1_ref, 0)
        softmax(s0_ref, c0_ref, p0_ref, a0_ref, 0, None)

        def pair(i, carry):
            kt = 2 * i
            scores(kt + 2, s0_ref, c0_ref, 0)
            softmax(s1_ref, c1_ref, p1_ref, a1_ref, 0, None)
            values(kt, p0_ref, a0_ref, 0)
            scores(kt + 3, s1_ref, c1_ref, 0)
            softmax(s0_ref, c0_ref, p0_ref, a0_ref, 0, None)
            values(kt + 1, p1_ref, a1_ref, 0)
            return carry

        lax.fori_loop(0, qi - 1, pair, 0)
        softmax(s1_ref, c1_ref, p1_ref, a1_ref, 0, None)
        values(kd - 2, p0_ref, a0_ref, 0)
        values(kd - 1, p1_ref, a1_ref, 0)
        scores(kd, s0_ref, c0_ref, 0)
        band()

    if may_be_first:
        @pl.when(qi == 0)
        def _():
            scores(0, s0_ref, c0_ref, 0)
            band()

        pl.when(qi > 0)(pipelined)
    else:
        pipelined()

    linit = linit_ref[0]
    lam = (jnp.exp(jnp.sum(lq1_ref[...] * lk1_ref[...], axis=-1, keepdims=True))
           - jnp.exp(jnp.sum(lq2_ref[...] * lk2_ref[...], axis=-1, keepdims=True)) + linit)
    za = _natural(za_ref[...]).astype(F32)
    for hp in range(2):
        r1 = 1.0 / l_ref[2 * hp]
        r2 = lam / l_ref[2 * hp + 1]
        ot = acc_ref[2 * hp] * r1 - acc_ref[2 * hp + 1] * r2
        o = ot.T
        y = o * lax.rsqrt(jnp.mean(o * o, axis=-1, keepdims=True) + NORM_EPS) * sg_ref[...]
        y = y * (1.0 - linit)
        z = za[:, hp * LANES:(hp + 1) * LANES]
        y = y * (z * jax.nn.sigmoid(z))
        o_ref[:, :, hp * LANES:(hp + 1) * LANES] = jnp.swapaxes(
            y.reshape(tq // CHUNK, CHUNK, LANES), 0, 1).astype(BF16)


def _natural(v):
    return jnp.swapaxes(v, 0, 1).reshape(v.shape[0] * v.shape[1], v.shape[2])


def _attention(proj, lam_q1, lam_k1, lam_q2, lam_k2, sub_g, layer, lambda_init, tq):
    _, bsz, _, nc, width = proj.shape
    s = nc * CHUNK
    tiles = ATTN_TILES_PER_STEP
    tc = tiles * tq // CHUNK
    npairs = width // PAIR
    assert tq == 2 * TK and s % (tiles * tq) == 0
    vec = lambda: pl.BlockSpec((None, 1, HEAD_DIM), lambda b, j, i: (layer, 0, 0))
    sbuf = lambda: pltpu.VMEM((4, tq // LANES, TK, LANES), F32)
    pbuf = lambda: pltpu.VMEM((4, tq // LANES, TK, LANES), BF16)
    rowv = lambda: pltpu.VMEM((4, 1, tq), F32)
    return pl.pallas_call(
        functools.partial(_attn_kernel, tq=tq, tiles=tiles),
        grid=(bsz, npairs, s // (tiles * tq)),
        in_specs=[pl.BlockSpec(memory_space=pltpu.SMEM),
                  pl.BlockSpec((None, None, CHUNK, tc, PAIR), lambda b, j, i: (1, b, 0, i, j)),
                  pl.BlockSpec((None, None, CHUNK, nc, PAIR), lambda b, j, i: (2, b, 0, 0, j)),
                  pl.BlockSpec((None, None, CHUNK, nc, PAIR), lambda b, j, i: (3, b, 0, 0, j)),
                  pl.BlockSpec((None, None, CHUNK, tc, PAIR), lambda b, j, i: (4, b, 0, i, j)),
                  vec(), vec(), vec(), vec(),
                  pl.BlockSpec((None, 1, 2 * HEAD_DIM), lambda b, j, i: (layer, 0, 0))],
        out_specs=pl.BlockSpec((None, CHUNK, tc, PAIR), lambda b, j, i: (b, 0, i, j)),
        out_shape=jax.ShapeDtypeStruct((bsz, CHUNK, nc, width), BF16),
        scratch_shapes=[pltpu.VMEM((4, s, PAIR), BF16), pltpu.VMEM((2, LANES + ONES_ROWS, s), BF16),
                        pltpu.VMEM((PAIR, tq), BF16),
                        rowv(), rowv(), pltpu.VMEM((4, LANES, tq), F32),
                        sbuf(), sbuf(), pbuf(), pbuf(), rowv(), rowv(), rowv(), rowv()],
        compiler_params=_params("arbitrary", "arbitrary", "arbitrary"),
        name="diff_attn",
    )(jnp.full((1,), lambda_init, F32), proj, proj, proj, proj, lam_q1, lam_k1, lam_q2, lam_k2, sub_g)


def _out_proj_kernel(ys_ref, ya_ref, ws_ref, wa_ref, x_ref, gate_ref, fg_ref, o_ref, *, final):
    pb, cb, width = ys_ref.shape
    rows = pb * cb
    y = jnp.dot(ys_ref[...].reshape(rows, width), ws_ref[...], preferred_element_type=F32)
    y = y + jnp.dot(ya_ref[...].reshape(rows, width), wa_ref[...], preferred_element_type=F32)
    xn = x_ref[...].reshape(rows, -1) + gate_ref[...] * y
    if final:
        xn = xn * lax.rsqrt(jnp.mean(xn * xn, axis=-1, keepdims=True) + NORM_EPS) * fg_ref[...]
        o_ref[...] = _natural(xn.reshape(pb, cb, -1))
    else:
        o_ref[...] = xn.reshape(pb, cb, -1)


def _out_proj(ys, ya, w_out_b, x, gate, final_g, layer, final):
    bsz, _, nc, d = x.shape
    width = ys.shape[-1]
    if final:
        pb, cb = CHUNK, TOKEN_TILE // CHUNK
        out_spec = pl.BlockSpec((None, TOKEN_TILE, d), lambda b, p, m: (b, m, 0))
        out_shape = jax.ShapeDtypeStruct((bsz, nc * CHUNK, d), F32)
    else:
        pb, cb = TOKEN_TILE // nc, nc
        out_spec = pl.BlockSpec((None, pb, cb, d), lambda b, p, m: (b, p, m, 0))
        out_shape = jax.ShapeDtypeStruct(x.shape, F32)
    tile = lambda last: pl.BlockSpec((None, pb, cb, last), lambda b, p, m: (b, p, m, 0))
    return pl.pallas_call(
        functools.partial(_out_proj_kernel, final=final),
        grid=(bsz, CHUNK // pb, nc // cb),
        in_specs=[tile(width), tile(width),
                  pl.BlockSpec((None, width, d), lambda b, p, m: (layer, 0, 0)),
                  pl.BlockSpec((None, width, d), lambda b, p, m: (layer, 1, 0)),
                  tile(d),
                  pl.BlockSpec((None, 1, d), lambda b, p, m: (b, 0, 0)),
                  pl.BlockSpec((1, d), lambda b, p, m: (0, 0))],
        out_specs=out_spec,
        out_shape=out_shape,
        compiler_params=_params("arbitrary", "arbitrary", "arbitrary"),
        name="out_proj",
    )(ys, ya, w_out_b, w_out_b, x, gate, final_g)


def _to_phase_major_kernel(x_ref, o_ref):
    rows, d = x_ref.shape
    o_ref[...] = jnp.swapaxes(x_ref[...].reshape(rows // CHUNK, CHUNK, d), 0, 1)


def _to_phase_major(x):
    bsz, s, d = x.shape
    tc = TOKEN_TILE // CHUNK
    return pl.pallas_call(
        _to_phase_major_kernel,
        grid=(bsz, s // TOKEN_TILE),
        in_specs=[pl.BlockSpec((None, TOKEN_TILE, d), lambda b, m: (b, m, 0))],
        out_specs=pl.BlockSpec((None, CHUNK, tc, d), lambda b, m: (b, 0, m, 0)),
        out_shape=jax.ShapeDtypeStruct((bsz, CHUNK, s // CHUNK, d), F32),
        compiler_params=_params("arbitrary", "arbitrary"),
        name="to_phase_major",
    )(x)


def _qk_relayout(w):
    half = HEAD_DIM // 2
    lead = w.shape[:-1]
    w = w.reshape(*lead, -1, 2, 2, 2, half)
    nd = len(lead)
    w = w.transpose(*range(nd), nd, nd + 3, nd + 1, nd + 2, nd + 4)
    return w.reshape(*lead, -1)


def _tile(n, target):
    t = min(n, target)
    while n % t or t % LANES:
        t -= LANES
    return t


def kernel(x, c, positions, norm_g, w_ada, b_ada, w_in, w_out, ssm_a_re, ssm_a_im, ssm_b_re, ssm_b_im,
           ssm_c_re, ssm_c_im, ssm_d, ssm_log_step, w_glu, b_glu, lam_q1, lam_k1, lam_q2, lam_k2,
           sub_g, final_g):
    bsz, s, d = x.shape
    depth = w_in.shape[0]
    width = d // 2
    nchunks = s // CHUNK
    assert w_in.shape[-1] == 6 * width and width % PAIR == 0 and bsz % SUBLANES == 0
    assert s % TOKEN_TILE == 0 and TOKEN_TILE % nchunks == 0

    w_in_p = jnp.concatenate(
        [w_in[:, :, width:2 * width].astype(BF16), _qk_relayout(w_in[:, :, 2 * width:3 * width]).astype(BF16),
         _qk_relayout(w_in[:, :, 3 * width:4 * width]).astype(BF16), w_in[:, :, 4 * width:].astype(BF16)], axis=-1)
    w_ut = jnp.swapaxes(lax.optimization_barrier(w_in[:, :, :width]), 1, 2).astype(BF16)
    w_out_b = w_out.astype(BF16)
    w_glu_b = w_glu.astype(BF16)
    toep, bst, cst, avec = _s5_tables(ssm_a_re, ssm_a_im, ssm_b_re, ssm_b_im, ssm_c_re, ssm_c_im,
                                      ssm_d, ssm_log_step)
    half = HEAD_DIM // 2
    inv_freq = jnp.tile(ROPE_THETA ** (-jnp.arange(half, dtype=F32) / half), LANES // half)
    pos_pm = jnp.swapaxes(positions.reshape(bsz, nchunks, CHUNK), 1, 2)
    ang = pos_pm.astype(F32)[..., None] * inv_freq
    cos_t, sin_t = jnp.cos(ang), jnp.sin(ang)

    mod = _ada_mod(c, w_ada, b_ada).reshape(depth, bsz, 1, 3 * d)
    norm_g3 = norm_g.reshape(depth, 1, d)
    b_glu3 = b_glu.reshape(depth, 1, 2 * width)
    lam3 = [v.reshape(depth, 1, HEAD_DIM) for v in (lam_q1, lam_k1, lam_q2, lam_k2)]
    sub_g3 = sub_g.reshape(depth, 1, 2 * HEAD_DIM)
    final_g2 = final_g.reshape(1, d)

    x = _to_phase_major(x)
    for l in range(depth):
        lambda_init = 0.8 - 0.6 * math.exp(-0.3 * l)
        shift, scale, gate = mod[l, :, :, :d], mod[l, :, :, d:2 * d], mod[l, :, :, 2 * d:]
        at, proj = _in_proj(x, shift, scale, norm_g3, w_in_p, w_ut, l, cos_t, sin_t)
        gt = _s5_core(at, toep, bst, cst, avec, l)
        ys = _glu(gt, proj, w_glu_b, b_glu3, l)
        ya = _attention(proj, *lam3, sub_g3, l, lambda_init, TOKEN_TILE)
        x = _out_proj(ys, ya, w_out_b, x, gate, final_g2, l, l == depth - 1)
    return x
```

```python
import functools
import math

import jax
import jax.numpy as jnp
from jax import lax
from jax.experimental import pallas as pl
from jax.experimental.pallas import tpu as pltpu

LANES = 128
SUBLANES = 8
V7X_VMEM_REQUEST_BYTES = 60 * 1024 * 1024

SSM_GROUP = 16
SSM_STATE = 64
HEAD_DIM = 64
ROPE_THETA = 10000.0
NORM_EPS = 1e-6
CHUNK = 16
PAIR = 4 * HEAD_DIM
TK = 256
TOKEN_TILE = 512
ONES_ROWS = 16
IN_PROJ_PHASES = 4
ATTN_TILES_PER_STEP = 4
Q_SCALE = HEAD_DIM ** -0.5 * math.log2(math.e)

F32 = jnp.float32
BF16 = jnp.bfloat16


def _params(*sem):
    return pltpu.CompilerParams(dimension_semantics=sem, vmem_limit_bytes=V7X_VMEM_REQUEST_BYTES)


def _ada_kernel(c_ref, w_ref, b_ref, o_ref):
    c = c_ref[...]
    act = c * jax.nn.sigmoid(c)
    o_ref[...] = jnp.dot(act, w_ref[...], preferred_element_type=F32,
                         precision=lax.Precision.HIGHEST) + b_ref[...]


def _ada_mod(c, w_ada, b_ada):
    depth, d, n3 = w_ada.shape
    bsz = c.shape[0]
    tn = _tile(n3, 1024)
    return pl.pallas_call(
        _ada_kernel,
        grid=(depth, n3 // tn),
        in_specs=[pl.BlockSpec((bsz, d), lambda l, n: (0, 0)),
                  pl.BlockSpec((None, d, tn), lambda l, n: (l, 0, n)),
                  pl.BlockSpec((None, 1, tn), lambda l, n: (l, 0, n))],
        out_specs=pl.BlockSpec((None, bsz, tn), lambda l, n: (l, 0, n)),
        out_shape=jax.ShapeDtypeStruct((depth, bsz, n3), F32),
        compiler_params=_params("arbitrary", "arbitrary"),
        name="ada_mod",
    )(c, w_ada, b_ada.reshape(depth, 1, n3))


def _in_proj_kernel(x_ref, shift_ref, scale_ref, g_ref, w_ref, wut_ref, cos_ref, sin_ref, at_ref, o_ref, h_ref,
                    *, npairs, nc, ph):
    n = pl.program_id(1)
    ip = pl.program_id(2)
    width = w_ref.shape[-1]

    @pl.when(n == 0)
    def _():
        for t in range(ph):
            xf = x_ref[t]
            y = xf * lax.rsqrt(jnp.mean(xf * xf, axis=-1, keepdims=True) + NORM_EPS) * g_ref[...]
            h = (y * (1.0 + scale_ref[...]) + shift_ref[...]).astype(BF16)
            h_ref[ip, t * nc:(t + 1) * nc, :] = h
            ut = lax.dot_general(wut_ref[...], h, (((1,), (1,)), ((), ())), preferred_element_type=F32)
            at_ref[:, t * SSM_GROUP:(t + 1) * SSM_GROUP, :] = ut.reshape(-1, SSM_GROUP, nc).astype(BF16)

    @pl.when(n > 0)
    def _():
        is_rope = jnp.logical_or(n == 2, n == 3)
        qs = jnp.where(n == 2, Q_SCALE, 1.0).astype(F32)
        cs = [jnp.where(is_rope, cos_ref[t] * qs, 1.0) for t in range(ph)]
        sn = [jnp.where(is_rope, sin_ref[t] * qs, 0.0) for t in range(ph)]
        h = h_ref[ip]
        for j in range(npairs):
            lo = j * PAIR
            acc = jnp.dot(h, w_ref[:, lo:lo + PAIR], preferred_element_type=F32)
            for t in range(ph):
                t1 = acc[t * nc:(t + 1) * nc, :LANES]
                t2 = acc[t * nc:(t + 1) * nc, LANES:]
                o_ref[t, :, lo:lo + LANES] = (t1 * cs[t] - t2 * sn[t]).astype(BF16)
                o_ref[t, :, lo + LANES:lo + PAIR] = (t2 * cs[t] + t1 * sn[t]).astype(BF16)


def _in_proj(x, shift, scale, norm_g, w_in_p, w_ut, layer, cos_t, sin_t):
    bsz, _, nc, d = x.shape
    width = w_in_p.shape[-1] // 5
    npairs = width // PAIR
    groups = width // SSM_GROUP
    ph = IN_PROJ_PHASES
    nip = CHUNK // ph
    seg = lambda n: jnp.maximum(n, 1)
    u_step = lambda n, ip: jnp.where(n == 0, ip, nip - 1)
    o_step = lambda n, ip: jnp.where(n == 0, 0, ip)
    return pl.pallas_call(
        functools.partial(_in_proj_kernel, npairs=npairs, nc=nc, ph=ph),
        grid=(bsz, 6, nip),
        in_specs=[pl.BlockSpec((None, ph, nc, d), lambda b, n, ip: (b, u_step(n, ip), 0, 0)),
                  pl.BlockSpec((None, 1, d), lambda b, n, ip: (b, 0, 0)),
                  pl.BlockSpec((None, 1, d), lambda b, n, ip: (b, 0, 0)),
                  pl.BlockSpec((None, 1, d), lambda b, n, ip: (layer, 0, 0)),
                  pl.BlockSpec((None, d, width), lambda b, n, ip: (layer, 0, seg(n) - 1)),
                  pl.BlockSpec((None, width, d), lambda b, n, ip: (layer, 0, 0), pipeline_mode=pl.Buffered(1)),
                  pl.BlockSpec((None, ph, nc, LANES), lambda b, n, ip: (b, ip, 0, 0)),
                  pl.BlockSpec((None, ph, nc, LANES), lambda b, n, ip: (b, ip, 0, 0))],
        out_specs=[pl.BlockSpec((groups, None, ph * SSM_GROUP, nc), lambda b, n, ip: (0, b, u_step(n, ip), 0)),
                   pl.BlockSpec((None, None, ph, nc, width), lambda b, n, ip: (seg(n) - 1, b, o_step(n, ip), 0, 0))],
        out_shape=[jax.ShapeDtypeStruct((groups, bsz, CHUNK * SSM_GROUP, nc), BF16),
                   jax.ShapeDtypeStruct((5, bsz, CHUNK, nc, width), BF16)],
        scratch_shapes=[pltpu.VMEM((nip, ph * nc, d), BF16)],
        compiler_params=_params("arbitrary", "arbitrary", "arbitrary"),
        name="in_proj",
    )(x, shift, scale, norm_g, w_in_p, w_ut, cos_t, sin_t)


def _gelu_tanh(y):
    return 0.5 * y * (1.0 + jnp.tanh(math.sqrt(2.0 / math.pi) * (y + 0.044715 * (y * y * y))))


def _s5_kernel(at_ref, toep_ref, bst_ref, cst_ref, av_ref, gt_ref, a_ref, sb_ref, xp_ref, *, bsz, nchunks):
    ns2 = 2 * SSM_STATE
    kk = CHUNK * SSM_GROUP
    for b in range(bsz):
        a_ref[b * nchunks:(b + 1) * nchunks, :] = at_ref[b].T
    a = a_ref[...]
    sb = jnp.dot(a, bst_ref[...], preferred_element_type=F32)
    sb_ref[...] = jnp.swapaxes(sb.reshape(bsz, nchunks, 2 * ns2), 0, 1).reshape(nchunks * bsz, 2 * ns2)
    ar = av_ref[0:1, :]
    ai1 = av_ref[1:2, :]
    ai2 = av_ref[2:3, :]

    def body(c, carry):
        xs, xw = carry
        r = pl.multiple_of(c * bsz, bsz)
        xp_ref[pl.ds(r, bsz), :] = xs
        sb = sb_ref[pl.ds(r, bsz), :]
        return (ar * xs + ai1 * xw + sb[:, :ns2], ar * xw + ai2 * xs + sb[:, ns2:])

    zero = jnp.zeros((bsz, ns2), F32)
    lax.fori_loop(0, nchunks, body, (zero, zero), unroll=8)
    xp = jnp.swapaxes(xp_ref[...].reshape(nchunks, bsz, ns2), 0, 1).reshape(bsz * nchunks, ns2)
    y = jnp.dot(a, toep_ref[...], preferred_element_type=F32)
    y = y + jnp.dot(xp.astype(BF16), cst_ref[...], preferred_element_type=F32)
    gy = _gelu_tanh(y).astype(BF16)
    for b in range(bsz):
        gt_ref[b] = gy[b * nchunks:(b + 1) * nchunks, :].T


def _s5_core(at, toep, bst, cst, avec, layer):
    g, bsz, kk, nchunks = at.shape
    ns2 = 2 * SSM_STATE
    m = bsz * nchunks
    return pl.pallas_call(
        functools.partial(_s5_kernel, bsz=bsz, nchunks=nchunks),
        grid=(g,),
        in_specs=[pl.BlockSpec((None, bsz, kk, nchunks), lambda i: (i, 0, 0, 0)),
                  pl.BlockSpec((None, None, kk, kk), lambda i: (layer, i, 0, 0)),
                  pl.BlockSpec((None, None, kk, 2 * ns2), lambda i: (layer, i, 0, 0)),
                  pl.BlockSpec((None, None, ns2, kk), lambda i: (layer, i, 0, 0)),
                  pl.BlockSpec((None, None, 4, ns2), lambda i: (layer, i, 0, 0))],
        out_specs=pl.BlockSpec((None, bsz, kk, nchunks), lambda i: (i, 0, 0, 0)),
        out_shape=jax.ShapeDtypeStruct((g, bsz, kk, nchunks), BF16),
        scratch_shapes=[pltpu.VMEM((m, kk), BF16), pltpu.VMEM((m, 2 * ns2), F32), pltpu.VMEM((m, ns2), F32)],
        compiler_params=_params("arbitrary"),
        name="s5_core",
    )(at, toep, bst, cst, avec)


def _s5_tables(a_re, a_im, b_re, b_im, c_re, c_im, d_skip, log_step):
    step = jnp.exp(log_step)[..., None]
    lr, li = a_re * step, a_im * step
    tau = jnp.arange(CHUNK + 1, dtype=F32)[:, None]
    mag = jnp.exp(tau * lr[..., None, :])
    pw_re = mag * jnp.cos(tau * li[..., None, :])
    pw_im = mag * jnp.sin(tau * li[..., None, :])
    num_re, num_im = pw_re[..., 1, :] - 1.0, pw_im[..., 1, :]
    den = a_re * a_re + a_im * a_im
    cf_re = (num_re * a_re + num_im * a_im) / den
    cf_im = (num_im * a_re - num_re * a_im) / den
    bt_re, bt_im = jnp.swapaxes(b_re, -1, -2), jnp.swapaxes(b_im, -1, -2)
    bb_re = cf_re[..., None, :] * bt_re - cf_im[..., None, :] * bt_im
    bb_im = cf_re[..., None, :] * bt_im + cf_im[..., None, :] * bt_re
    e_re = pw_re[..., None, :] * bb_re[..., None, :, :] - pw_im[..., None, :] * bb_im[..., None, :, :]
    e_im = pw_re[..., None, :] * bb_im[..., None, :, :] + pw_im[..., None, :] * bb_re[..., None, :, :]
    cn_re, cn_im = jnp.moveaxis(c_re, -1, 2), jnp.moveaxis(c_im, -1, 2)
    en_re, en_im = jnp.moveaxis(e_re, -1, 2), jnp.moveaxis(e_im, -1, 2)
    kern = jnp.sum(cn_re[:, :, :, None, :, None] * en_re[:, :, :, :, None, :]
                   - cn_im[:, :, :, None, :, None] * en_im[:, :, :, :, None, :], axis=2)
    eye = jnp.eye(SSM_GROUP, dtype=F32)
    kern = kern.at[:, :, 0].add(d_skip[..., :, None] * eye)
    idx = jnp.arange(CHUNK)
    lag = idx[None, :] - idx[:, None]
    blk = jnp.where((lag >= 0)[..., None, None], kern[:, :, jnp.clip(lag, 0)], 0.0)
    dp, g = a_re.shape[:2]
    kk = CHUNK * SSM_GROUP
    toep = blk.transpose(0, 1, 2, 5, 3, 4).reshape(dp, g, kk, kk)
    back = CHUNK - 1 - idx
    bs_re = e_re[:, :, back].reshape(dp, g, kk, SSM_STATE)
    bs_im = e_im[:, :, back].reshape(dp, g, kk, SSM_STATE)
    bst = jnp.concatenate([bs_re, bs_im, bs_im, bs_re], axis=-1)
    fw_re, fw_im = pw_re[:, :, 1:], pw_im[:, :, 1:]
    ca_re = c_re[:, :, None] * fw_re[..., None, :] - c_im[:, :, None] * fw_im[..., None, :]
    ca_im = c_re[:, :, None] * fw_im[..., None, :] + c_im[:, :, None] * fw_re[..., None, :]
    cst = jnp.concatenate([ca_re.transpose(0, 1, 4, 2, 3).reshape(dp, g, SSM_STATE, kk),
                           -ca_im.transpose(0, 1, 4, 2, 3).reshape(dp, g, SSM_STATE, kk)], axis=2)
    ar, ai = pw_re[:, :, CHUNK], pw_im[:, :, CHUNK]
    avec = jnp.stack([jnp.concatenate([ar, ar], -1), jnp.concatenate([-ai, ai], -1),
                      jnp.concatenate([ai, -ai], -1), jnp.zeros_like(jnp.concatenate([ar, ar], -1))], axis=2)
    return toep.astype(BF16), bst.astype(BF16), cst.astype(BF16), avec


def _glu_kernel(gt_ref, zs_ref, w_ref, b_ref, o_ref, *, width, nc):
    gt = gt_ref[...]
    gy = jnp.concatenate([gt[:, t * SSM_GROUP:(t + 1) * SSM_GROUP, :].reshape(width, nc).T for t in range(2)],
                         axis=0)
    for lo in range(0, width, PAIR):
        a = jnp.dot(gy, w_ref[:, lo:lo + PAIR], preferred_element_type=F32) + b_ref[:, lo:lo + PAIR]
        g = jnp.dot(gy, w_ref[:, width + lo:width + lo + PAIR], preferred_element_type=F32)
        out = a * jax.nn.sigmoid(g + b_ref[:, width + lo:width + lo + PAIR])
        for t in range(2):
            z = zs_ref[t, :, lo:lo + PAIR].astype(F32)
            o_ref[t, :, lo:lo + PAIR] = (out[t * nc:(t + 1) * nc, :] * (z * jax.nn.sigmoid(z))).astype(BF16)


def _glu(gt, proj, w_glu_b, b_glu, layer):
    groups, bsz, kk, nc = gt.shape
    width = groups * SSM_GROUP
    return pl.pallas_call(
        functools.partial(_glu_kernel, width=width, nc=nc),
        grid=(bsz, CHUNK // 2),
        in_specs=[pl.BlockSpec((groups, None, 2 * SSM_GROUP, nc), lambda b, ip: (0, b, ip, 0)),
                  pl.BlockSpec((None, None, 2, nc, width), lambda b, ip: (0, b, ip, 0, 0)),
                  pl.BlockSpec((None, width, 2 * width), lambda b, ip: (layer, 0, 0)),
                  pl.BlockSpec((None, 1, 2 * width), lambda b, ip: (layer, 0, 0))],
        out_specs=pl.BlockSpec((None, 2, nc, width), lambda b, ip: (b, ip, 0, 0)),
        out_shape=jax.ShapeDtypeStruct((bsz, CHUNK, nc, width), BF16),
        compiler_params=_params("arbitrary", "arbitrary"),
        name="s5_glu",
    )(gt, proj, w_glu_b, b_glu)


def _attn_kernel(linit_ref, q_ref, k_ref, v_ref, za_ref, lq1_ref, lk1_ref, lq2_ref, lk2_ref, sg_ref,
                 o_ref, km_ref, vt_ref, *scratch, tq, tiles):
    step = pl.program_id(2)

    @pl.when(step == 0)
    def _():
        kk = _natural(k_ref[...])
        grp = (lax.broadcasted_iota(jnp.int32, kk.shape, 1) % LANES) // (HEAD_DIM // 2)
        for combo in range(4):
            km_ref[combo] = jnp.where(grp == combo, kk, jnp.zeros_like(kk))
        vt = _natural(v_ref[...]).T
        for hp in range(2):
            vt_ref[hp, :LANES, :] = vt[hp * LANES:(hp + 1) * LANES]
            vt_ref[hp, LANES:, :] = jnp.ones((ONES_ROWS, vt.shape[1]), BF16)

    tc = tq // CHUNK
    for u in range(tiles):
        rows = slice(u * tc, (u + 1) * tc)
        _attn_tile(step * tiles + u, linit_ref, q_ref.at[:, rows, :], za_ref.at[:, rows, :], lq1_ref, lk1_ref,
                   lq2_ref, lk2_ref, sg_ref, o_ref.at[:, rows, :], km_ref, vt_ref, *scratch, tq=tq,
                   may_be_first=(u == 0))


def _attn_tile(qi, linit_ref, q_ref, za_ref, lq1_ref, lk1_ref, lq2_ref, lk2_ref, sg_ref, o_ref, km_ref, vt_ref,
               qt_ref, m_ref, l_ref, acc_ref, s0_ref, s1_ref, p0_ref, p1_ref, a0_ref, a1_ref, c0_ref, c1_ref,
               *, tq, may_be_first):
    nq = tq // TK
    qt_ref[...] = _natural(q_ref[...]).T
    m_ref[...] = jnp.full(m_ref.shape, -jnp.inf, F32)
    l_ref[...] = jnp.zeros(l_ref.shape, F32)
    acc_ref[...] = jnp.zeros(acc_ref.shape, F32)

    def scores(kt, s_ref, cm_ref, c0):
        ks = pl.multiple_of(kt * TK, TK)
        qt = qt_ref[:, c0:]
        for combo in range(4):
            s = jnp.dot(km_ref[combo, pl.ds(ks, TK), :], qt, preferred_element_type=F32)
            for c in range(c0, tq, LANES):
                s_ref[combo, c // LANES] = s[:, c - c0:c - c0 + LANES]
            cm_ref[combo, :, c0:] = jnp.max(s, axis=0, keepdims=True)

    def softmax(s_ref, cm_ref, p_ref, a_ref, c0, shift):
        for combo in range(4):
            for c in range(c0, tq, LANES):
                cs = slice(c, c + LANES)
                if shift is not None:
                    row = lax.broadcasted_iota(jnp.int32, (TK, LANES), 0)
                    col = lax.broadcasted_iota(jnp.int32, (TK, LANES), 1)
                    s = jnp.where(row + (shift - c) <= col, s_ref[combo, c // LANES], -jnp.inf)
                    cmax = jnp.max(s, axis=0, keepdims=True)
                else:
                    s = s_ref[combo, c // LANES]
                    cmax = cm_ref[combo, :, cs]
                m_prev = m_ref[combo, :, cs]
                m_new = jnp.maximum(m_prev, cmax)
                alpha = jnp.exp2(m_prev - m_new)
                p = jnp.exp2(s - m_new)
                p_ref[combo, c // LANES] = p.astype(BF16)
                a_ref[combo, :, cs] = alpha
                m_ref[combo, :, cs] = m_new

    def values(kt, p_ref, a_ref, c0):
        ks = pl.multiple_of(kt * TK, TK)
        cs = slice(c0, tq)
        for combo in range(4):
            hp = combo // 2
            p = jnp.concatenate([p_ref[combo, c // LANES] for c in range(c0, tq, LANES)], axis=1)
            pv = jnp.dot(vt_ref[hp, :, pl.ds(ks, TK)], p, preferred_element_type=F32)
            alpha = a_ref[combo, :, cs]
            acc_ref[combo, :, cs] = alpha * acc_ref[combo, :, cs] + pv[:LANES]
            l_ref[combo, :, cs] = alpha * l_ref[combo, :, cs] + pv[LANES:LANES + 1]

    kd = qi * nq

    def band():
        scores(kd + 1, s1_ref, c1_ref, TK)
        softmax(s0_ref, c0_ref, p0_ref, a0_ref, 0, 0)
        values(kd, p0_ref, a0_ref, 0)
        softmax(s1_ref, c1_ref, p1_ref, a1_ref, TK, TK)
        values(kd + 1, p1_ref, a1_ref, TK)

    def pipelined():
        scores(0, s0_ref, c0_ref, 0)
        scores(1, s1_ref, c1_ref, 0)
        softmax(s0_ref, c0_ref, p0_ref, a0_ref, 0, None)

        def pair(i, carry):
            kt = 2 * i
            scores(kt + 2, s0_ref, c0_ref, 0)
            softmax(s1_ref, c1_ref, p1_ref, a1_ref, 0, None)
            values(kt, p0_ref, a0_ref, 0)
            scores(kt + 3, s1_ref, c1_ref, 0)
            softmax(s0_ref, c0_ref, p0_ref, a0_ref, 0, None)
            values(kt + 1, p1_ref, a1_ref, 0)
            return carry

        lax.fori_loop(0, qi - 1, pair, 0)
        softmax(s1_ref, c1_ref, p1_ref, a1_ref, 0, None)
        values(kd - 2, p0_ref, a0_ref, 0)
        values(kd - 1, p1_ref, a1_ref, 0)
        scores(kd, s0_ref, c0_ref, 0)
        band()

    if may_be_first:
        @pl.when(qi == 0)
        def _():
            scores(0, s0_ref, c0_ref, 0)
            band()

        pl.when(qi > 0)(pipelined)
    else:
        pipelined()

    linit = linit_ref[0]
    lam = (jnp.exp(jnp.sum(lq1_ref[...] * lk1_ref[...], axis=-1, keepdims=True))
           - jnp.exp(jnp.sum(lq2_ref[...] * lk2_ref[...], axis=-1, keepdims=True)) + linit)
    za = _natural(za_ref[...]).astype(F32)
    for hp in range(2):
        r1 = 1.0 / l_ref[2 * hp]
        r2 = lam / l_ref[2 * hp + 1]
        ot = acc_ref[2 * hp] * r1 - acc_ref[2 * hp + 1] * r2
        yt = ot * lax.rsqrt(jnp.mean(ot * ot, axis=0, keepdims=True) + NORM_EPS)
        yt = jnp.concatenate([yt[:, c:c + LANES] * sg_ref[...] for c in range(0, tq, LANES)], axis=1)
        y = (yt * (1.0 - linit)).T
        z = za[:, hp * LANES:(hp + 1) * LANES]
        y = y * (z * jax.nn.sigmoid(z))
        o_ref[:, :, hp * LANES:(hp + 1) * LANES] = jnp.swapaxes(
            y.astype(BF16).reshape(tq // CHUNK, CHUNK, LANES), 0, 1)


def _natural(v):
    return jnp.swapaxes(v, 0, 1).reshape(v.shape[0] * v.shape[1], v.shape[2])


def _attention(proj, lam_q1, lam_k1, lam_q2, lam_k2, sub_g, layer, lambda_init, tq):
    _, bsz, _, nc, width = proj.shape
    s = nc * CHUNK
    tiles = ATTN_TILES_PER_STEP
    tc = tiles * tq // CHUNK
    npairs = width // PAIR
    assert tq == 2 * TK and s % (tiles * tq) == 0
    vec = lambda: pl.BlockSpec((None, 1, HEAD_DIM), lambda b, j, i: (layer, 0, 0))
    sbuf = lambda: pltpu.VMEM((4, tq // LANES, TK, LANES), F32)
    pbuf = lambda: pltpu.VMEM((4, tq // LANES, TK, LANES), BF16)
    rowv = lambda: pltpu.VMEM((4, 1, tq), F32)
    return pl.pallas_call(
        functools.partial(_attn_kernel, tq=tq, tiles=tiles),
        grid=(bsz, npairs, s // (tiles * tq)),
        in_specs=[pl.BlockSpec(memory_space=pltpu.SMEM),
                  pl.BlockSpec((None, None, CHUNK, tc, PAIR), lambda b, j, i: (1, b, 0, i, j)),
                  pl.BlockSpec((None, None, CHUNK, nc, PAIR), lambda b, j, i: (2, b, 0, 0, j)),
                  pl.BlockSpec((None, None, CHUNK, nc, PAIR), lambda b, j, i: (3, b, 0, 0, j)),
                  pl.BlockSpec((None, None, CHUNK, tc, PAIR), lambda b, j, i: (4, b, 0, i, j)),
                  vec(), vec(), vec(), vec(),
                  pl.BlockSpec((None, 2 * HEAD_DIM, LANES), lambda b, j, i: (layer, 0, 0))],
        out_specs=pl.BlockSpec((None, CHUNK, tc, PAIR), lambda b, j, i: (b, 0, i, j)),
        out_shape=jax.ShapeDtypeStruct((bsz, CHUNK, nc, width), BF16),
        scratch_shapes=[pltpu.VMEM((4, s, PAIR), BF16), pltpu.VMEM((2, LANES + ONES_ROWS, s), BF16),
                        pltpu.VMEM((PAIR, tq), BF16),
                        rowv(), rowv(), pltpu.VMEM((4, LANES, tq), F32),
                        sbuf(), sbuf(), pbuf(), pbuf(), rowv(), rowv(), rowv(), rowv()],
        compiler_params=_params("arbitrary", "arbitrary", "arbitrary"),
        name="diff_attn",
    )(jnp.full((1,), lambda_init, F32), proj, proj, proj, proj, lam_q1, lam_k1, lam_q2, lam_k2, sub_g)


def _out_proj_kernel(ys_ref, ya_ref, ws_ref, wa_ref, x_ref, gate_ref, fg_ref, o_ref, *, final):
    pb, cb, width = ys_ref.shape
    rows = pb * cb
    y = jnp.dot(ys_ref[...].reshape(rows, width), ws_ref[...], preferred_element_type=F32)
    y = y + jnp.dot(ya_ref[...].reshape(rows, width), wa_ref[...], preferred_element_type=F32)
    xn = x_ref[...].reshape(rows, -1) + gate_ref[...] * y
    if final:
        xn = xn * lax.rsqrt(jnp.mean(xn * xn, axis=-1, keepdims=True) + NORM_EPS) * fg_ref[...]
        o_ref[...] = _natural(xn.reshape(pb, cb, -1))
    else:
        o_ref[...] = xn.reshape(pb, cb, -1)


def _out_proj(ys, ya, w_out_b, x, gate, final_g, layer, final):
    bsz, _, nc, d = x.shape
    width = ys.shape[-1]
    if final:
        pb, cb = CHUNK, TOKEN_TILE // CHUNK
        out_spec = pl.BlockSpec((None, TOKEN_TILE, d), lambda b, p, m: (b, m, 0))
        out_shape = jax.ShapeDtypeStruct((bsz, nc * CHUNK, d), F32)
    else:
        pb, cb = TOKEN_TILE // nc, nc
        out_spec = pl.BlockSpec((None, pb, cb, d), lambda b, p, m: (b, p, m, 0))
        out_shape = jax.ShapeDtypeStruct(x.shape, F32)
    tile = lambda last: pl.BlockSpec((None, pb, cb, last), lambda b, p, m: (b, p, m, 0))
    return pl.pallas_call(
        functools.partial(_out_proj_kernel, final=final),
        grid=(bsz, CHUNK // pb, nc // cb),
        in_specs=[tile(width), tile(width),
                  pl.BlockSpec((None, width, d), lambda b, p, m: (layer, 0, 0)),
                  pl.BlockSpec((None, width, d), lambda b, p, m: (layer, 1, 0)),
                  tile(d),
                  pl.BlockSpec((None, 1, d), lambda b, p, m: (b, 0, 0)),
                  pl.BlockSpec((1, d), lambda b, p, m: (0, 0))],
        out_specs=out_spec,
        out_shape=out_shape,
        compiler_params=_params("arbitrary", "arbitrary", "arbitrary"),
        name="out_proj",
    )(ys, ya, w_out_b, w_out_b, x, gate, final_g)


def _to_phase_major_kernel(x_ref, o_ref):
    rows, d = x_ref.shape
    o_ref[...] = jnp.swapaxes(x_ref[...].reshape(rows // CHUNK, CHUNK, d), 0, 1)


def _to_phase_major(x):
    bsz, s, d = x.shape
    tc = TOKEN_TILE // CHUNK
    return pl.pallas_call(
        _to_phase_major_kernel,
        grid=(bsz, s // TOKEN_TILE),
        in_specs=[pl.BlockSpec((None, TOKEN_TILE, d), lambda b, m: (b, m, 0))],
        out_specs=pl.BlockSpec((None, CHUNK, tc, d), lambda b, m: (b, 0, m, 0)),
        out_shape=jax.ShapeDtypeStruct((bsz, CHUNK, s // CHUNK, d), F32),
        compiler_params=_params("arbitrary", "arbitrary"),
        name="to_phase_major",
    )(x)


def _qk_relayout(w):
    half = HEAD_DIM // 2
    lead = w.shape[:-1]
    w = w.reshape(*lead, -1, 2, 2, 2, half)
    nd = len(lead)
    w = w.transpose(*range(nd), nd, nd + 3, nd + 1, nd + 2, nd + 4)
    return w.reshape(*lead, -1)


def _tile(n, target):
    t = min(n, target)
    while n % t or t % LANES:
        t -= LANES
    return t


def kernel(x, c, positions, norm_g, w_ada, b_ada, w_in, w_out, ssm_a_re, ssm_a_im, ssm_b_re, ssm_b_im,
           ssm_c_re, ssm_c_im, ssm_d, ssm_log_step, w_glu, b_glu, lam_q1, lam_k1, lam_q2, lam_k2,
           sub_g, final_g):
    bsz, s, d = x.shape
    depth = w_in.shape[0]
    width = d // 2
    nchunks = s // CHUNK
    assert w_in.shape[-1] == 6 * width and width % PAIR == 0 and bsz % SUBLANES == 0
    assert s % TOKEN_TILE == 0 and TOKEN_TILE % nchunks == 0

    w_in_p = jnp.concatenate(
        [w_in[:, :, width:2 * width].astype(BF16), _qk_relayout(w_in[:, :, 2 * width:3 * width]).astype(BF16),
         _qk_relayout(w_in[:, :, 3 * width:4 * width]).astype(BF16), w_in[:, :, 4 * width:].astype(BF16)], axis=-1)
    w_ut = jnp.swapaxes(lax.optimization_barrier(w_in[:, :, :width]), 1, 2).astype(BF16)
    w_out_b = w_out.astype(BF16)
    w_glu_b = w_glu.astype(BF16)
    toep, bst, cst, avec = _s5_tables(ssm_a_re, ssm_a_im, ssm_b_re, ssm_b_im, ssm_c_re, ssm_c_im,
                                      ssm_d, ssm_log_step)
    half = HEAD_DIM // 2
    inv_freq = jnp.tile(ROPE_THETA ** (-jnp.arange(half, dtype=F32) / half), LANES // half)
    pos_pm = jnp.swapaxes(positions.reshape(bsz, nchunks, CHUNK), 1, 2)
    ang = pos_pm.astype(F32)[..., None] * inv_freq
    cos_t, sin_t = jnp.cos(ang), jnp.sin(ang)

    mod = _ada_mod(c, w_ada, b_ada).reshape(depth, bsz, 1, 3 * d)
    norm_g3 = norm_g.reshape(depth, 1, d)
    b_glu3 = b_glu.reshape(depth, 1, 2 * width)
    lam3 = [v.reshape(depth, 1, HEAD_DIM) for v in (lam_q1, lam_k1, lam_q2, lam_k2)]
    sub_g3 = jnp.broadcast_to(sub_g[:, :, None], (depth, 2 * HEAD_DIM, LANES))
    final_g2 = final_g.reshape(1, d)

    x = _to_phase_major(x)
    for l in range(depth):
        lambda_init = 0.8 - 0.6 * math.exp(-0.3 * l)
        shift, scale, gate = mod[l, :, :, :d], mod[l, :, :, d:2 * d], mod[l, :, :, 2 * d:]
        at, proj = _in_proj(x, shift, scale, norm_g3, w_in_p, w_ut, l, cos_t, sin_t)
        gt = _s5_core(at, toep, bst, cst, avec, l)
        ys = _glu(gt, proj, w_glu_b, b_glu3, l)
        ya = _attention(proj, *lam3, sub_g3, l, lambda_init, TOKEN_TILE)
        x = _out_proj(ys, ya, w_out_b, x, gate, final_g2, l, l == depth - 1)
    return x
```
